```python
import jax
import jax.numpy as jnp
from jax import lax
import numpy as np


D_MODEL = 4096
BATCH = 1
SEQ = 8192
DEPTH = 2
DEC_BATCH = 16
DEC_SEQ = 64
PAST_LEN = 2048

CHUNK = 64
N_META = 16
EPS = 1e-6
A_HEADS = 16
A_KV_HEADS = 4
A_HEAD_DIM = 128
A_WIDTH = A_HEADS * A_HEAD_DIM
A_KV_WIDTH = A_KV_HEADS * A_HEAD_DIM
IDX_HEADS = 16
IDX_DIM = 128
IDX_Q_WIDTH = IDX_HEADS * IDX_DIM
INDEX_SCALE = (IDX_HEADS * IDX_DIM) ** -0.5
TOPK_MAX = 256
Q_BLOCK = 128
B_WIDTH = D_MODEL // 2
B_CONV = 31
C_WIDTH = D_MODEL
C_CONV = 3
N_EVEN = (DEPTH + 1) // 2
N_ODD = DEPTH // 2
E_IN_SIZES = (A_WIDTH, A_KV_WIDTH, A_KV_WIDTH, IDX_Q_WIDTH, IDX_DIM, IDX_HEADS, B_WIDTH, B_WIDTH, A_WIDTH + B_WIDTH)
E_IN_WIDTH = sum(E_IN_SIZES)
O_IN_SIZES = (C_WIDTH, C_WIDTH, C_WIDTH, C_WIDTH)
O_IN_WIDTH = sum(O_IN_SIZES)

kernel_name = 'hybrid_dsa_conformer_shortconv_stream_step'


def rms_norm(x, g):
    xf = x.astype(jnp.float32)
    y = xf * lax.rsqrt(jnp.mean(xf * xf, axis=-1, keepdims=True) + EPS)
    return (y * g.astype(jnp.float32)).astype(x.dtype)


def layer_norm(x, g, b):
    xf = x.astype(jnp.float32)
    xc = xf - jnp.mean(xf, axis=-1, keepdims=True)
    y = xc * lax.rsqrt(jnp.mean(xc * xc, axis=-1, keepdims=True) + EPS)
    return (y * g.astype(jnp.float32) + b.astype(jnp.float32)).astype(x.dtype)


def split_cols(y, sizes):
    return jnp.split(y, np.cumsum(sizes)[:-1].tolist(), axis=-1)


def causal_dwconv(u, hist, w, b):
    full = jnp.concatenate([hist, u], axis=1)
    width = w.shape[0]
    y = lax.conv_general_dilated(full, w[:, None, :], window_strides=(1,), padding='VALID',
                                 dimension_numbers=('NWC', 'WIO', 'NWC'),
                                 feature_group_count=u.shape[-1])
    return y + b, full[:, full.shape[1] - (width - 1):]


def dsa_attend(q, qi, wi, qc, k, v, ki, kc, topk):
    nb, nq, nh, dh = q.shape
    s_idx = jax.nn.relu(jnp.einsum('bqhd,bld->bqhl', qi, ki).astype(jnp.float32))
    score = jnp.einsum('bqh,bqhl->bql', wi.astype(jnp.float32), s_idx)
    adm = kc[:, None, :] <= qc[:, :, None]
    score = jnp.where(adm, score, -jnp.inf)
    _, sel = lax.top_k(score, topk)
    valid = jnp.take_along_axis(adm, sel, axis=-1)
    bi = jnp.arange(nb)[:, None, None]
    kg = k[bi, sel]
    vg = v[bi, sel]
    qg = q.reshape(nb, nq, A_KV_HEADS, nh // A_KV_HEADS, dh)
    s = jnp.einsum('bqgrd,bqkgd->bqgrk', qg, kg).astype(jnp.float32) * (dh ** -0.5)
    s = jnp.where(valid[:, :, None, None, :], s, -jnp.inf)
    p = jax.nn.softmax(s, axis=-1).astype(vg.dtype)
    o = jnp.einsum('bqgrk,bqkgd->bqgrd', p, vg)
    return o.reshape(nb, nq, nh * dh)


def even_layer(x, hist_k, hist_v, hist_ki, hist_u, kc, qc, topk,
               norm_pre, w_in, kln_g, kln_b, bconv_w, bconv_b, bln_g, bln_b, w_out, norm_post):
    nb, nt, _ = x.shape
    h = rms_norm(x, norm_pre)
    q, k, v, qi, ki, wi, bval, bglu, gate = split_cols(h @ w_in, E_IN_SIZES)
    q = q.reshape(nb, nt, A_HEADS, A_HEAD_DIM)
    k = k.reshape(nb, nt, A_KV_HEADS, A_HEAD_DIM)
    v = v.reshape(nb, nt, A_KV_HEADS, A_HEAD_DIM)
    qi = qi.reshape(nb, nt, IDX_HEADS, IDX_DIM)
    ki = layer_norm(ki, kln_g, kln_b)
    wi = wi * INDEX_SCALE
    k_all = jnp.concatenate([hist_k, k], axis=1)
    v_all = jnp.concatenate([hist_v, v], axis=1)
    ki_all = jnp.concatenate([hist_ki, ki], axis=1)
    if nt > Q_BLOCK:
        n_blk = -(-nt // Q_BLOCK)
        pad = n_blk * Q_BLOCK - nt

        def to_blocks(a):
            a = jnp.pad(a, [(0, 0), (0, pad)] + [(0, 0)] * (a.ndim - 2))
            return jnp.moveaxis(a.reshape((nb, n_blk, Q_BLOCK) + a.shape[2:]), 1, 0)

        oa = lax.map(lambda blk: dsa_attend(blk[0], blk[1], blk[2], blk[3], k_all, v_all, ki_all, kc, topk),
                     (to_blocks(q), to_blocks(qi), to_blocks(wi), to_blocks(qc)))
        oa = jnp.moveaxis(oa, 0, 1).reshape(nb, n_blk * Q_BLOCK, A_WIDTH)[:, :nt]
    else:
        oa = dsa_attend(q, qi, wi, qc, k_all, v_all, ki_all, kc, topk)
    u = bval * jax.nn.sigmoid(bglu)
    c, new_u = causal_dwconv(u, hist_u, bconv_w, bconv_b)
    ob = jax.nn.silu(layer_norm(c, bln_g, bln_b))
    o = jnp.concatenate([oa, ob], axis=-1) * jax.nn.silu(gate)
    x = x + rms_norm(o @ w_out, norm_post)
    return x, k, v, ki, new_u


def odd_layer(x, hist_z, norm_pre, w_in, conv_w, conv_b, w_out, norm_post):
    h = rms_norm(x, norm_pre)
    hx, cg, bg, gate = split_cols(h @ w_in, O_IN_SIZES)
    cz, new_z = causal_dwconv(cg * hx, hist_z, conv_w, conv_b)
    o = bg * cz * jax.nn.silu(gate)
    x = x + rms_norm(o @ w_out, norm_post)
    return x, new_z


def setup_inputs(seed: int = 0) -> dict:
    key = jax.random.key(seed)
    ks = jax.random.split(key, 24)

    def nrm(i, shape, scale=1.0):
        return scale * jax.random.normal(ks[i], shape, jnp.float32)

    D = D_MODEL
    return {
        'x_prompt': nrm(0, (BATCH, SEQ, D)),
        'x_sample': nrm(1, (DEC_BATCH, DEC_SEQ, D)),
        'cache_a_k': nrm(2, (N_EVEN, DEC_BATCH, PAST_LEN, A_KV_HEADS, A_HEAD_DIM)),
        'cache_a_v': nrm(3, (N_EVEN, DEC_BATCH, PAST_LEN, A_KV_HEADS, A_HEAD_DIM)),
        'cache_a_kidx': nrm(4, (N_EVEN, DEC_BATCH, PAST_LEN, IDX_DIM)),
        'state_b_conv': nrm(5, (N_EVEN, DEC_BATCH, B_CONV - 1, B_WIDTH), 0.5),
        'state_c_conv': nrm(6, (N_ODD, DEC_BATCH, C_CONV - 1, C_WIDTH), 0.5),
        'meta': nrm(7, (N_META, D)),
        'e_norm_pre': 1.0 + nrm(8, (N_EVEN, D), 0.02),
        'e_w_in': nrm(9, (N_EVEN, D, E_IN_WIDTH), D ** -0.5),
        'e_kidx_ln_g': 1.0 + nrm(10, (N_EVEN, IDX_DIM), 0.02),
        'e_kidx_ln_b': nrm(11, (N_EVEN, IDX_DIM), 0.02),
        'e_bconv_w': nrm(12, (N_EVEN, B_CONV, B_WIDTH), B_CONV ** -0.5),
        'e_bconv_b': nrm(13, (N_EVEN, B_WIDTH), 0.02),
        'e_bln_g': 1.0 + nrm(14, (N_EVEN, B_WIDTH), 0.02),
        'e_bln_b': nrm(15, (N_EVEN, B_WIDTH), 0.02),
        'e_w_out': nrm(16, (N_EVEN, A_WIDTH + B_WIDTH, D), (A_WIDTH + B_WIDTH) ** -0.5),
        'e_norm_post': 1.0 + nrm(17, (N_EVEN, D), 0.02),
        'o_norm_pre': 1.0 + nrm(18, (N_ODD, D), 0.02),
        'o_w_in': nrm(19, (N_ODD, D, O_IN_WIDTH), D ** -0.5),
        'o_conv_w': nrm(20, (N_ODD, C_CONV, C_WIDTH), C_CONV ** -0.5),
        'o_conv_b': nrm(21, (N_ODD, C_WIDTH), 0.02),
        'o_w_out': nrm(22, (N_ODD, C_WIDTH, D), C_WIDTH ** -0.5),
        'o_norm_post': 1.0 + nrm(23, (N_ODD, D), 0.02),
    }


def reference(x_prompt, x_sample, cache_a_k, cache_a_v, cache_a_kidx, state_b_conv, state_c_conv,
              meta, e_norm_pre, e_w_in, e_kidx_ln_g, e_kidx_ln_b, e_bconv_w, e_bconv_b, e_bln_g, e_bln_b,
              e_w_out, e_norm_post, o_norm_pre, o_w_in, o_conv_w, o_conv_b, o_w_out, o_norm_post):
    nb, seq, d = x_prompt.shape
    ndb, nds, _ = x_sample.shape
    past = cache_a_k.shape[2]
    dt = x_prompt.dtype
    topk_p = min(TOPK_MAX, seq // 4)
    topk_s = min(TOPK_MAX, (past + nds) // 4)

    tp = N_META + seq
    xp = jnp.concatenate([jnp.broadcast_to(meta.astype(dt)[None], (nb, N_META, d)), x_prompt], axis=1)
    pos = jnp.arange(tp, dtype=jnp.int32)
    chunk_p = jnp.where(pos < N_META, 0, (pos - N_META) // CHUNK + 1).astype(jnp.int32)
    chunk_p = jnp.broadcast_to(chunk_p[None], (nb, tp))
    kc_s = jnp.concatenate([jnp.zeros((past,), jnp.int32), jnp.ones((nds,), jnp.int32)])
    kc_s = jnp.broadcast_to(kc_s[None], (ndb, past + nds))
    qc_s = jnp.ones((ndb, nds), jnp.int32)
    xs = x_sample

    pk, pv, pki, pbu, pcz = [], [], [], [], []
    sk, sv, ski, sbu, scz = [], [], [], [], []
    for layer in range(DEPTH):
        i = layer // 2
        if layer % 2 == 0:
            wts = (e_norm_pre[i], e_w_in[i], e_kidx_ln_g[i], e_kidx_ln_b[i], e_bconv_w[i], e_bconv_b[i],
                   e_bln_g[i], e_bln_b[i], e_w_out[i], e_norm_post[i])
            xp, k_, v_, ki_, u_ = even_layer(
                xp, jnp.zeros((nb, 0, A_KV_HEADS, A_HEAD_DIM), dt), jnp.zeros((nb, 0, A_KV_HEADS, A_HEAD_DIM), dt),
                jnp.zeros((nb, 0, IDX_DIM), dt), jnp.zeros((nb, B_CONV - 1, B_WIDTH), dt),
                chunk_p, chunk_p, topk_p, *wts)
            pk.append(k_); pv.append(v_); pki.append(ki_); pbu.append(u_)
            xs, k_, v_, ki_, u_ = even_layer(
                xs, cache_a_k[i], cache_a_v[i], cache_a_kidx[i], state_b_conv[i],
                kc_s, qc_s, topk_s, *wts)
            sk.append(k_); sv.append(v_); ski.append(ki_); sbu.append(u_)
        else:
            wts = (o_norm_pre[i], o_w_in[i], o_conv_w[i], o_conv_b[i], o_w_out[i], o_norm_post[i])
            xp, z_ = odd_layer(xp, jnp.zeros((nb, C_CONV - 1, C_WIDTH), dt), *wts)
            pcz.append(z_)
            xs, z_ = odd_layer(xs, state_c_conv[i], *wts)
            scz.append(z_)

    y_prompt = xp[:, N_META:]
    y_sample = xs
    prompt_a_k = jnp.stack(pk)
    prompt_a_v = jnp.stack(pv)
    prompt_a_kidx = jnp.stack(pki)
    prompt_b_conv = jnp.stack(pbu)
    prompt_c_conv = jnp.stack(pcz)
    sample_a_k = jnp.stack(sk)
    sample_a_v = jnp.stack(sv)
    sample_a_kidx = jnp.stack(ski)
    sample_b_conv = jnp.stack(sbu)
    sample_c_conv = jnp.stack(scz)
    return (y_prompt, y_sample, prompt_a_k, prompt_a_v, prompt_a_kidx, prompt_b_conv, prompt_c_conv,
            sample_a_k, sample_a_v, sample_a_kidx, sample_b_conv, sample_c_conv)
```

```python
import functools

import jax
import jax.numpy as jnp
from jax import lax
from jax.experimental import pallas as pl
from jax.experimental.pallas import tpu as pltpu

CHUNK = 64
N_META = 16
EPS = 1e-6
A_HEADS = 16
A_KV_HEADS = 4
A_HEAD_DIM = 128
A_WIDTH = A_HEADS * A_HEAD_DIM
A_KV_WIDTH = A_KV_HEADS * A_HEAD_DIM
A_GROUP = A_HEADS // A_KV_HEADS
IDX_HEADS = 16
IDX_DIM = 128
IDX_Q_WIDTH = IDX_HEADS * IDX_DIM
INDEX_SCALE = (IDX_HEADS * IDX_DIM) ** -0.5
TOPK_MAX = 256
B_CONV = 31
C_CONV = 3

V7X_LANES = 128
V7X_SUBLANES = 8
V7X_VMEM_LIMIT_BYTES = 56 * 1024 * 1024

Q_BLOCK = 128
KEY_BLOCK = 512
COUNT_ROWS = 256
SEQ_TILE = 64
B_HALO = 32
C_HALO = 8
INT_MIN = -2 ** 31
MASKED = -1e30

F32 = jnp.float32
BF16 = jnp.bfloat16
I32 = jnp.int32


def _round_up(x, m):
    return (x + m - 1) // m * m


def _row_tile(rows, cap):
    best = 16
    for t in range(16, cap + 1, 16):
        if rows % t == 0:
            best = t
    return best


def _params(sem):
    return pltpu.CompilerParams(dimension_semantics=sem, vmem_limit_bytes=V7X_VMEM_LIMIT_BYTES)


def _rms(x, g):
    return x * lax.rsqrt(jnp.mean(x * x, axis=-1, keepdims=True) + EPS) * g


def _prenorm_kernel(x_ref, g_ref, h_ref):
    h_ref[...] = _rms(x_ref[...], g_ref[...]).astype(h_ref.dtype)


def _postnorm_prenorm_kernel(x_ref, y_ref, gpost_ref, gpre_ref, xo_ref, h_ref):
    x1 = x_ref[...] + _rms(y_ref[...], gpost_ref[...])
    xo_ref[...] = x1
    h_ref[...] = _rms(x1, gpre_ref[...]).astype(h_ref.dtype)


def _postnorm_kernel(x_ref, y_ref, gpost_ref, xo_ref):
    xo_ref[...] = x_ref[...] + _rms(y_ref[...], gpost_ref[...])


def _rowwise_call(body, row_inputs, vec_inputs, out_dtypes, name):
    rows, d = row_inputs[0].shape
    bm = _row_tile(rows, 256)
    row_spec = pl.BlockSpec((bm, d), lambda i: (i, 0))
    vec_spec = pl.BlockSpec((1, d), lambda i: (0, 0))
    return pl.pallas_call(
        body,
        grid=(rows // bm,),
        in_specs=[row_spec] * len(row_inputs) + [vec_spec] * len(vec_inputs),
        out_specs=[row_spec] * len(out_dtypes),
        out_shape=[jax.ShapeDtypeStruct((rows, d), dt) for dt in out_dtypes],
        compiler_params=_params(("parallel",)),
        name=name,
    )(*row_inputs, *[v.reshape(1, d) for v in vec_inputs])


def _matmul_kernel(*refs, n_x, n_w, n_aux, terms, epilogue):
    xs = refs[:n_x]
    ws = refs[n_x:n_x + n_w]
    auxs = refs[n_x + n_w:n_x + n_w + n_aux]
    outs = refs[n_x + n_w + n_aux:]
    accs = [jnp.dot(xs[a][...], ws[b][...], preferred_element_type=F32) for a, b in terms]
    results = epilogue(*accs, *[r[...] for r in auxs])
    for o_ref, r in zip(outs, results):
        o_ref[...] = r.astype(o_ref.dtype)


def _matmul_call(xs, ws, terms, epilogue, out_dtypes, *, bn, auxs=(), name):
    rows = xs[0].shape[0]
    n = ws[0].shape[1]
    bm = _row_tile(rows, 1536)
    bn = max(t for t in range(V7X_LANES, min(bn, n) + 1, V7X_LANES) if n % t == 0)
    assert all(w.shape[1] == n for w in ws)
    in_specs = [pl.BlockSpec((bm, x.shape[1]), lambda i, j: (i, 0)) for x in xs]
    in_specs += [pl.BlockSpec((w.shape[0], bn), lambda i, j: (0, j)) for w in ws]
    in_specs += [pl.BlockSpec((1, bn), lambda i, j: (0, j)) for _ in auxs]
    out_spec = pl.BlockSpec((bm, bn), lambda i, j: (i, j))
    body = functools.partial(_matmul_kernel, n_x=len(xs), n_w=len(ws), n_aux=len(auxs),
                             terms=terms, epilogue=epilogue)
    return pl.pallas_call(
        body,
        grid=(rows // bm, n // bn),
        in_specs=in_specs,
        out_specs=[out_spec] * len(out_dtypes),
        out_shape=[jax.ShapeDtypeStruct((rows, n), dt) for dt in out_dtypes],
        compiler_params=_params(("parallel", "arbitrary")),
        name=name,
    )(*xs, *ws, *[a.reshape(1, n) for a in auxs])


def _silu(x):
    return x * jax.nn.sigmoid(x)


def _ep_q_qi(q, qi):
    return q * (A_HEAD_DIM ** -0.5), qi


def _ep_kv(k, v):
    return k, v, k, v


def _ep_kidx(ki, wi, g, b):
    kc = ki - jnp.mean(ki, axis=-1, keepdims=True)
    kn = kc * lax.rsqrt(jnp.mean(kc * kc, axis=-1, keepdims=True) + EPS) * g + b
    return kn, kn, wi * INDEX_SCALE


def _ep_glu(val, glu):
    return (val * jax.nn.sigmoid(glu),)


def _ep_silu(gate):
    return (_silu(gate),)


def _ep_sum(a, b):
    return (a + b,)


def _ep_identity(a):
    return (a,)


def _ep_odd(hx, cg, bg, gate):
    return cg * hx, bg * _silu(gate)


def _attn_kernel(qs_ref, qi_ref, wi_ref, sg_ref, ki_ref, k_ref, vt_ref, o_ref,
                 qrows, qirows, keys, bias, m_scr, l_scr, acc_scr, o_scr,
                 *, topk, n_lo, prompt, out_rows):
    step = pl.program_id(0)
    nq = Q_BLOCK
    lb = KEY_BLOCK

    for h in range(A_HEADS):
        qrows[h * nq:(h + 1) * nq, :] = qs_ref[:, h * A_HEAD_DIM:(h + 1) * A_HEAD_DIM]
    for h in range(IDX_HEADS):
        qirows[h * nq:(h + 1) * nq, :] = qi_ref[:, h * IDX_DIM:(h + 1) * IDX_DIM]
    w_t = wi_ref[...].T

    lane = lax.broadcasted_iota(I32, (1, nq), 1)
    if prompt:
        frame = (step - 1) * nq + lane
        limit = jnp.where(step == 0, 0, Q_BLOCK + CHUNK * (frame // CHUNK + 1))
        n_kt = jnp.where(step == 0, 1, pl.cdiv(nq * (step + 1), lb))
    else:
        limit = jnp.zeros((1, nq), I32)
        n_kt = ki_ref.shape[0] // lb

    def score_block(kt, carry):
        r0 = pl.multiple_of(kt * lb, lb)
        d = lax.dot_general(ki_ref[pl.ds(r0, lb), :], qirows[...], (((1,), (1,)), ((), ())),
                            preferred_element_type=F32)
        sc = jnp.zeros((lb, nq), F32)
        for h in range(IDX_HEADS):
            sc = sc + jnp.maximum(d[:, h * nq:(h + 1) * nq], 0.0) * w_t[h:h + 1, :]
        pos = r0 + lax.broadcasted_iota(I32, (lb, nq), 0)
        adm = (pos < n_lo) | ((pos >= Q_BLOCK) & (pos < limit))
        bits = lax.bitcast_convert_type(sc, I32)
        key = bits ^ ((bits >> 31) & 0x7FFFFFFF)
        keys[pl.ds(r0, lb), :] = jnp.where(adm, key, INT_MIN)
        return carry

    lax.fori_loop(0, n_kt, score_block, 0)

    n_ct = n_kt * (lb // COUNT_ROWS)

    def count_ge(cand):
        def body(c, acc):
            r0 = pl.multiple_of(c * COUNT_ROWS, COUNT_ROWS)
            hit = jnp.where(keys[pl.ds(r0, COUNT_ROWS), :] >= cand, 1.0, 0.0)
            for i in range(COUNT_ROWS // V7X_SUBLANES):
                acc = acc + hit[i * V7X_SUBLANES:(i + 1) * V7X_SUBLANES, :]
            return acc
        acc = lax.fori_loop(0, n_ct, body, jnp.zeros((V7X_SUBLANES, nq), F32))
        return jnp.sum(acc, axis=0, keepdims=True)

    def search_bit(i, prefix):
        trial = prefix | lax.shift_left(jnp.int32(1), 31 - i)
        return jnp.where(count_ge(trial ^ INT_MIN) >= float(topk), trial, prefix)

    prefix = lax.fori_loop(0, 32, search_bit, jnp.zeros((1, nq), I32))
    thr = jnp.maximum(prefix ^ INT_MIN, INT_MIN + 1)

    def bias_block(c, carry):
        r0 = pl.multiple_of(c * COUNT_ROWS, COUNT_ROWS)
        bias[pl.ds(r0, COUNT_ROWS), :] = jnp.where(keys[pl.ds(r0, COUNT_ROWS), :] >= thr, 0.0, MASKED)
        return carry

    lax.fori_loop(0, n_ct, bias_block, 0)

    gw = A_GROUP * nq
    for g in range(A_KV_HEADS):
        m_scr[...] = jnp.full((1, gw), MASKED, F32)
        l_scr[...] = jnp.zeros((1, gw), F32)
        acc_scr[...] = jnp.zeros((A_HEAD_DIM, gw), F32)

        def attend_block(kt, carry, g=g):
            r0 = pl.multiple_of(kt * lb, lb)
            s = lax.dot_general(k_ref[pl.ds(r0, lb), g * A_HEAD_DIM:(g + 1) * A_HEAD_DIM],
                                qrows[g * gw:(g + 1) * gw, :], (((1,), (1,)), ((), ())),
                                preferred_element_type=F32)
            bb = bias[pl.ds(r0, lb), :]
            s = s + jnp.concatenate([bb] * A_GROUP, axis=1)
            m_old = m_scr[...]
            m_new = jnp.maximum(m_old, jnp.max(s, axis=0, keepdims=True))
            alpha = jnp.exp(m_old - m_new)
            p = jnp.exp(s - m_new)
            l_scr[...] = l_scr[...] * alpha + jnp.sum(p, axis=0, keepdims=True)
            acc_scr[...] = acc_scr[...] * alpha + jnp.dot(vt_ref[g, kt], p.astype(BF16),
                                                          preferred_element_type=F32)
            m_scr[...] = m_new
            return carry

        lax.fori_loop(0, n_kt, attend_block, 0)
        o_t = acc_scr[...] / l_scr[...]
        for r in range(A_GROUP):
            h = g * A_GROUP + r
            cols = slice(h * A_HEAD_DIM, (h + 1) * A_HEAD_DIM)
            o_scr[:, cols] = o_t[:, r * nq:(r + 1) * nq].T * sg_ref[:, cols]

    if out_rows == nq:
        o_ref[...] = o_scr[...].astype(o_ref.dtype)
    else:
        half = pl.multiple_of((step % (nq // out_rows)) * out_rows, out_rows)
        o_ref[...] = o_scr[pl.ds(half, out_rows), :].astype(o_ref.dtype)


def _attn_call(qs, qi, wi, sg, ki_seq, k_seq, vt_seq, *, q_index, key_index, grid, out_rows_total,
               out_rows, out_index, topk, n_lo, prompt, name):
    lpad = ki_seq.shape[1]
    gw = A_GROUP * Q_BLOCK
    body = functools.partial(_attn_kernel, topk=topk, n_lo=n_lo, prompt=prompt, out_rows=out_rows)
    qspec = pl.BlockSpec((Q_BLOCK, A_WIDTH), lambda s: (q_index(s), 0))
    return pl.pallas_call(
        body,
        grid=(grid,),
        in_specs=[
            qspec, qspec,
            pl.BlockSpec((Q_BLOCK, V7X_LANES), lambda s: (q_index(s), 0)),
            pl.BlockSpec((Q_BLOCK, A_WIDTH), lambda s: (q_index(s), 0)),
            pl.BlockSpec((None, lpad, IDX_DIM), lambda s: (key_index(s), 0, 0)),
            pl.BlockSpec((None, lpad, A_KV_WIDTH), lambda s: (key_index(s), 0, 0)),
            pl.BlockSpec((None, A_KV_HEADS, lpad // KEY_BLOCK, A_HEAD_DIM, KEY_BLOCK),
                         lambda s: (key_index(s), 0, 0, 0, 0)),
        ],
        out_specs=pl.BlockSpec((out_rows, A_WIDTH), lambda s: (out_index(s), 0)),
        out_shape=jax.ShapeDtypeStruct((out_rows_total, A_WIDTH), BF16),
        scratch_shapes=[
            pltpu.VMEM((A_HEADS * Q_BLOCK, A_HEAD_DIM), BF16),
            pltpu.VMEM((IDX_HEADS * Q_BLOCK, IDX_DIM), BF16),
            pltpu.VMEM((lpad, Q_BLOCK), I32),
            pltpu.VMEM((lpad, Q_BLOCK), F32),
            pltpu.VMEM((1, gw), F32),
            pltpu.VMEM((1, gw), F32),
            pltpu.VMEM((A_HEAD_DIM, gw), F32),
            pltpu.VMEM((Q_BLOCK, A_WIDTH), F32),
        ],
        compiler_params=_params(("arbitrary",)),
        name=name,
    )(qs, qi, wi, sg, ki_seq, k_seq, vt_seq)


def _key_layouts(ki, k, v, lpad):
    nb, length = k.shape[:2]
    pad = ((0, 0), (0, lpad - length), (0, 0))
    ki, k, v = jnp.pad(ki, pad), jnp.pad(k, pad), jnp.pad(v, pad)
    vt = v.reshape(nb, lpad // KEY_BLOCK, KEY_BLOCK, A_KV_HEADS, A_HEAD_DIM).transpose(0, 3, 1, 4, 2)
    return ki, k, vt


def _load_history(step, n_prompt_tiles, halo, ext, state_ref, cur_ref):
    @pl.when(step == 0)
    def _():
        ext[0:halo, :] = jnp.zeros((halo, ext.shape[1]), F32)

    @pl.when(step > n_prompt_tiles)
    def _():
        ext[0:halo, :] = state_ref[...]

    ext[halo:halo + SEQ_TILE, :] = cur_ref[...]


def _save_history(step, halo, ext):
    @pl.when(step == 0)
    def _():
        if halo > N_META:
            ext[0:halo - N_META, :] = jnp.zeros((halo - N_META, ext.shape[1]), F32)
        keep = min(halo, N_META)
        ext[halo - keep:halo, :] = ext[halo + N_META - keep:halo + N_META, :]

    @pl.when(step > 0)
    def _():
        ext[0:halo, :] = ext[SEQ_TILE:SEQ_TILE + halo, :]


def _causal_taps(ext, w_ref, halo, width, cols):
    first = halo - (width - 1)
    acc = None
    for j in range(width):
        term = ext[first + j:first + j + SEQ_TILE, cols] * w_ref[j:j + 1, cols]
        acc = term if acc is None else acc + term
    return acc


def _bconv_kernel(u_ref, st_ref, sg_ref, w_ref, cb_ref, g_ref, b_ref, o_ref, ext, y_scr,
                  *, n_prompt_tiles):
    step = pl.program_id(0)
    width = u_ref.shape[1]
    _load_history(step, n_prompt_tiles, B_HALO, ext, st_ref, u_ref)
    total = jnp.zeros((SEQ_TILE, V7X_LANES), F32)
    for c in range(width // V7X_LANES):
        cols = slice(c * V7X_LANES, (c + 1) * V7X_LANES)
        y = _causal_taps(ext, w_ref, B_HALO, B_CONV, cols) + cb_ref[:, cols]
        y_scr[:, cols] = y
        total = total + y
    mean = jnp.sum(total, axis=-1, keepdims=True) / width
    sq = jnp.zeros((SEQ_TILE, V7X_LANES), F32)
    for c in range(width // V7X_LANES):
        cols = slice(c * V7X_LANES, (c + 1) * V7X_LANES)
        yc = y_scr[:, cols] - mean
        sq = sq + yc * yc
    inv = lax.rsqrt(jnp.sum(sq, axis=-1, keepdims=True) / width + EPS)
    for c in range(width // V7X_LANES):
        cols = slice(c * V7X_LANES, (c + 1) * V7X_LANES)
        yn = (y_scr[:, cols] - mean) * inv * g_ref[:, cols] + b_ref[:, cols]
        o_ref[:, cols] = (_silu(yn) * sg_ref[:, cols]).astype(o_ref.dtype)
    _save_history(step, B_HALO, ext)


def _cconv_kernel(z_ref, st_ref, bgs_ref, w_ref, cb_ref, o_ref, ext, *, n_prompt_tiles):
    step = pl.program_id(0)
    width = z_ref.shape[1]
    _load_history(step, n_prompt_tiles, C_HALO, ext, st_ref, z_ref)
    for c in range(width // V7X_LANES):
        cols = slice(c * V7X_LANES, (c + 1) * V7X_LANES)
        y = _causal_taps(ext, w_ref, C_HALO, C_CONV, cols) + cb_ref[:, cols]
        o_ref[:, cols] = (bgs_ref[:, cols] * y).astype(o_ref.dtype)
    _save_history(step, C_HALO, ext)


def _stream_conv_call(body, x, state, gate, gate_col_block, vecs, *, halo, n_prompt, n_sample,
                      scratch, name):
    rows, width = x.shape
    n_tiles = n_prompt + n_sample
    assert rows == (n_tiles + 2) * SEQ_TILE
    tile = lambda s: jnp.where(s == 0, n_tiles, jnp.where(s > n_tiles, s, s - 1))
    row_spec = pl.BlockSpec((SEQ_TILE, width), lambda s: (tile(s), 0))
    vec_specs = [pl.BlockSpec(v.shape, lambda s: (0, 0)) for v in vecs]
    return pl.pallas_call(
        functools.partial(body, n_prompt_tiles=n_prompt),
        grid=(2 + n_tiles,),
        in_specs=[
            row_spec,
            pl.BlockSpec((None, halo, width), lambda s: (jnp.clip(s - 1 - n_prompt, 0, n_sample - 1), 0, 0)),
            pl.BlockSpec((SEQ_TILE, width), lambda s: (tile(s), gate_col_block)),
        ] + vec_specs,
        out_specs=row_spec,
        out_shape=jax.ShapeDtypeStruct((rows, width), BF16),
        scratch_shapes=[pltpu.VMEM((halo + SEQ_TILE, width), F32)] + scratch,
        compiler_params=_params(("arbitrary",)),
        name=name,
    )(x, state, gate, *vecs)


def _pad_rows_front(a, rows):
    return jnp.pad(a, ((0, 0), (rows - a.shape[1], 0), (0, 0)))


def kernel(x_prompt, x_sample, cache_a_k, cache_a_v, cache_a_kidx, state_b_conv, state_c_conv, meta,
           e_norm_pre, e_w_in, e_kidx_ln_g, e_kidx_ln_b, e_bconv_w, e_bconv_b, e_bln_g, e_bln_b,
           e_w_out, e_norm_post, o_norm_pre, o_w_in, o_conv_w, o_conv_b, o_w_out, o_norm_post):
    nb, s_len, d = x_prompt.shape
    ndb, nds, _ = x_sample.shape
    past = cache_a_k.shape[2]
    bw = d // 2
    assert nb == 1 and e_w_in.shape[0] == 1 and o_w_in.shape[0] == 1
    assert nds == SEQ_TILE and s_len % Q_BLOCK == 0 and (ndb * nds) % Q_BLOCK == 0
    assert s_len >= B_CONV - 1 and A_WIDTH % bw == 0 and bw % V7X_LANES == 0
    ns = ndb * nds
    meta_row = s_len + ns
    rows = meta_row + Q_BLOCK
    topk_p = min(TOPK_MAX, s_len // 4)
    topk_s = min(TOPK_MAX, (past + nds) // 4)

    x_all = jnp.concatenate([x_prompt[0], x_sample.reshape(ns, d), meta,
                             jnp.zeros((Q_BLOCK - N_META, d), F32)], axis=0)

    w_in = e_w_in[0]
    col = [0]
    for width in (A_WIDTH, A_KV_WIDTH, A_KV_WIDTH, IDX_Q_WIDTH, IDX_DIM, IDX_HEADS, bw, bw, A_WIDTH + bw):
        col.append(col[-1] + width)
    w_q, w_k, w_v, w_qi, w_ki, w_wi, w_val, w_glu, w_gate = [
        w_in[:, col[i]:col[i + 1]].astype(BF16) for i in range(9)]
    w_wi = jnp.pad(w_wi, ((0, 0), (0, V7X_LANES - IDX_HEADS)))
    w_out = e_w_out[0].astype(BF16)

    h0, = _rowwise_call(_prenorm_kernel, [x_all], [e_norm_pre[0]], [BF16], "even_prenorm")
    q_s, q_i = _matmul_call([h0], [w_q, w_qi], [(0, 0), (0, 1)], _ep_q_qi, [BF16, BF16], bn=512,
                            name="even_q_qidx")
    k32, v32, k16, v16 = _matmul_call([h0], [w_k, w_v], [(0, 0), (0, 1)], _ep_kv,
                                      [F32, F32, BF16, BF16], bn=512, name="even_kv")
    ki32, ki16, wi = _matmul_call([h0], [w_ki, w_wi], [(0, 0), (0, 1)], _ep_kidx, [F32, BF16, F32],
                                  bn=V7X_LANES, auxs=[e_kidx_ln_g[0], e_kidx_ln_b[0]], name="even_kidx")
    u, = _matmul_call([h0], [w_val, w_glu], [(0, 0), (0, 1)], _ep_glu, [F32], bn=512, name="even_glu")
    sg, = _matmul_call([h0], [w_gate], [(0, 0)], _ep_silu, [F32], bn=512, name="even_gate")

    lpad_p = _round_up(Q_BLOCK + s_len, KEY_BLOCK)
    seq = lambda a: jnp.concatenate([a[meta_row:meta_row + Q_BLOCK], a[:s_len]], axis=0)[None]
    ki_p, k_p, vt_p = _key_layouts(seq(ki16), seq(k16), seq(v16), lpad_p)
    n_qb = s_len // Q_BLOCK
    oa_p = _attn_call(
        q_s, q_i, wi, sg, ki_p, k_p, vt_p,
        q_index=lambda s: jnp.where(s == 0, meta_row // Q_BLOCK, s - 1), key_index=lambda s: 0,
        grid=1 + n_qb, out_rows_total=s_len + Q_BLOCK, out_rows=Q_BLOCK,
        out_index=lambda s: jnp.where(s == 0, n_qb, s - 1),
        topk=topk_p, n_lo=N_META, prompt=True, name="prompt_attention")

    ls = past + nds
    lpad_s = _round_up(ls, KEY_BLOCK)
    new = lambda a: a[s_len:meta_row].reshape(ndb, nds, a.shape[1])
    ki_s, k_s, vt_s = _key_layouts(
        jnp.concatenate([cache_a_kidx[0].astype(BF16), new(ki16)], axis=1),
        jnp.concatenate([cache_a_k[0].reshape(ndb, past, A_KV_WIDTH).astype(BF16), new(k16)], axis=1),
        jnp.concatenate([cache_a_v[0].reshape(ndb, past, A_KV_WIDTH).astype(BF16), new(v16)], axis=1),
        lpad_s)
    per_block = Q_BLOCK // nds
    oa_s = _attn_call(
        q_s, q_i, wi, sg, ki_s, k_s, vt_s,
        q_index=lambda s: s_len // Q_BLOCK + s // per_block, key_index=lambda s: s,
        grid=ndb, out_rows_total=ns, out_rows=nds, out_index=lambda s: s,
        topk=topk_s, n_lo=ls, prompt=False, name="sample_attention")
    oa = jnp.concatenate([oa_p[:s_len], oa_s, oa_p[s_len:]], axis=0)

    n_prompt_tiles = s_len // SEQ_TILE
    bconv_w = jnp.pad(e_bconv_w[0], ((0, B_HALO - B_CONV), (0, 0)))
    ob = _stream_conv_call(
        _bconv_kernel, u, _pad_rows_front(state_b_conv[0], B_HALO), sg, A_WIDTH // bw,
        [bconv_w, e_bconv_b[0].reshape(1, bw), e_bln_g[0].reshape(1, bw), e_bln_b[0].reshape(1, bw)],
        halo=B_HALO, n_prompt=n_prompt_tiles, n_sample=ndb,
        scratch=[pltpu.VMEM((SEQ_TILE, bw), F32)], name="conformer_conv")

    y0, = _matmul_call([oa, ob], [w_out[:A_WIDTH], w_out[A_WIDTH:]], [(0, 0), (1, 1)], _ep_sum, [F32],
                       bn=512, name="even_out")
    x1, h1 = _rowwise_call(_postnorm_prenorm_kernel, [x_all, y0], [e_norm_post[0], o_norm_pre[0]],
                           [F32, BF16], "even_postnorm_odd_prenorm")

    w_odd = o_w_in[0]
    w_hx, w_cg, w_bg, w_og = [w_odd[:, i * d:(i + 1) * d].astype(BF16) for i in range(4)]
    z, bgs = _matmul_call([h1], [w_hx, w_cg, w_bg, w_og], [(0, 0), (0, 1), (0, 2), (0, 3)], _ep_odd,
                          [F32, F32], bn=256, name="odd_in")
    cconv_w = jnp.pad(o_conv_w[0], ((0, C_HALO - C_CONV), (0, 0)))
    o1 = _stream_conv_call(
        _cconv_kernel, z, _pad_rows_front(state_c_conv[0], C_HALO), bgs, 0,
        [cconv_w, o_conv_b[0].reshape(1, d)],
        halo=C_HALO, n_prompt=n_prompt_tiles, n_sample=ndb, scratch=[], name="short_conv")
    y1, = _matmul_call([o1], [o_w_out[0].astype(BF16)], [(0, 0)], _ep_identity, [F32], bn=512,
                       name="odd_out")
    x2, = _rowwise_call(_postnorm_kernel, [x1, y1], [o_norm_post[0]], [F32], "odd_postnorm")

    def prompt_rows(a):
        return jnp.concatenate([a[meta_row:meta_row + N_META], a[:s_len]], axis=0)

    def sample_rows(a):
        return a[s_len:meta_row].reshape(ndb, nds, a.shape[1])

    kv_shape = (A_KV_HEADS, A_HEAD_DIM)
    y_prompt = x2[:s_len][None]
    y_sample = x2[s_len:meta_row].reshape(ndb, nds, d)
    prompt_a_k = prompt_rows(k32).reshape(1, 1, N_META + s_len, *kv_shape)
    prompt_a_v = prompt_rows(v32).reshape(1, 1, N_META + s_len, *kv_shape)
    prompt_a_kidx = prompt_rows(ki32)[None, None]
    prompt_b_conv = u[s_len - (B_CONV - 1):s_len][None, None]
    prompt_c_conv = z[s_len - (C_CONV - 1):s_len][None, None]
    sample_a_k = sample_rows(k32).reshape(1, ndb, nds, *kv_shape)
    sample_a_v = sample_rows(v32).reshape(1, ndb, nds, *kv_shape)
    sample_a_kidx = sample_rows(ki32)[None]
    sample_b_conv = sample_rows(u)[None, :, nds - (B_CONV - 1):]
    sample_c_conv = sample_rows(z)[None, :, nds - (C_CONV - 1):]
    return (y_prompt, y_sample, prompt_a_k, prompt_a_v, prompt_a_kidx, prompt_b_conv, prompt_c_conv,
            sample_a_k, sample_a_v, sample_a_kidx, sample_b_conv, sample_c_conv)
```

```python
import functools

import jax
import jax.numpy as jnp
from jax import lax
from jax.experimental import pallas as pl
from jax.experimental.pallas import tpu as pltpu

CHUNK = 64
N_META = 16
EPS = 1e-6
A_HEADS = 16
A_KV_HEADS = 4
A_HEAD_DIM = 128
A_WIDTH = A_HEADS * A_HEAD_DIM
A_KV_WIDTH = A_KV_HEADS * A_HEAD_DIM
A_GROUP = A_HEADS // A_KV_HEADS
IDX_HEADS = 16
IDX_DIM = 128
IDX_Q_WIDTH = IDX_HEADS * IDX_DIM
INDEX_SCALE = (IDX_HEADS * IDX_DIM) ** -0.5
TOPK_MAX = 256
B_CONV = 31
C_CONV = 3

V7X_LANES = 128
V7X_SUBLANES = 8
V7X_VMEM_LIMIT_BYTES = 56 * 1024 * 1024

Q_BLOCK = 128
KEY_BLOCK = 512
COUNT_ROWS = 256
SEQ_TILE = 64
B_HALO = 32
C_HALO = 8
INT_MIN = -2 ** 31
MASKED = -1e30
LOG2_E = 1.4426950408889634

F32 = jnp.float32
BF16 = jnp.bfloat16
I32 = jnp.int32


def _round_up(x, m):
    return (x + m - 1) // m * m


def _row_tile(rows, cap):
    best = 16
    for t in range(16, cap + 1, 16):
        if rows % t == 0:
            best = t
    return best


def _params(sem):
    return pltpu.CompilerParams(dimension_semantics=sem, vmem_limit_bytes=V7X_VMEM_LIMIT_BYTES)


def _rms(x, g):
    return x * lax.rsqrt(jnp.mean(x * x, axis=-1, keepdims=True) + EPS) * g


def _prenorm_kernel(x_ref, g_ref, h_ref):
    h_ref[...] = _rms(x_ref[...], g_ref[...]).astype(h_ref.dtype)


def _postnorm_prenorm_kernel(x_ref, y_ref, gpost_ref, gpre_ref, xo_ref, h_ref):
    x1 = x_ref[...] + _rms(y_ref[...], gpost_ref[...])
    xo_ref[...] = x1
    h_ref[...] = _rms(x1, gpre_ref[...]).astype(h_ref.dtype)


def _postnorm_kernel(x_ref, y_ref, gpost_ref, xo_ref):
    xo_ref[...] = x_ref[...] + _rms(y_ref[...], gpost_ref[...])


def _rowwise_call(body, row_inputs, vec_inputs, out_dtypes, name):
    rows, d = row_inputs[0].shape
    bm = _row_tile(rows, 256)
    row_spec = pl.BlockSpec((bm, d), lambda i: (i, 0))
    vec_spec = pl.BlockSpec((1, d), lambda i: (0, 0))
    return pl.pallas_call(
        body,
        grid=(rows // bm,),
        in_specs=[row_spec] * len(row_inputs) + [vec_spec] * len(vec_inputs),
        out_specs=[row_spec] * len(out_dtypes),
        out_shape=[jax.ShapeDtypeStruct((rows, d), dt) for dt in out_dtypes],
        compiler_params=_params(("parallel",)),
        name=name,
    )(*row_inputs, *[v.reshape(1, d) for v in vec_inputs])


def _matmul_kernel(*refs, n_x, n_w, n_aux, terms, epilogue):
    xs = refs[:n_x]
    ws = refs[n_x:n_x + n_w]
    auxs = refs[n_x + n_w:n_x + n_w + n_aux]
    outs = refs[n_x + n_w + n_aux:]
    accs = [jnp.dot(xs[a][...], ws[b][...], preferred_element_type=F32) for a, b in terms]
    results = epilogue(*accs, *[r[...] for r in auxs])
    for o_ref, r in zip(outs, results):
        o_ref[...] = r.astype(o_ref.dtype)


def _matmul_call(xs, ws, terms, epilogue, out_dtypes, *, bn, auxs=(), name):
    rows = xs[0].shape[0]
    n = ws[0].shape[1]
    bm = _row_tile(rows, 1536)
    bn = max(t for t in range(V7X_LANES, min(bn, n) + 1, V7X_LANES) if n % t == 0)
    assert all(w.shape[1] == n for w in ws)
    in_specs = [pl.BlockSpec((bm, x.shape[1]), lambda i, j: (i, 0)) for x in xs]
    in_specs += [pl.BlockSpec((w.shape[0], bn), lambda i, j: (0, j)) for w in ws]
    in_specs += [pl.BlockSpec((1, bn), lambda i, j: (0, j)) for _ in auxs]
    out_spec = pl.BlockSpec((bm, bn), lambda i, j: (i, j))
    body = functools.partial(_matmul_kernel, n_x=len(xs), n_w=len(ws), n_aux=len(auxs),
                             terms=terms, epilogue=epilogue)
    return pl.pallas_call(
        body,
        grid=(rows // bm, n // bn),
        in_specs=in_specs,
        out_specs=[out_spec] * len(out_dtypes),
        out_shape=[jax.ShapeDtypeStruct((rows, n), dt) for dt in out_dtypes],
        compiler_params=_params(("parallel", "arbitrary")),
        name=name,
    )(*xs, *ws, *[a.reshape(1, n) for a in auxs])


def _silu(x):
    return x * jax.nn.sigmoid(x)


def _ep_q_qi(q, qi):
    return q * (A_HEAD_DIM ** -0.5 * LOG2_E), qi


def _ep_kv(k, v):
    return k, v, k, v


def _ep_kidx(ki, wi, g, b):
    kc = ki - jnp.mean(ki, axis=-1, keepdims=True)
    kn = kc * lax.rsqrt(jnp.mean(kc * kc, axis=-1, keepdims=True) + EPS) * g + b
    return kn, kn, wi * INDEX_SCALE


def _ep_glu(val, glu):
    return (val * jax.nn.sigmoid(glu),)


def _ep_silu(gate):
    return (_silu(gate),)


def _ep_sum(a, b):
    return (a + b,)


def _ep_identity(a):
    return (a,)


def _ep_odd(hx, cg, bg, gate):
    return cg * hx, bg * _silu(gate)


def _attn_kernel(qs_ref, qi_ref, wi_ref, sg_ref, ki_ref, k_ref, vt_ref, o_ref,
                 qrows, qirows, keys, bias, s_a, s_b, m_scr, l_scr, acc_scr, o_scr,
                 *, topk, n_lo, prompt, out_rows):
    step = pl.program_id(0)
    nq = Q_BLOCK
    lb = KEY_BLOCK

    one_hot = (lax.broadcasted_iota(I32, (nq, nq), 0) == lax.broadcasted_iota(I32, (nq, nq), 1))
    for h in range(A_HEADS):
        qrows[h * nq:(h + 1) * nq, 0:A_HEAD_DIM] = qs_ref[:, h * A_HEAD_DIM:(h + 1) * A_HEAD_DIM]
        qrows[h * nq:(h + 1) * nq, A_HEAD_DIM:A_HEAD_DIM + nq] = one_hot.astype(BF16)
    for h in range(IDX_HEADS):
        qirows[h * nq:(h + 1) * nq, :] = qi_ref[:, h * IDX_DIM:(h + 1) * IDX_DIM]
    w_t = wi_ref[...].T

    lane = lax.broadcasted_iota(I32, (1, nq), 1)
    if prompt:
        frame = (step - 1) * nq + lane
        limit = jnp.where(step == 0, 0, Q_BLOCK + CHUNK * (frame // CHUNK + 1))
        n_kt = jnp.where(step == 0, 1, pl.cdiv(nq * (step + 1), lb))
    else:
        limit = jnp.zeros((1, nq), I32)
        n_kt = ki_ref.shape[0] // lb - 1

    def score_block(kt, carry):
        r0 = pl.multiple_of(kt * lb, lb)
        d = lax.dot_general(ki_ref[pl.ds(r0, lb), :], qirows[...], (((1,), (1,)), ((), ())),
                            preferred_element_type=F32)
        sc = jnp.zeros((lb, nq), F32)
        for h in range(IDX_HEADS):
            sc = sc + jnp.maximum(d[:, h * nq:(h + 1) * nq], 0.0) * w_t[h:h + 1, :]
        pos = r0 + lax.broadcasted_iota(I32, (lb, nq), 0)
        adm = (pos < n_lo) | ((pos >= Q_BLOCK) & (pos < limit))
        bits = lax.bitcast_convert_type(sc, I32)
        key = bits ^ ((bits >> 31) & 0x7FFFFFFF)
        keys[pl.ds(r0, lb), :] = jnp.where(adm, key, INT_MIN)
        return carry

    lax.fori_loop(0, n_kt, score_block, 0)

    n_ct = n_kt * (lb // COUNT_ROWS)

    def count_ge(cand):
        def body(c, acc):
            r0 = pl.multiple_of(c * COUNT_ROWS, COUNT_ROWS)
            hit = jnp.where(keys[pl.ds(r0, COUNT_ROWS), :] >= cand, 1.0, 0.0)
            parts = [hit[i * V7X_SUBLANES:(i + 1) * V7X_SUBLANES, :]
                     for i in range(COUNT_ROWS // V7X_SUBLANES)]
            while len(parts) > 1:
                parts = [a + b for a, b in zip(parts[::2], parts[1::2])]
            return acc + parts[0]
        acc = lax.fori_loop(0, n_ct, body, jnp.zeros((V7X_SUBLANES, nq), F32))
        return jnp.sum(acc, axis=0, keepdims=True)

    def search_bit(i, prefix):
        trial = prefix | lax.shift_left(jnp.int32(1), 31 - i)
        return jnp.where(count_ge(trial ^ INT_MIN) >= float(topk), trial, prefix)

    prefix = lax.fori_loop(0, 32, search_bit, jnp.zeros((1, nq), I32))
    thr = jnp.maximum(prefix ^ INT_MIN, INT_MIN + 1)

    def bias_block(c, carry):
        r0 = pl.multiple_of(c * COUNT_ROWS, COUNT_ROWS)
        bias[pl.ds(r0, COUNT_ROWS), :] = jnp.where(keys[pl.ds(r0, COUNT_ROWS), :] >= thr, 0.0,
                                                   MASKED).astype(BF16)
        return carry

    lax.fori_loop(0, n_ct, bias_block, 0)

    gw = A_GROUP * nq
    m_scr[...] = jnp.full(m_scr.shape, MASKED, F32)
    l_scr[...] = jnp.zeros(l_scr.shape, F32)
    acc_scr[...] = jnp.zeros(acc_scr.shape, F32)

    bias[pl.ds(pl.multiple_of(n_kt * lb, lb), lb), :] = jnp.full((lb, nq), MASKED, BF16)

    def score_keys(kt, s_ref):
        r0 = pl.multiple_of(kt * lb, lb)
        bb = bias[pl.ds(r0, lb), :]
        for g in range(A_KV_HEADS):
            k_aug = jnp.concatenate(
                [k_ref[pl.ds(r0, lb), g * A_HEAD_DIM:(g + 1) * A_HEAD_DIM], bb], axis=1)
            s_ref[g] = lax.dot_general(k_aug, qrows[g * gw:(g + 1) * gw, :],
                                       (((1,), (1,)), ((), ())),
                                       preferred_element_type=F32)

    def softmax_pv(kt, s_ref):
        for g in range(A_KV_HEADS):
            s = s_ref[g]
            m_old = m_scr[g]
            m_new = jnp.maximum(m_old, jnp.max(s, axis=0, keepdims=True))
            alpha = jnp.exp2(m_old - m_new)
            p = jnp.exp2(s - m_new)
            l_scr[g] = l_scr[g] * alpha + jnp.sum(p, axis=0, keepdims=True)
            acc_scr[g] = acc_scr[g] * alpha + jnp.dot(vt_ref[g, kt], p.astype(BF16),
                                                      preferred_element_type=F32)
            m_scr[g] = m_new

    score_keys(0, s_a)

    def attend_pair(j, carry):
        kt = 2 * j
        score_keys(kt + 1, s_b)
        softmax_pv(kt, s_a)
        score_keys(jnp.minimum(kt + 2, n_kt), s_a)
        softmax_pv(kt + 1, s_b)
        return carry

    lax.fori_loop(0, (n_kt + 1) // 2, attend_pair, 0)
    for g in range(A_KV_HEADS):
        o_t = acc_scr[g] / l_scr[g]
        for r in range(A_GROUP):
            h = g * A_GROUP + r
            cols = slice(h * A_HEAD_DIM, (h + 1) * A_HEAD_DIM)
            o_scr[:, cols] = o_t[:, r * nq:(r + 1) * nq].T * sg_ref[:, cols]

    if out_rows == nq:
        o_ref[...] = o_scr[...].astype(o_ref.dtype)
    else:
        half = pl.multiple_of((step % (nq // out_rows)) * out_rows, out_rows)
        o_ref[...] = o_scr[pl.ds(half, out_rows), :].astype(o_ref.dtype)


def _attn_call(qs, qi, wi, sg, ki_seq, k_seq, vt_seq, *, q_index, key_index, grid, out_rows_total,
               out_rows, out_index, topk, n_lo, prompt, name):
    lpad = ki_seq.shape[1]
    gw = A_GROUP * Q_BLOCK
    body = functools.partial(_attn_kernel, topk=topk, n_lo=n_lo, prompt=prompt, out_rows=out_rows)
    qspec = pl.BlockSpec((Q_BLOCK, A_WIDTH), lambda s: (q_index(s), 0))
    return pl.pallas_call(
        body,
        grid=(grid,),
        in_specs=[
            qspec, qspec,
            pl.BlockSpec((Q_BLOCK, V7X_LANES), lambda s: (q_index(s), 0)),
            pl.BlockSpec((Q_BLOCK, A_WIDTH), lambda s: (q_index(s), 0)),
            pl.BlockSpec((None, lpad, IDX_DIM), lambda s: (key_index(s), 0, 0)),
            pl.BlockSpec((None, lpad, A_KV_WIDTH), lambda s: (key_index(s), 0, 0)),
            pl.BlockSpec((None, A_KV_HEADS, lpad // KEY_BLOCK, A_HEAD_DIM, KEY_BLOCK),
                         lambda s: (key_index(s), 0, 0, 0, 0)),
        ],
        out_specs=pl.BlockSpec((out_rows, A_WIDTH), lambda s: (out_index(s), 0)),
        out_shape=jax.ShapeDtypeStruct((out_rows_total, A_WIDTH), BF16),
        scratch_shapes=[
            pltpu.VMEM((A_HEADS * Q_BLOCK, A_HEAD_DIM + Q_BLOCK), BF16),
            pltpu.VMEM((IDX_HEADS * Q_BLOCK, IDX_DIM), BF16),
            pltpu.VMEM((lpad, Q_BLOCK), I32),
            pltpu.VMEM((lpad, Q_BLOCK), BF16),
            pltpu.VMEM((A_KV_HEADS, KEY_BLOCK, gw), F32),
            pltpu.VMEM((A_KV_HEADS, KEY_BLOCK, gw), F32),
            pltpu.VMEM((A_KV_HEADS, 1, gw), F32),
            pltpu.VMEM((A_KV_HEADS, 1, gw), F32),
            pltpu.VMEM((A_KV_HEADS, A_HEAD_DIM, gw), F32),
            pltpu.VMEM((Q_BLOCK, A_WIDTH), F32),
        ],
        compiler_params=_params(("arbitrary",)),
        name=name,
    )(qs, qi, wi, sg, ki_seq, k_seq, vt_seq)


def _key_layouts(ki, k, v, lpad):
    nb, length = k.shape[:2]
    pad = ((0, 0), (0, lpad - length), (0, 0))
    ki, k, v = jnp.pad(ki, pad), jnp.pad(k, pad), jnp.pad(v, pad)
    vt = v.reshape(nb, lpad // KEY_BLOCK, KEY_BLOCK, A_KV_HEADS, A_HEAD_DIM).transpose(0, 3, 1, 4, 2)
    return ki, k, vt


def _load_history(step, n_prompt_tiles, halo, ext, state_ref, cur_ref):
    @pl.when(step == 0)
    def _():
        ext[0:halo, :] = jnp.zeros((halo, ext.shape[1]), F32)

    @pl.when(step > n_prompt_tiles)
    def _():
        ext[0:halo, :] = state_ref[...]

    ext[halo:halo + SEQ_TILE, :] = cur_ref[...]


def _save_history(step, halo, ext):
    @pl.when(step == 0)
    def _():
        if halo > N_META:
            ext[0:halo - N_META, :] = jnp.zeros((halo - N_META, ext.shape[1]), F32)
        keep = min(halo, N_META)
        ext[halo - keep:halo, :] = ext[halo + N_META - keep:halo + N_META, :]

    @pl.when(step > 0)
    def _():
        ext[0:halo, :] = ext[SEQ_TILE:SEQ_TILE + halo, :]


def _causal_taps(ext, w_ref, halo, width, cols):
    first = halo - (width - 1)
    acc = None
    for j in range(width):
        term = ext[first + j:first + j + SEQ_TILE, cols] * w_ref[j:j + 1, cols]
        acc = term if acc is None else acc + term
    return acc


def _bconv_kernel(u_ref, st_ref, sg_ref, w_ref, cb_ref, g_ref, b_ref, o_ref, ext, y_scr,
                  *, n_prompt_tiles):
    step = pl.program_id(0)
    width = u_ref.shape[1]
    _load_history(step, n_prompt_tiles, B_HALO, ext, st_ref, u_ref)
    total = jnp.zeros((SEQ_TILE, V7X_LANES), F32)
    for c in range(width // V7X_LANES):
        cols = slice(c * V7X_LANES, (c + 1) * V7X_LANES)
        y = _causal_taps(ext, w_ref, B_HALO, B_CONV, cols) + cb_ref[:, cols]
        y_scr[:, cols] = y
        total = total + y
    mean = jnp.sum(total, axis=-1, keepdims=True) / width
    sq = jnp.zeros((SEQ_TILE, V7X_LANES), F32)
    for c in range(width // V7X_LANES):
        cols = slice(c * V7X_LANES, (c + 1) * V7X_LANES)
        yc = y_scr[:, cols] - mean
        sq = sq + yc * yc
    inv = lax.rsqrt(jnp.sum(sq, axis=-1, keepdims=True) / width + EPS)
    for c in range(width // V7X_LANES):
        cols = slice(c * V7X_LANES, (c + 1) * V7X_LANES)
        yn = (y_scr[:, cols] - mean) * inv * g_ref[:, cols] + b_ref[:, cols]
        o_ref[:, cols] = (_silu(yn) * sg_ref[:, cols]).astype(o_ref.dtype)
    _save_history(step, B_HALO, ext)


def _cconv_kernel(z_ref, st_ref, bgs_ref, w_ref, cb_ref, o_ref, ext, *, n_prompt_tiles):
    step = pl.program_id(0)
    width = z_ref.shape[1]
    _load_history(step, n_prompt_tiles, C_HALO, ext, st_ref, z_ref)
    for c in range(width // V7X_LANES):
        cols = slice(c * V7X_LANES, (c + 1) * V7X_LANES)
        y = _causal_taps(ext, w_ref, C_HALO, C_CONV, cols) + cb_ref[:, cols]
        o_ref[:, cols] = (bgs_ref[:, cols] * y).astype(o_ref.dtype)
    _save_history(step, C_HALO, ext)


def _stream_conv_call(body, x, state, gate, gate_col_block, vecs, *, halo, n_prompt, n_sample,
                      scratch, name):
    rows, width = x.shape
    n_tiles = n_prompt + n_sample
    assert rows == (n_tiles + 2) * SEQ_TILE
    tile = lambda s: jnp.where(s == 0, n_tiles, jnp.where(s > n_tiles, s, s - 1))
    row_spec = pl.BlockSpec((SEQ_TILE, width), lambda s: (tile(s), 0))
    vec_specs = [pl.BlockSpec(v.shape, lambda s: (0, 0)) for v in vecs]
    return pl.pallas_call(
        functools.partial(body, n_prompt_tiles=n_prompt),
        grid=(2 + n_tiles,),
        in_specs=[
            row_spec,
            pl.BlockSpec((None, halo, width), lambda s: (jnp.clip(s - 1 - n_prompt, 0, n_sample - 1), 0, 0)),
            pl.BlockSpec((SEQ_TILE, width), lambda s: (tile(s), gate_col_block)),
        ] + vec_specs,
        out_specs=row_spec,
        out_shape=jax.ShapeDtypeStruct((rows, width), BF16),
        scratch_shapes=[pltpu.VMEM((halo + SEQ_TILE, width), F32)] + scratch,
        compiler_params=_params(("arbitrary",)),
        name=name,
    )(x, state, gate, *vecs)


def _pad_rows_front(a, rows):
    return jnp.pad(a, ((0, 0), (rows - a.shape[1], 0), (0, 0)))


def kernel(x_prompt, x_sample, cache_a_k, cache_a_v, cache_a_kidx, state_b_conv, state_c_conv, meta,
           e_norm_pre, e_w_in, e_kidx_ln_g, e_kidx_ln_b, e_bconv_w, e_bconv_b, e_bln_g, e_bln_b,
           e_w_out, e_norm_post, o_norm_pre, o_w_in, o_conv_w, o_conv_b, o_w_out, o_norm_post):
    nb, s_len, d = x_prompt.shape
    ndb, nds, _ = x_sample.shape
    past = cache_a_k.shape[2]
    bw = d // 2
    assert nb == 1 and e_w_in.shape[0] == 1 and o_w_in.shape[0] == 1
    assert nds == SEQ_TILE and s_len % Q_BLOCK == 0 and (ndb * nds) % Q_BLOCK == 0
    assert s_len >= B_CONV - 1 and A_WIDTH % bw == 0 and bw % V7X_LANES == 0
    ns = ndb * nds
    meta_row = s_len + ns
    rows = meta_row + Q_BLOCK
    topk_p = min(TOPK_MAX, s_len // 4)
    topk_s = min(TOPK_MAX, (past + nds) // 4)

    x_all = jnp.concatenate([x_prompt[0], x_sample.reshape(ns, d), meta,
                             jnp.zeros((Q_BLOCK - N_META, d), F32)], axis=0)

    w_in = e_w_in[0]
    col = [0]
    for width in (A_WIDTH, A_KV_WIDTH, A_KV_WIDTH, IDX_Q_WIDTH, IDX_DIM, IDX_HEADS, bw, bw, A_WIDTH + bw):
        col.append(col[-1] + width)
    w_q, w_k, w_v, w_qi, w_ki, w_wi, w_val, w_glu, w_gate = [
        w_in[:, col[i]:col[i + 1]].astype(BF16) for i in range(9)]
    w_wi = jnp.pad(w_wi, ((0, 0), (0, V7X_LANES - IDX_HEADS)))
    w_out = e_w_out[0].astype(BF16)

    h0, = _rowwise_call(_prenorm_kernel, [x_all], [e_norm_pre[0]], [BF16], "even_prenorm")
    q_s, q_i = _matmul_call([h0], [w_q, w_qi], [(0, 0), (0, 1)], _ep_q_qi, [BF16, BF16], bn=512,
                            name="even_q_qidx")
    k32, v32, k16, v16 = _matmul_call([h0], [w_k, w_v], [(0, 0), (0, 1)], _ep_kv,
                                      [F32, F32, BF16, BF16], bn=512, name="even_kv")
    ki32, ki16, wi = _matmul_call([h0], [w_ki, w_wi], [(0, 0), (0, 1)], _ep_kidx, [F32, BF16, F32],
                                  bn=V7X_LANES, auxs=[e_kidx_ln_g[0], e_kidx_ln_b[0]], name="even_kidx")
    u, = _matmul_call([h0], [w_val, w_glu], [(0, 0), (0, 1)], _ep_glu, [F32], bn=512, name="even_glu")
    sg, = _matmul_call([h0], [w_gate], [(0, 0)], _ep_silu, [F32], bn=512, name="even_gate")

    lpad_p = _round_up(Q_BLOCK + s_len, KEY_BLOCK) + KEY_BLOCK
    seq = lambda a: jnp.concatenate([a[meta_row:meta_row + Q_BLOCK], a[:s_len]], axis=0)[None]
    ki_p, k_p, vt_p = _key_layouts(seq(ki16), seq(k16), seq(v16), lpad_p)
    n_qb = s_len // Q_BLOCK
    oa_p = _attn_call(
        q_s, q_i, wi, sg, ki_p, k_p, vt_p,
        q_index=lambda s: jnp.where(s == 0, meta_row // Q_BLOCK, s - 1), key_index=lambda s: 0,
        grid=1 + n_qb, out_rows_total=s_len + Q_BLOCK, out_rows=Q_BLOCK,
        out_index=lambda s: jnp.where(s == 0, n_qb, s - 1),
        topk=topk_p, n_lo=N_META, prompt=True, name="prompt_attention")

    ls = past + nds
    lpad_s = _round_up(ls, KEY_BLOCK) + KEY_BLOCK
    new = lambda a: a[s_len:meta_row].reshape(ndb, nds, a.shape[1])
    ki_s, k_s, vt_s = _key_layouts(
        jnp.concatenate([cache_a_kidx[0].astype(BF16), new(ki16)], axis=1),
        jnp.concatenate([cache_a_k[0].reshape(ndb, past, A_KV_WIDTH).astype(BF16), new(k16)], axis=1),
        jnp.concatenate([cache_a_v[0].reshape(ndb, past, A_KV_WIDTH).astype(BF16), new(v16)], axis=1),
        lpad_s)
    per_block = Q_BLOCK // nds
    oa_s = _attn_call(
        q_s, q_i, wi, sg, ki_s, k_s, vt_s,
        q_index=lambda s: s_len // Q_BLOCK + s // per_block, key_index=lambda s: s,
        grid=ndb, out_rows_total=ns, out_rows=nds, out_index=lambda s: s,
        topk=topk_s, n_lo=ls, prompt=False, name="sample_attention")
    oa = jnp.concatenate([oa_p[:s_len], oa_s, oa_p[s_len:]], axis=0)

    n_prompt_tiles = s_len // SEQ_TILE
    bconv_w = jnp.pad(e_bconv_w[0], ((0, B_HALO - B_CONV), (0, 0)))
    ob = _stream_conv_call(
        _bconv_kernel, u, _pad_rows_front(state_b_conv[0], B_HALO), sg, A_WIDTH // bw,
        [bconv_w, e_bconv_b[0].reshape(1, bw), e_bln_g[0].reshape(1, bw), e_bln_b[0].reshape(1, bw)],
        halo=B_HALO, n_prompt=n_prompt_tiles, n_sample=ndb,
        scratch=[pltpu.VMEM((SEQ_TILE, bw), F32)], name="conformer_conv")

    y0, = _matmul_call([oa, ob], [w_out[:A_WIDTH], w_out[A_WIDTH:]], [(0, 0), (1, 1)], _ep_sum, [F32],
                       bn=512, name="even_out")
    x1, h1 = _rowwise_call(_postnorm_prenorm_kernel, [x_all, y0], [e_norm_post[0], o_norm_pre[0]],
                           [F32, BF16], "even_postnorm_odd_prenorm")

    w_odd = o_w_in[0]
    w_hx, w_cg, w_bg, w_og = [w_odd[:, i * d:(i + 1) * d].astype(BF16) for i in range(4)]
    z, bgs = _matmul_call([h1], [w_hx, w_cg, w_bg, w_og], [(0, 0), (0, 1), (0, 2), (0, 3)], _ep_odd,
                          [F32, F32], bn=256, name="odd_in")
    cconv_w = jnp.pad(o_conv_w[0], ((0, C_HALO - C_CONV), (0, 0)))
    o1 = _stream_conv_call(
        _cconv_kernel, z, _pad_rows_front(state_c_conv[0], C_HALO), bgs, 0,
        [cconv_w, o_conv_b[0].reshape(1, d)],
        halo=C_HALO, n_prompt=n_prompt_tiles, n_sample=ndb, scratch=[], name="short_conv")
    y1, = _matmul_call([o1], [o_w_out[0].astype(BF16)], [(0, 0)], _ep_identity, [F32], bn=512,
                       name="odd_out")
    x2, = _rowwise_call(_postnorm_kernel, [x1, y1], [o_norm_post[0]], [F32], "odd_postnorm")

    def prompt_rows(a):
        return jnp.concatenate([a[meta_row:meta_row + N_META], a[:s_len]], axis=0)

    def sample_rows(a):
        return a[s_len:meta_row].reshape(ndb, nds, a.shape[1])

    kv_shape = (A_KV_HEADS, A_HEAD_DIM)
    y_prompt = x2[:s_len][None]
    y_sample = x2[s_len:meta_row].reshape(ndb, nds, d)
    prompt_a_k = prompt_rows(k32).reshape(1, 1, N_META + s_len, *kv_shape)
    prompt_a_v = prompt_rows(v32).reshape(1, 1, N_META + s_len, *kv_shape)
    prompt_a_kidx = prompt_rows(ki32)[None, None]
    prompt_b_conv = u[s_len - (B_CONV - 1):s_len][None, None]
    prompt_c_conv = z[s_len - (C_CONV - 1):s_len][None, None]
    sample_a_k = sample_rows(k32).reshape(1, ndb, nds, *kv_shape)
    sample_a_v = sample_rows(v32).reshape(1, ndb, nds, *kv_shape)
    sample_a_kidx = sample_rows(ki32)[None]
    sample_b_conv = sample_rows(u)[None, :, nds - (B_CONV - 1):]
    sample_c_conv = sample_rows(z)[None, :, nds - (C_CONV - 1):]
    return (y_prompt, y_sample, prompt_a_k, prompt_a_v, prompt_a_kidx, prompt_b_conv, prompt_c_conv,
            sample_a_k, sample_a_v, sample_a_kidx, sample_b_conv, sample_c_conv)
```

```python
import functools

import jax
import jax.numpy as jnp
from jax import lax
from jax.experimental import pallas as pl
from jax.experimental.pallas import tpu as pltpu

CHUNK = 64
N_META = 16
EPS = 1e-6
A_HEADS = 16
A_KV_HEADS = 4
A_HEAD_DIM = 128
A_WIDTH = A_HEADS * A_HEAD_DIM
A_KV_WIDTH = A_KV_HEADS * A_HEAD_DIM
A_GROUP = A_HEADS // A_KV_HEADS
IDX_HEADS = 16
IDX_DIM = 128
IDX_Q_WIDTH = IDX_HEADS * IDX_DIM
INDEX_SCALE = (IDX_HEADS * IDX_DIM) ** -0.5
TOPK_MAX = 256
B_CONV = 31
C_CONV = 3

V7X_LANES = 128
V7X_SUBLANES = 8
V7X_VMEM_LIMIT_BYTES = 56 * 1024 * 1024

Q_BLOCK = 128
KEY_BLOCK = 512
COUNT_ROWS = 512
SEQ_TILE = 64
B_HALO = 32
C_HALO = 8
INT_MIN = -2 ** 31
MASKED = -1e30
LOG2_E = 1.4426950408889634

F32 = jnp.float32
BF16 = jnp.bfloat16
I32 = jnp.int32
I16 = jnp.int16
HALF_BIAS = 2 ** 15


def _round_up(x, m):
    return (x + m - 1) // m * m


def _row_tile(rows, cap):
    best = 16
    for t in range(16, cap + 1, 16):
        if rows % t == 0:
            best = t
    return best


def _params(sem):
    return pltpu.CompilerParams(dimension_semantics=sem, vmem_limit_bytes=V7X_VMEM_LIMIT_BYTES)


def _rms(x, g):
    return x * lax.rsqrt(jnp.mean(x * x, axis=-1, keepdims=True) + EPS) * g


def _gather_pool_block(xp_ref, xs_ref, meta_ref, x_scr, n_prompt, n_sample):
    i = pl.program_id(0)

    @pl.when(i < n_prompt)
    def _():
        x_scr[...] = xp_ref[...]

    @pl.when((i >= n_prompt) & (i < n_prompt + n_sample))
    def _():
        x_scr[...] = xs_ref[...]

    @pl.when(i == n_prompt + n_sample)
    def _():
        x_scr[0:N_META, :] = meta_ref[...]
        x_scr[N_META:, :] = jnp.zeros((Q_BLOCK - N_META, x_scr.shape[1]), F32)


def _prenorm_kernel(xp_ref, xs_ref, meta_ref, g_ref, h_ref, x_scr, *, n_prompt, n_sample):
    _gather_pool_block(xp_ref, xs_ref, meta_ref, x_scr, n_prompt, n_sample)
    h_ref[...] = _rms(x_scr[...], g_ref[...]).astype(h_ref.dtype)


def _postnorm_prenorm_kernel(xp_ref, xs_ref, meta_ref, y_ref, gpost_ref, gpre_ref, xo_ref, h_ref, x_scr,
                             *, n_prompt, n_sample):
    _gather_pool_block(xp_ref, xs_ref, meta_ref, x_scr, n_prompt, n_sample)
    x1 = x_scr[...] + _rms(y_ref[...], gpost_ref[...])
    xo_ref[...] = x1
    h_ref[...] = _rms(x1, gpre_ref[...]).astype(h_ref.dtype)


def _postnorm_kernel(x_ref, y_ref, gpost_ref, xo_ref):
    xo_ref[...] = x_ref[...] + _rms(y_ref[...], gpost_ref[...])


def _pool_rowwise_call(body, x_prompt, x_sample, meta, pool_inputs, vec_inputs, out_dtypes, name):
    s_len, d = x_prompt.shape
    n_prompt, n_sample = s_len // Q_BLOCK, x_sample.shape[0] // Q_BLOCK
    n_blocks = n_prompt + n_sample + 1
    pool_spec = pl.BlockSpec((Q_BLOCK, d), lambda i: (i, 0))
    vec_spec = pl.BlockSpec((1, d), lambda i: (0, 0))
    return pl.pallas_call(
        functools.partial(body, n_prompt=n_prompt, n_sample=n_sample),
        grid=(n_blocks,),
        in_specs=[
            pl.BlockSpec((Q_BLOCK, d), lambda i: (jnp.minimum(i, n_prompt - 1), 0)),
            pl.BlockSpec((Q_BLOCK, d), lambda i: (jnp.clip(i - n_prompt, 0, n_sample - 1), 0)),
            pl.BlockSpec((N_META, d), lambda i: (0, 0)),
        ] + [pool_spec] * len(pool_inputs) + [vec_spec] * len(vec_inputs),
        out_specs=[pool_spec] * len(out_dtypes),
        out_shape=[jax.ShapeDtypeStruct((n_blocks * Q_BLOCK, d), dt) for dt in out_dtypes],
        scratch_shapes=[pltpu.VMEM((Q_BLOCK, d), F32)],
        compiler_params=_params(("arbitrary",)),
        name=name,
    )(x_prompt, x_sample, meta, *pool_inputs, *[v.reshape(1, d) for v in vec_inputs])


def _final_postnorm_call(x1, y1, g, first_block, n_blocks, name):
    d = x1.shape[1]
    in_spec = pl.BlockSpec((Q_BLOCK, d), lambda i: (i + first_block, 0))
    return pl.pallas_call(
        _postnorm_kernel,
        grid=(n_blocks,),
        in_specs=[in_spec, in_spec, pl.BlockSpec((1, d), lambda i: (0, 0))],
        out_specs=pl.BlockSpec((Q_BLOCK, d), lambda i: (i, 0)),
        out_shape=jax.ShapeDtypeStruct((n_blocks * Q_BLOCK, d), F32),
        compiler_params=_params(("parallel",)),
        name=name,
    )(x1, y1, g.reshape(1, d))


def _matmul_kernel(*refs, n_x, n_w, n_aux, terms, epilogue):
    xs = refs[:n_x]
    ws = refs[n_x:n_x + n_w]
    auxs = refs[n_x + n_w:n_x + n_w + n_aux]
    outs = refs[n_x + n_w + n_aux:]
    accs = [jnp.dot(xs[a][...], ws[b][...], preferred_element_type=F32) for a, b in terms]
    results = epilogue(*accs, *[r[...] for r in auxs])
    for o_ref, r in zip(outs, results):
        o_ref[...] = r.astype(o_ref.dtype)


def _matmul_call(xs, ws, terms, epilogue, out_dtypes, *, bn, auxs=(), name):
    rows = xs[0].shape[0]
    n = ws[0].shape[1]
    bm = _row_tile(rows, 1536)
    bn = max(t for t in range(V7X_LANES, min(bn, n) + 1, V7X_LANES) if n % t == 0)
    assert all(w.shape[1] == n for w in ws)
    in_specs = [pl.BlockSpec((bm, x.shape[1]), lambda i, j: (i, 0)) for x in xs]
    in_specs += [pl.BlockSpec((w.shape[0], bn), lambda i, j: (0, j)) for w in ws]
    in_specs += [pl.BlockSpec((1, bn), lambda i, j: (0, j)) for _ in auxs]
    out_spec = pl.BlockSpec((bm, bn), lambda i, j: (i, j))
    body = functools.partial(_matmul_kernel, n_x=len(xs), n_w=len(ws), n_aux=len(auxs),
                             terms=terms, epilogue=epilogue)
    return pl.pallas_call(
        body,
        grid=(rows // bm, n // bn),
        in_specs=in_specs,
        out_specs=[out_spec] * len(out_dtypes),
        out_shape=[jax.ShapeDtypeStruct((rows, n), dt) for dt in out_dtypes],
        compiler_params=_params(("parallel", "arbitrary")),
        name=name,
    )(*xs, *ws, *[a.reshape(1, n) for a in auxs])


def _silu(x):
    return x * jax.nn.sigmoid(x)


def _ep_q_qi(q, qi):
    return q * (A_HEAD_DIM ** -0.5 * LOG2_E), qi


def _ep_kv(k, v):
    return k, v, k, v


def _ep_kidx(ki, wi, g, b):
    kc = ki - jnp.mean(ki, axis=-1, keepdims=True)
    kn = kc * lax.rsqrt(jnp.mean(kc * kc, axis=-1, keepdims=True) + EPS) * g + b
    return kn, kn, wi * INDEX_SCALE


def _ep_glu(val, glu):
    return (val * jax.nn.sigmoid(glu),)


def _ep_silu(gate):
    return (_silu(gate),)


def _ep_sum(a, b):
    return (a + b,)


def _ep_identity(a):
    return (a,)


def _ep_odd(hx, cg, bg, gate):
    return cg * hx, bg * _silu(gate)


def _prompt_attn_kernel(qs_ref, qi_ref, wi_ref, sg_ref, ki_ref, k_ref, v_ref, o_ref, *scratch,
                        topk):
    step = pl.program_id(0)
    frame = (step - 1) * Q_BLOCK + lax.broadcasted_iota(I32, (1, Q_BLOCK), 1)
    limit = jnp.where(step == 0, 0, Q_BLOCK + CHUNK * (frame // CHUNK + 1))
    n_kt = jnp.where(step == 0, 1, pl.cdiv(Q_BLOCK * (step + 1), KEY_BLOCK))
    _attend(qs_ref, qi_ref, wi_ref, sg_ref, ki_ref, k_ref, v_ref, *scratch,
            topk=topk, n_lo=N_META, limit=limit, n_kt=n_kt)
    o_ref[...] = scratch[-1][...].astype(o_ref.dtype)


def _sample_attn_kernel(qs_ref, qi_ref, wi_ref, sg_ref, cki_ref, ck_ref, cv_ref, nki_ref, nk_ref, nv_ref,
                        o_ref, ki_buf, k_buf, v_buf, *scratch, topk):
    step = pl.program_id(0)
    past, new = ck_ref.shape[0], nk_ref.shape[0]
    for buf, cache_ref, new_ref in ((ki_buf, cki_ref, nki_ref), (k_buf, ck_ref, nk_ref), (v_buf, cv_ref, nv_ref)):
        for r in range(0, past, KEY_BLOCK):
            n = min(KEY_BLOCK, past - r)
            buf[r:r + n, :] = cache_ref[r:r + n, :].astype(BF16)
        buf[past:past + new, :] = new_ref[...]
        buf[past + new:, :] = jnp.zeros((buf.shape[0] - past - new, buf.shape[1]), BF16)
    _attend(qs_ref, qi_ref, wi_ref, sg_ref, ki_buf, k_buf, v_buf, *scratch,
            topk=topk, n_lo=past + new, limit=jnp.zeros((1, Q_BLOCK), I32),
            n_kt=ki_buf.shape[0] // KEY_BLOCK - 1)
    half = pl.multiple_of((step % (Q_BLOCK // new)) * new, new)
    o_ref[...] = scratch[-1][pl.ds(half, new), :].astype(o_ref.dtype)


def _attend(qs_ref, qi_ref, wi_ref, sg_ref, ki_ref, k_ref, v_ref,
            qrows, qirows, keys, half, bias, s_a, s_b, m_scr, l_scr, acc_scr, o_scr,
            *, topk, n_lo, limit, n_kt):
    nq = Q_BLOCK
    lb = KEY_BLOCK

    one_hot = (lax.broadcasted_iota(I32, (nq, nq), 0) == lax.broadcasted_iota(I32, (nq, nq), 1))
    for h in range(A_HEADS):
        qrows[h * nq:(h + 1) * nq, 0:A_HEAD_DIM] = qs_ref[:, h * A_HEAD_DIM:(h + 1) * A_HEAD_DIM]
        qrows[h * nq:(h + 1) * nq, A_HEAD_DIM:A_HEAD_DIM + nq] = one_hot.astype(BF16)
    for h in range(IDX_HEADS):
        qirows[h * nq:(h + 1) * nq, :] = qi_ref[:, h * IDX_DIM:(h + 1) * IDX_DIM]
    w_t = wi_ref[...].T

    def score_block(kt, carry):
        r0 = pl.multiple_of(kt * lb, lb)
        d = lax.dot_general(ki_ref[pl.ds(r0, lb), :], qirows[...], (((1,), (1,)), ((), ())),
                            preferred_element_type=F32)
        sc = jnp.zeros((lb, nq), F32)
        for h in range(IDX_HEADS):
            sc = sc + jnp.maximum(d[:, h * nq:(h + 1) * nq], 0.0) * w_t[h:h + 1, :]
        pos = r0 + lax.broadcasted_iota(I32, (lb, nq), 0)
        adm = (pos < n_lo) | ((pos >= Q_BLOCK) & (pos < limit))
        bits = lax.bitcast_convert_type(sc, I32)
        key = bits ^ ((bits >> 31) & 0x7FFFFFFF)
        key = jnp.where(adm, key, INT_MIN)
        keys[pl.ds(r0, lb), :] = key
        half[pl.ds(r0, lb), :] = (key >> 16).astype(I16)
        return carry

    lax.fori_loop(0, n_kt, score_block, 0)

    n_ct = n_kt * (lb // COUNT_ROWS)

    rows16 = 2 * V7X_SUBLANES

    def count_ge(cand):
        cand = cand.astype(I16)

        def body(c, acc):
            r0 = pl.multiple_of(c * COUNT_ROWS, COUNT_ROWS)
            hit = jnp.where(half[pl.ds(r0, COUNT_ROWS), :] >= cand, jnp.int16(1), jnp.int16(0))
            parts = [hit[i * rows16:(i + 1) * rows16, :] for i in range(COUNT_ROWS // rows16)]
            while len(parts) > 1:
                parts = [a + b for a, b in zip(parts[::2], parts[1::2])]
            return acc + parts[0]
        acc = lax.fori_loop(0, n_ct, body, jnp.zeros((rows16, nq), I16))
        return jnp.sum(acc.astype(F32), axis=0, keepdims=True)

    def kth_largest_half():
        def search_bit(i, prefix):
            trial = prefix | lax.shift_left(jnp.int32(1), 15 - i)
            return jnp.where(count_ge(trial - HALF_BIAS) >= float(topk), trial, prefix)
        return lax.fori_loop(0, 16, search_bit, jnp.zeros((1, nq), I32)) - HALF_BIAS

    hi_thr = kth_largest_half()

    def low_half_block(c, carry):
        r0 = pl.multiple_of(c * COUNT_ROWS, COUNT_ROWS)
        k = keys[pl.ds(r0, COUNT_ROWS), :]
        hi = k >> 16
        lo = (k & 0xFFFF) - HALF_BIAS
        half[pl.ds(r0, COUNT_ROWS), :] = jnp.where(
            hi == hi_thr, lo, jnp.where(hi > hi_thr, HALF_BIAS - 1, -HALF_BIAS)).astype(I16)
        return carry

    lax.fori_loop(0, n_ct, low_half_block, 0)
    lo_thr = kth_largest_half()
    thr = lax.shift_left(hi_thr, 16) | (lo_thr + HALF_BIAS)
    thr = jnp.maximum(thr, INT_MIN + 1)

    def count_where(pred):
        def body(c, acc):
            r0 = pl.multiple_of(c * COUNT_ROWS, COUNT_ROWS)
            pos = r0 + lax.broadcasted_iota(I32, (COUNT_ROWS, nq), 0)
            hit = jnp.where(pred(keys[pl.ds(r0, COUNT_ROWS), :], pos), 1.0, 0.0)
            parts = [hit[i * V7X_SUBLANES:(i + 1) * V7X_SUBLANES, :]
                     for i in range(COUNT_ROWS // V7X_SUBLANES)]
            while len(parts) > 1:
                parts = [a + b for a, b in zip(parts[::2], parts[1::2])]
            return acc + parts[0]
        acc = lax.fori_loop(0, n_ct, body, jnp.zeros((V7X_SUBLANES, nq), F32))
        return jnp.sum(acc, axis=0, keepdims=True)

    def write_bias(selected):
        def body(c, carry):
            r0 = pl.multiple_of(c * COUNT_ROWS, COUNT_ROWS)
            pos = r0 + lax.broadcasted_iota(I32, (COUNT_ROWS, nq), 0)
            bias[pl.ds(r0, COUNT_ROWS), :] = jnp.where(
                selected(keys[pl.ds(r0, COUNT_ROWS), :], pos), 0.0, MASKED).astype(BF16)
            return carry
        lax.fori_loop(0, n_ct, body, 0)

    n_above = count_where(lambda k, pos: k > thr)
    n_tied = count_where(lambda k, pos: k == thr)
    keep = float(topk) - n_above
    surplus = jnp.max(jnp.where(n_tied > keep, 1.0, 0.0))

    @pl.when(surplus == 0.0)
    def _():
        write_bias(lambda k, pos: k >= thr)

    @pl.when(surplus > 0.0)
    def _():
        position_bits = keys.shape[0].bit_length()

        def cut_bit(i, prefix):
            trial = prefix | lax.shift_left(jnp.int32(1), position_bits - 1 - i)
            admitted = count_where(lambda k, pos: (k == thr) & (pos < trial))
            return jnp.where(admitted < keep, trial, prefix)
        cut = lax.fori_loop(0, position_bits, cut_bit, jnp.zeros((1, nq), I32)) + 1
        write_bias(lambda k, pos: (k > thr) | ((k == thr) & (pos < cut)))

    gw = A_GROUP * nq
    m_scr[...] = jnp.full(m_scr.shape, MASKED, F32)
    l_scr[...] = jnp.zeros(l_scr.shape, F32)
    acc_scr[...] = jnp.zeros(acc_scr.shape, F32)

    bias[pl.ds(pl.multiple_of(n_kt * lb, lb), lb), :] = jnp.full((lb, nq), MASKED, BF16)

    def score_keys(kt, s_ref):
        r0 = pl.multiple_of(kt * lb, lb)
        bb = bias[pl.ds(r0, lb), :]
        for g in range(A_KV_HEADS):
            k_aug = jnp.concatenate(
                [k_ref[pl.ds(r0, lb), g * A_HEAD_DIM:(g + 1) * A_HEAD_DIM], bb], axis=1)
            s_ref[g] = lax.dot_general(k_aug, qrows[g * gw:(g + 1) * gw, :],
                                       (((1,), (1,)), ((), ())),
                                       preferred_element_type=F32)

    def softmax_pv(kt, s_ref):
        for g in range(A_KV_HEADS):
            s = s_ref[g]
            m_old = m_scr[g]
            m_new = jnp.maximum(m_old, jnp.max(s, axis=0, keepdims=True))
            alpha = jnp.exp2(m_old - m_new)
            p = jnp.exp2(s - m_new)
            l_scr[g] = l_scr[g] * alpha + jnp.sum(p, axis=0, keepdims=True)
            v_blk = v_ref[pl.ds(pl.multiple_of(kt * lb, lb), lb), g * A_HEAD_DIM:(g + 1) * A_HEAD_DIM]
            acc_scr[g] = acc_scr[g] * alpha + lax.dot_general(
                v_blk, p.astype(BF16), (((0,), (0,)), ((), ())), preferred_element_type=F32)
            m_scr[g] = m_new

    score_keys(0, s_a)

    def attend_pair(j, carry):
        kt = 2 * j
        score_keys(kt + 1, s_b)
        softmax_pv(kt, s_a)
        score_keys(jnp.minimum(kt + 2, n_kt), s_a)
        softmax_pv(kt + 1, s_b)
        return carry

    lax.fori_loop(0, (n_kt + 1) // 2, attend_pair, 0)
    for g in range(A_KV_HEADS):
        o_t = acc_scr[g] / l_scr[g]
        for r in range(A_GROUP):
            h = g * A_GROUP + r
            cols = slice(h * A_HEAD_DIM, (h + 1) * A_HEAD_DIM)
            o_scr[:, cols] = o_t[:, r * nq:(r + 1) * nq].T * sg_ref[:, cols]


def _attn_scratch(lpad):
    gw = A_GROUP * Q_BLOCK
    return [
        pltpu.VMEM((A_HEADS * Q_BLOCK, A_HEAD_DIM + Q_BLOCK), BF16),
        pltpu.VMEM((IDX_HEADS * Q_BLOCK, IDX_DIM), BF16),
        pltpu.VMEM((lpad, Q_BLOCK), I32),
        pltpu.VMEM((lpad, Q_BLOCK), I16),
        pltpu.VMEM((lpad, Q_BLOCK), BF16),
        pltpu.VMEM((A_KV_HEADS, KEY_BLOCK, gw), F32),
        pltpu.VMEM((A_KV_HEADS, KEY_BLOCK, gw), F32),
        pltpu.VMEM((A_KV_HEADS, 1, gw), F32),
        pltpu.VMEM((A_KV_HEADS, 1, gw), F32),
        pltpu.VMEM((A_KV_HEADS, A_HEAD_DIM, gw), F32),
        pltpu.VMEM((Q_BLOCK, A_WIDTH), F32),
    ]


def _query_specs(q_index):
    qspec = pl.BlockSpec((Q_BLOCK, A_WIDTH), lambda s: (q_index(s), 0))
    return [qspec, qspec, pl.BlockSpec((Q_BLOCK, V7X_LANES), lambda s: (q_index(s), 0)), qspec]


def _prompt_attn_call(qs, qi, wi, sg, ki_seq, k_seq, v_seq, *, s_len, meta_block, topk):
    lpad = ki_seq.shape[0]
    n_qb = s_len // Q_BLOCK
    full = lambda a: pl.BlockSpec(a.shape, lambda s: (0, 0))
    return pl.pallas_call(
        functools.partial(_prompt_attn_kernel, topk=topk),
        grid=(1 + n_qb,),
        in_specs=_query_specs(lambda s: jnp.where(s == 0, meta_block, s - 1))
        + [full(ki_seq), full(k_seq), full(v_seq)],
        out_specs=pl.BlockSpec((Q_BLOCK, A_WIDTH), lambda s: (jnp.where(s == 0, n_qb, s - 1), 0)),
        out_shape=jax.ShapeDtypeStruct((s_len + Q_BLOCK, A_WIDTH), BF16),
        scratch_shapes=_attn_scratch(lpad),
        compiler_params=_params(("arbitrary",)),
        name="prompt_attention",
    )(qs, qi, wi, sg, ki_seq, k_seq, v_seq)


def _sample_attn_call(qs, qi, wi, sg, cache_ki, cache_k, cache_v, ki16, k16, v16, *, s_len, topk):
    ndb, past, _ = cache_k.shape
    new = SEQ_TILE
    lpad = _round_up(past + new, KEY_BLOCK) + KEY_BLOCK
    cache_spec = lambda a: pl.BlockSpec((None, past, a.shape[2]), lambda s: (s, 0, 0))
    new_spec = lambda a: pl.BlockSpec((new, a.shape[1]), lambda s: (s_len // new + s, 0))
    return pl.pallas_call(
        functools.partial(_sample_attn_kernel, topk=topk),
        grid=(ndb,),
        in_specs=_query_specs(lambda s: s_len // Q_BLOCK + s // (Q_BLOCK // new))
        + [cache_spec(cache_ki), cache_spec(cache_k), cache_spec(cache_v),
           new_spec(ki16), new_spec(k16), new_spec(v16)],
        out_specs=pl.BlockSpec((new, A_WIDTH), lambda s: (s, 0)),
        out_shape=jax.ShapeDtypeStruct((ndb * new, A_WIDTH), BF16),
        scratch_shapes=[pltpu.VMEM((lpad, IDX_DIM), BF16), pltpu.VMEM((lpad, A_KV_WIDTH), BF16),
                        pltpu.VMEM((lpad, A_KV_WIDTH), BF16)] + _attn_scratch(lpad),
        compiler_params=_params(("arbitrary",)),
        name="sample_attention",
    )(qs, qi, wi, sg, cache_ki, cache_k, cache_v, ki16, k16, v16)


def _load_history(step, n_prompt_tiles, halo, ext, state_ref, cur_ref):
    @pl.when(step == 0)
    def _():
        ext[0:halo, :] = jnp.zeros((halo, ext.shape[1]), F32)

    @pl.when(step > n_prompt_tiles)
    def _():
        ext[0:halo, :] = state_ref[...]

    ext[halo:halo + SEQ_TILE, :] = cur_ref[...]


def _save_history(step, halo, ext):
    @pl.when(step == 0)
    def _():
        if halo > N_META:
            ext[0:halo - N_META, :] = jnp.zeros((halo - N_META, ext.shape[1]), F32)
        keep = min(halo, N_META)
        ext[halo - keep:halo, :] = ext[halo + N_META - keep:halo + N_META, :]

    @pl.when(step > 0)
    def _():
        ext[0:halo, :] = ext[SEQ_TILE:SEQ_TILE + halo, :]


def _causal_taps(ext, w_ref, halo, width, cols):
    first = halo - (width - 1)
    x = ext[:, cols]
    rows = x.shape[0]
    rotated = {0: x}
    for j in range(width):
        r = (first + j) % V7X_SUBLANES
        if r not in rotated:
            rotated[r] = pltpu.roll(x, rows - r, axis=0)
    acc = None
    for j in range(width):
        r = (first + j) % V7X_SUBLANES
        base = first + j - r
        term = rotated[r][base:base + SEQ_TILE, :] * w_ref[j:j + 1, cols]
        acc = term if acc is None else acc + term
    return acc


def _bconv_kernel(u_ref, st_ref, sg_ref, w_ref, cb_ref, g_ref, b_ref, o_ref, ext, y_scr,
                  *, n_prompt_tiles):
    step = pl.program_id(0)
    width = u_ref.shape[1]
    _load_history(step, n_prompt_tiles, B_HALO, ext, st_ref, u_ref)
    total = jnp.zeros((SEQ_TILE, V7X_LANES), F32)
    for c in range(width // V7X_LANES):
        cols = slice(c * V7X_LANES, (c + 1) * V7X_LANES)
        y = _causal_taps(ext, w_ref, B_HALO, B_CONV, cols) + cb_ref[:, cols]
        y_scr[:, cols] = y
        total = total + y
    mean = jnp.sum(total, axis=-1, keepdims=True) / width
    sq = jnp.zeros((SEQ_TILE, V7X_LANES), F32)
    for c in range(width // V7X_LANES):
        cols = slice(c * V7X_LANES, (c + 1) * V7X_LANES)
        yc = y_scr[:, cols] - mean
        sq = sq + yc * yc
    inv = lax.rsqrt(jnp.sum(sq, axis=-1, keepdims=True) / width + EPS)
    for c in range(width // V7X_LANES):
        cols = slice(c * V7X_LANES, (c + 1) * V7X_LANES)
        yn = (y_scr[:, cols] - mean) * inv * g_ref[:, cols] + b_ref[:, cols]
        o_ref[:, cols] = (_silu(yn) * sg_ref[:, cols]).astype(o_ref.dtype)
    _save_history(step, B_HALO, ext)


def _cconv_kernel(z_ref, st_ref, bgs_ref, w_ref, cb_ref, o_ref, ext, *, n_prompt_tiles):
    step = pl.program_id(0)
    width = z_ref.shape[1]
    _load_history(step, n_prompt_tiles, C_HALO, ext, st_ref, z_ref)
    for c in range(width // V7X_LANES):
        cols = slice(c * V7X_LANES, (c + 1) * V7X_LANES)
        y = _causal_taps(ext, w_ref, C_HALO, C_CONV, cols) + cb_ref[:, cols]
        o_ref[:, cols] = (bgs_ref[:, cols] * y).astype(o_ref.dtype)
    _save_history(step, C_HALO, ext)


def _stream_conv_call(body, x, state, gate, gate_col_block, vecs, *, halo, n_prompt, n_sample,
                      scratch, name):
    rows, width = x.shape
    n_tiles = n_prompt + n_sample
    assert rows == (n_tiles + 2) * SEQ_TILE
    tile = lambda s: jnp.where(s == 0, n_tiles, jnp.where(s > n_tiles, s, s - 1))
    row_spec = pl.BlockSpec((SEQ_TILE, width), lambda s: (tile(s), 0))
    vec_specs = [pl.BlockSpec(v.shape, lambda s: (0, 0)) for v in vecs]
    return pl.pallas_call(
        functools.partial(body, n_prompt_tiles=n_prompt),
        grid=(2 + n_tiles,),
        in_specs=[
            row_spec,
            pl.BlockSpec((None, halo, width), lambda s: (jnp.clip(s - 1 - n_prompt, 0, n_sample - 1), 0, 0)),
            pl.BlockSpec((SEQ_TILE, width), lambda s: (tile(s), gate_col_block)),
        ] + vec_specs,
        out_specs=row_spec,
        out_shape=jax.ShapeDtypeStruct((rows, width), BF16),
        scratch_shapes=[pltpu.VMEM((halo + SEQ_TILE, width), F32)] + scratch,
        compiler_params=_params(("arbitrary",)),
        name=name,
    )(x, state, gate, *vecs)


def _pad_rows_front(a, rows):
    return jnp.pad(a, ((0, 0), (rows - a.shape[1], 0), (0, 0)))


def kernel(x_prompt, x_sample, cache_a_k, cache_a_v, cache_a_kidx, state_b_conv, state_c_conv, meta,
           e_norm_pre, e_w_in, e_kidx_ln_g, e_kidx_ln_b, e_bconv_w, e_bconv_b, e_bln_g, e_bln_b,
           e_w_out, e_norm_post, o_norm_pre, o_w_in, o_conv_w, o_conv_b, o_w_out, o_norm_post):
    nb, s_len, d = x_prompt.shape
    ndb, nds, _ = x_sample.shape
    past = cache_a_k.shape[2]
    bw = d // 2
    assert nb == 1 and e_w_in.shape[0] == 1 and o_w_in.shape[0] == 1
    assert nds == SEQ_TILE and s_len % Q_BLOCK == 0 and (ndb * nds) % Q_BLOCK == 0
    assert s_len >= B_CONV - 1 and A_WIDTH % bw == 0 and bw % V7X_LANES == 0
    ns = ndb * nds
    meta_row = s_len + ns
    rows = meta_row + Q_BLOCK
    topk_p = min(TOPK_MAX, s_len // 4)
    topk_s = min(TOPK_MAX, (past + nds) // 4)

    sources = (x_prompt[0], x_sample.reshape(ns, d), meta)

    w_in = e_w_in[0]
    col = [0]
    for width in (A_WIDTH, A_KV_WIDTH, A_KV_WIDTH, IDX_Q_WIDTH, IDX_DIM, IDX_HEADS, bw, bw, A_WIDTH + bw):
        col.append(col[-1] + width)
    w_q, w_k, w_v, w_qi, w_ki, w_wi, w_val, w_glu, w_gate = [
        w_in[:, col[i]:col[i + 1]].astype(BF16) for i in range(9)]
    w_wi = jnp.pad(w_wi, ((0, 0), (0, V7X_LANES - IDX_HEADS)))
    w_out = e_w_out[0].astype(BF16)

    h0, = _pool_rowwise_call(_prenorm_kernel, *sources, [], [e_norm_pre[0]], [BF16], "even_prenorm")
    q_s, q_i = _matmul_call([h0], [w_q, w_qi], [(0, 0), (0, 1)], _ep_q_qi, [BF16, BF16], bn=512,
                            name="even_q_qidx")
    k32, v32, k16, v16 = _matmul_call([h0], [w_k, w_v], [(0, 0), (0, 1)], _ep_kv,
                                      [F32, F32, BF16, BF16], bn=512, name="even_kv")
    ki32, ki16, wi = _matmul_call([h0], [w_ki, w_wi], [(0, 0), (0, 1)], _ep_kidx, [F32, BF16, F32],
                                  bn=V7X_LANES, auxs=[e_kidx_ln_g[0], e_kidx_ln_b[0]], name="even_kidx")
    u, = _matmul_call([h0], [w_val, w_glu], [(0, 0), (0, 1)], _ep_glu, [F32], bn=512, name="even_glu")
    sg, = _matmul_call([h0], [w_gate], [(0, 0)], _ep_silu, [F32], bn=512, name="even_gate")

    lpad_p = _round_up(Q_BLOCK + s_len, KEY_BLOCK) + KEY_BLOCK
    seq = lambda a: jnp.concatenate(
        [a[meta_row:meta_row + Q_BLOCK], a[:s_len],
         jnp.zeros((lpad_p - Q_BLOCK - s_len, a.shape[1]), a.dtype)], axis=0)
    oa_p = _prompt_attn_call(q_s, q_i, wi, sg, seq(ki16), seq(k16), seq(v16),
                             s_len=s_len, meta_block=meta_row // Q_BLOCK, topk=topk_p)
    oa_s = _sample_attn_call(q_s, q_i, wi, sg, cache_a_kidx[0],
                             cache_a_k[0].reshape(ndb, past, A_KV_WIDTH),
                             cache_a_v[0].reshape(ndb, past, A_KV_WIDTH), ki16, k16, v16,
                             s_len=s_len, topk=topk_s)
    oa = jnp.concatenate([oa_p[:s_len], oa_s, oa_p[s_len:]], axis=0)

    n_prompt_tiles = s_len // SEQ_TILE
    bconv_w = jnp.pad(e_bconv_w[0], ((0, B_HALO - B_CONV), (0, 0)))
    ob = _stream_conv_call(
        _bconv_kernel, u, _pad_rows_front(state_b_conv[0], B_HALO), sg, A_WIDTH // bw,
        [bconv_w, e_bconv_b[0].reshape(1, bw), e_bln_g[0].reshape(1, bw), e_bln_b[0].reshape(1, bw)],
        halo=B_HALO, n_prompt=n_prompt_tiles, n_sample=ndb,
        scratch=[pltpu.VMEM((SEQ_TILE, bw), F32)], name="conformer_conv")

    y0, = _matmul_call([oa, ob], [w_out[:A_WIDTH], w_out[A_WIDTH:]], [(0, 0), (1, 1)], _ep_sum, [F32],
                       bn=512, name="even_out")
    x1, h1 = _pool_rowwise_call(_postnorm_prenorm_kernel, *sources, [y0],
                                [e_norm_post[0], o_norm_pre[0]], [F32, BF16], "even_postnorm_odd_prenorm")

    w_odd = o_w_in[0]
    w_hx, w_cg, w_bg, w_og = [w_odd[:, i * d:(i + 1) * d].astype(BF16) for i in range(4)]
    z, bgs = _matmul_call([h1], [w_hx, w_cg, w_bg, w_og], [(0, 0), (0, 1), (0, 2), (0, 3)], _ep_odd,
                          [F32, F32], bn=256, name="odd_in")
    cconv_w = jnp.pad(o_conv_w[0], ((0, C_HALO - C_CONV), (0, 0)))
    o1 = _stream_conv_call(
        _cconv_kernel, z, _pad_rows_front(state_c_conv[0], C_HALO), bgs, 0,
        [cconv_w, o_conv_b[0].reshape(1, d)],
        halo=C_HALO, n_prompt=n_prompt_tiles, n_sample=ndb, scratch=[], name="short_conv")
    y1, = _matmul_call([o1], [o_w_out[0].astype(BF16)], [(0, 0)], _ep_identity, [F32], bn=512,
                       name="odd_out")
    y_prompt = _final_postnorm_call(x1, y1, o_norm_post[0], 0, s_len // Q_BLOCK, "odd_postnorm_prompt")
    y_sample = _final_postnorm_call(x1, y1, o_norm_post[0], s_len // Q_BLOCK, ns // Q_BLOCK,
                                    "odd_postnorm_sample")

    def prompt_rows(a):
        return jnp.concatenate([a[meta_row:meta_row + N_META], a[:s_len]], axis=0)

    def sample_rows(a):
        return a[s_len:meta_row].reshape(ndb, nds, a.shape[1])

    kv_shape = (A_KV_HEADS, A_HEAD_DIM)
    y_prompt = y_prompt[None]
    y_sample = y_sample.reshape(ndb, nds, d)
    prompt_a_k = prompt_rows(k32).reshape(1, 1, N_META + s_len, *kv_shape)
    prompt_a_v = prompt_rows(v32).reshape(1, 1, N_META + s_len, *kv_shape)
    prompt_a_kidx = prompt_rows(ki32)[None, None]
    prompt_b_conv = u[s_len - (B_CONV - 1):s_len][None, None]
    prompt_c_conv = z[s_len - (C_CONV - 1):s_len][None, None]
    sample_a_k = sample_rows(k32).reshape(1, ndb, nds, *kv_shape)
    sample_a_v = sample_rows(v32).reshape(1, ndb, nds, *kv_shape)
    sample_a_kidx = sample_rows(ki32)[None]
    sample_b_conv = sample_rows(u)[None, :, nds - (B_CONV - 1):]
    sample_c_conv = sample_rows(z)[None, :, nds - (C_CONV - 1):]
    return (y_prompt, y_sample, prompt_a_k, prompt_a_v, prompt_a_kidx, prompt_b_conv, prompt_c_conv,
            sample_a_k, sample_a_v, sample_a_kidx, sample_b_conv, sample_c_conv)
```

```python
import functools
from typing import NamedTuple

import jax
import jax.numpy as jnp
from jax import lax
from jax.experimental import pallas as pl
from jax.experimental.pallas import tpu as pltpu

CHUNK = 64
N_META = 16
EPS = 1e-6
A_HEADS = 16
A_KV_HEADS = 4
A_HEAD_DIM = 128
A_WIDTH = A_HEADS * A_HEAD_DIM
A_KV_WIDTH = A_KV_HEADS * A_HEAD_DIM
A_GROUP = A_HEADS // A_KV_HEADS
IDX_HEADS = 16
IDX_DIM = 128
IDX_Q_WIDTH = IDX_HEADS * IDX_DIM
INDEX_SCALE = (IDX_HEADS * IDX_DIM) ** -0.5
TOPK_MAX = 256
B_CONV = 31
C_CONV = 3

V7X_LANES = 128
V7X_SUBLANES = 8
V7X_VMEM_LIMIT_BYTES = 56 * 1024 * 1024

Q_BLOCK = 128
KEY_BLOCK = 512
COUNT_ROWS = 512
SEQ_TILE = 64
B_HALO = 32
C_HALO = 8
INT_MIN = -2 ** 31
MASKED = -1e30
LOG2_E = 1.4426950408889634

F32 = jnp.float32
BF16 = jnp.bfloat16
I32 = jnp.int32


def _round_up(x, m):
    return (x + m - 1) // m * m


def _row_tile(rows, cap):
    best = 16
    for t in range(16, cap + 1, 16):
        if rows % t == 0:
            best = t
    return best


def _params(sem):
    return pltpu.CompilerParams(dimension_semantics=sem, vmem_limit_bytes=V7X_VMEM_LIMIT_BYTES)


def _rms(x, g):
    return x * lax.rsqrt(jnp.mean(x * x, axis=-1, keepdims=True) + EPS) * g


def _gather_pool_block(xp_ref, xs_ref, meta_ref, x_scr, n_prompt, n_sample):
    i = pl.program_id(0)

    @pl.when(i < n_prompt)
    def _():
        x_scr[...] = xp_ref[...]

    @pl.when((i >= n_prompt) & (i < n_prompt + n_sample))
    def _():
        x_scr[...] = xs_ref[...]

    @pl.when(i == n_prompt + n_sample)
    def _():
        x_scr[0:N_META, :] = meta_ref[...]
        x_scr[N_META:, :] = jnp.zeros((Q_BLOCK - N_META, x_scr.shape[1]), F32)


def _prenorm_kernel(xp_ref, xs_ref, meta_ref, g_ref, h_ref, x_scr, *, n_prompt, n_sample):
    _gather_pool_block(xp_ref, xs_ref, meta_ref, x_scr, n_prompt, n_sample)
    h_ref[...] = _rms(x_scr[...], g_ref[...]).astype(h_ref.dtype)


def _postnorm_prenorm_kernel(xp_ref, xs_ref, meta_ref, y_ref, gpost_ref, gpre_ref, xo_ref, h_ref, x_scr,
                             *, n_prompt, n_sample):
    _gather_pool_block(xp_ref, xs_ref, meta_ref, x_scr, n_prompt, n_sample)
    x1 = x_scr[...] + _rms(y_ref[...], gpost_ref[...])
    xo_ref[...] = x1
    h_ref[...] = _rms(x1, gpre_ref[...]).astype(h_ref.dtype)


def _postnorm_kernel(x_ref, y_ref, gpost_ref, xo_ref):
    xo_ref[...] = x_ref[...] + _rms(y_ref[...], gpost_ref[...])


def _pool_rowwise_call(body, x_prompt, x_sample, meta, pool_inputs, vec_inputs, out_dtypes, name):
    s_len, d = x_prompt.shape
    n_prompt, n_sample = s_len // Q_BLOCK, x_sample.shape[0] // Q_BLOCK
    n_blocks = n_prompt + n_sample + 1
    pool_spec = pl.BlockSpec((Q_BLOCK, d), lambda i: (i, 0))
    vec_spec = pl.BlockSpec((1, d), lambda i: (0, 0))
    return pl.pallas_call(
        functools.partial(body, n_prompt=n_prompt, n_sample=n_sample),
        grid=(n_blocks,),
        in_specs=[
            pl.BlockSpec((Q_BLOCK, d), lambda i: (jnp.minimum(i, n_prompt - 1), 0)),
            pl.BlockSpec((Q_BLOCK, d), lambda i: (jnp.clip(i - n_prompt, 0, n_sample - 1), 0)),
            pl.BlockSpec((N_META, d), lambda i: (0, 0)),
        ] + [pool_spec] * len(pool_inputs) + [vec_spec] * len(vec_inputs),
        out_specs=[pool_spec] * len(out_dtypes),
        out_shape=[jax.ShapeDtypeStruct((n_blocks * Q_BLOCK, d), dt) for dt in out_dtypes],
        scratch_shapes=[pltpu.VMEM((Q_BLOCK, d), F32)],
        compiler_params=_params(("arbitrary",)),
        name=name,
    )(x_prompt, x_sample, meta, *pool_inputs, *[v.reshape(1, d) for v in vec_inputs])


def _final_postnorm_call(x1, y1, g, first_block, n_blocks, name):
    d = x1.shape[1]
    in_spec = pl.BlockSpec((Q_BLOCK, d), lambda i: (i + first_block, 0))
    return pl.pallas_call(
        _postnorm_kernel,
        grid=(n_blocks,),
        in_specs=[in_spec, in_spec, pl.BlockSpec((1, d), lambda i: (0, 0))],
        out_specs=pl.BlockSpec((Q_BLOCK, d), lambda i: (i, 0)),
        out_shape=jax.ShapeDtypeStruct((n_blocks * Q_BLOCK, d), F32),
        compiler_params=_params(("parallel",)),
        name=name,
    )(x1, y1, g.reshape(1, d))


def _matmul_kernel(*refs, n_x, n_w, n_aux, terms, epilogue):
    xs = refs[:n_x]
    ws = refs[n_x:n_x + n_w]
    auxs = refs[n_x + n_w:n_x + n_w + n_aux]
    outs = refs[n_x + n_w + n_aux:]
    accs = [jnp.dot(xs[a][...], ws[b][...], preferred_element_type=F32) for a, b in terms]
    results = epilogue(*accs, *[r[...] for r in auxs])
    for o_ref, r in zip(outs, results):
        o_ref[...] = r.astype(o_ref.dtype)


class _WeightView(NamedTuple):
    array: jax.Array
    row0: int
    rows: int
    col0: int
    cols: int


def _matmul_call(xs, ws, terms, epilogue, out_dtypes, *, bn, auxs=(), name):
    rows = xs[0].shape[0]
    n = ws[0].cols
    bm = _row_tile(rows, 1536)
    bn = max(t for t in range(V7X_LANES, min(bn, n) + 1, V7X_LANES)
             if n % t == 0 and all(w.col0 % t == 0 for w in ws))
    assert all(w.cols == n and w.row0 % w.rows == 0 for w in ws)
    in_specs = [pl.BlockSpec((bm, x.shape[1]), lambda i, j: (i, 0)) for x in xs]
    in_specs += [pl.BlockSpec((w.rows, bn), lambda i, j, w=w: (w.row0 // w.rows, w.col0 // bn + j))
                 for w in ws]
    in_specs += [pl.BlockSpec((1, bn), lambda i, j: (0, j)) for _ in auxs]
    out_spec = pl.BlockSpec((bm, bn), lambda i, j: (i, j))
    body = functools.partial(_matmul_kernel, n_x=len(xs), n_w=len(ws), n_aux=len(auxs),
                             terms=terms, epilogue=epilogue)
    return pl.pallas_call(
        body,
        grid=(rows // bm, n // bn),
        in_specs=in_specs,
        out_specs=[out_spec] * len(out_dtypes),
        out_shape=[jax.ShapeDtypeStruct((rows, n), dt) for dt in out_dtypes],
        compiler_params=_params(("parallel", "arbitrary")),
        name=name,
    )(*xs, *[w.array for w in ws], *[a.reshape(1, n) for a in auxs])


def _silu(x):
    return x * jax.nn.sigmoid(x)


def _ep_q_qi(q, qi):
    return q * (A_HEAD_DIM ** -0.5 * LOG2_E), qi


def _ep_kv(k, v):
    return k, v, k, v


def _ep_kidx(ki, wi, g, b):
    kc = ki - jnp.mean(ki, axis=-1, keepdims=True)
    kn = kc * lax.rsqrt(jnp.mean(kc * kc, axis=-1, keepdims=True) + EPS) * g + b
    return kn, kn, wi * INDEX_SCALE


def _ep_glu(val, glu):
    return (val * jax.nn.sigmoid(glu),)


def _ep_silu(gate):
    return (_silu(gate),)


def _ep_sum(a, b):
    return (a + b,)


def _ep_identity(a):
    return (a,)


def _ep_odd(hx, cg, bg, gate):
    return cg * hx, bg * _silu(gate)


def _prompt_attn_kernel(qs_ref, qi_ref, wi_ref, sg_ref, ki_ref, k_ref, v_ref, o_ref, *scratch,
                        topk):
    step = pl.program_id(0)
    frame = (step - 1) * Q_BLOCK + lax.broadcasted_iota(I32, (1, Q_BLOCK), 1)
    limit = jnp.where(step == 0, 0, Q_BLOCK + CHUNK * (frame // CHUNK + 1))
    n_kt = jnp.where(step == 0, 1, pl.cdiv(Q_BLOCK * (step + 1), KEY_BLOCK))
    _attend(qs_ref, qi_ref, wi_ref, sg_ref, ki_ref, k_ref, v_ref, *scratch,
            topk=topk, n_lo=N_META, limit=limit, n_kt=n_kt)
    o_ref[...] = scratch[-1][...].astype(o_ref.dtype)


def _sample_attn_kernel(qs_ref, qi_ref, wi_ref, sg_ref, cki_ref, ck_ref, cv_ref, nki_ref, nk_ref, nv_ref,
                        o_ref, ki_buf, k_buf, v_buf, *scratch, topk):
    step = pl.program_id(0)
    past, new = cki_ref.shape[0], nk_ref.shape[0]
    for buf, cache_ref, new_ref in ((ki_buf, cki_ref, nki_ref), (k_buf, ck_ref, nk_ref), (v_buf, cv_ref, nv_ref)):
        heads = cache_ref.shape[0] // past
        for r in range(0, past, KEY_BLOCK):
            n = min(KEY_BLOCK, past - r)
            for g in range(heads):
                buf[r:r + n, g * A_HEAD_DIM:(g + 1) * A_HEAD_DIM] = cache_ref[
                    pl.ds(r * heads + g, n, stride=heads), :].astype(BF16)
        buf[past:past + new, :] = new_ref[...]
        buf[past + new:, :] = jnp.zeros((buf.shape[0] - past - new, buf.shape[1]), BF16)
    _attend(qs_ref, qi_ref, wi_ref, sg_ref, ki_buf, k_buf, v_buf, *scratch,
            topk=topk, n_lo=past + new, limit=jnp.zeros((1, Q_BLOCK), I32),
            n_kt=ki_buf.shape[0] // KEY_BLOCK - 1)
    half = pl.multiple_of((step % (Q_BLOCK // new)) * new, new)
    o_ref[...] = scratch[-1][pl.ds(half, new), :].astype(o_ref.dtype)


def _attend(qs_ref, qi_ref, wi_ref, sg_ref, ki_ref, k_ref, v_ref,
            qrows, qirows, keys, bias, s_a, s_b, m_scr, l_scr, acc_scr, o_scr,
            *, topk, n_lo, limit, n_kt):
    nq = Q_BLOCK
    lb = KEY_BLOCK

    one_hot = (lax.broadcasted_iota(I32, (nq, nq), 0) == lax.broadcasted_iota(I32, (nq, nq), 1))
    for h in range(A_HEADS):
        qrows[h * nq:(h + 1) * nq, 0:A_HEAD_DIM] = qs_ref[:, h * A_HEAD_DIM:(h + 1) * A_HEAD_DIM]
        qrows[h * nq:(h + 1) * nq, A_HEAD_DIM:A_HEAD_DIM + nq] = one_hot.astype(BF16)
    for h in range(IDX_HEADS):
        qirows[h * nq:(h + 1) * nq, :] = qi_ref[:, h * IDX_DIM:(h + 1) * IDX_DIM]
    w_t = wi_ref[...].T

    def score_block(kt, carry):
        r0 = pl.multiple_of(kt * lb, lb)
        d = lax.dot_general(ki_ref[pl.ds(r0, lb), :], qirows[...], (((1,), (1,)), ((), ())),
                            preferred_element_type=F32)
        sc = jnp.zeros((lb, nq), F32)
        for h in range(IDX_HEADS):
            sc = sc + jnp.maximum(d[:, h * nq:(h + 1) * nq], 0.0) * w_t[h:h + 1, :]
        pos = r0 + lax.broadcasted_iota(I32, (lb, nq), 0)
        adm = (pos < n_lo) | ((pos >= Q_BLOCK) & (pos < limit))
        bits = lax.bitcast_convert_type(sc, I32)
        key = bits ^ ((bits >> 31) & 0x7FFFFFFF)
        keys[pl.ds(r0, lb), :] = jnp.where(adm, key, INT_MIN)
        return carry

    lax.fori_loop(0, n_kt, score_block, 0)

    n_ct = n_kt * (lb // COUNT_ROWS)

    def count_where(pred):
        def body(c, acc):
            r0 = pl.multiple_of(c * COUNT_ROWS, COUNT_ROWS)
            pos = r0 + lax.broadcasted_iota(I32, (COUNT_ROWS, nq), 0)
            hit = jnp.where(pred(keys[pl.ds(r0, COUNT_ROWS), :], pos), 1.0, 0.0)
            parts = [hit[i * V7X_SUBLANES:(i + 1) * V7X_SUBLANES, :]
                     for i in range(COUNT_ROWS // V7X_SUBLANES)]
            while len(parts) > 1:
                parts = [a + b for a, b in zip(parts[::2], parts[1::2])]
            return acc + parts[0]
        acc = lax.fori_loop(0, n_ct, body, jnp.zeros((V7X_SUBLANES, nq), F32))
        return jnp.sum(acc, axis=0, keepdims=True)

    def search_bit(i, prefix):
        trial = prefix | lax.shift_left(jnp.int32(1), 31 - i)
        cand = trial ^ INT_MIN
        return jnp.where(count_where(lambda k, pos: k >= cand) >= float(topk), trial, prefix)

    prefix = lax.fori_loop(0, 32, search_bit, jnp.zeros((1, nq), I32))
    thr = jnp.maximum(prefix ^ INT_MIN, INT_MIN + 1)

    def write_bias(selected):
        def body(c, carry):
            r0 = pl.multiple_of(c * COUNT_ROWS, COUNT_ROWS)
            pos = r0 + lax.broadcasted_iota(I32, (COUNT_ROWS, nq), 0)
            bias[pl.ds(r0, COUNT_ROWS), :] = jnp.where(
                selected(keys[pl.ds(r0, COUNT_ROWS), :], pos), 0.0, MASKED).astype(BF16)
            return carry
        lax.fori_loop(0, n_ct, body, 0)

    n_above = count_where(lambda k, pos: k > thr)
    n_tied = count_where(lambda k, pos: k == thr)
    keep = float(topk) - n_above
    surplus = jnp.max(jnp.where(n_tied > keep, 1.0, 0.0))

    @pl.when(surplus == 0.0)
    def _():
        write_bias(lambda k, pos: k >= thr)

    @pl.when(surplus > 0.0)
    def _():
        position_bits = keys.shape[0].bit_length()

        def cut_bit(i, prefix):
            trial = prefix | lax.shift_left(jnp.int32(1), position_bits - 1 - i)
            admitted = count_where(lambda k, pos: (k == thr) & (pos < trial))
            return jnp.where(admitted < keep, trial, prefix)
        cut = lax.fori_loop(0, position_bits, cut_bit, jnp.zeros((1, nq), I32)) + 1
        write_bias(lambda k, pos: (k > thr) | ((k == thr) & (pos < cut)))

    gw = A_GROUP * nq
    m_scr[...] = jnp.full(m_scr.shape, MASKED, F32)
    l_scr[...] = jnp.zeros(l_scr.shape, F32)
    acc_scr[...] = jnp.zeros(acc_scr.shape, F32)

    bias[pl.ds(pl.multiple_of(n_kt * lb, lb), lb), :] = jnp.full((lb, nq), MASKED, BF16)

    def score_keys(kt, s_ref):
        r0 = pl.multiple_of(kt * lb, lb)
        bb = bias[pl.ds(r0, lb), :]
        for g in range(A_KV_HEADS):
            k_aug = jnp.concatenate(
                [k_ref[pl.ds(r0, lb), g * A_HEAD_DIM:(g + 1) * A_HEAD_DIM], bb], axis=1)
            s_ref[g] = lax.dot_general(k_aug, qrows[g * gw:(g + 1) * gw, :],
                                       (((1,), (1,)), ((), ())),
                                       preferred_element_type=F32)

    def softmax_pv(kt, s_ref):
        for g in range(A_KV_HEADS):
            s = s_ref[g]
            m_old = m_scr[g]
            m_new = jnp.maximum(m_old, jnp.max(s, axis=0, keepdims=True))
            alpha = jnp.exp2(m_old - m_new)
            p = jnp.exp2(s - m_new)
            l_scr[g] = l_scr[g] * alpha + jnp.sum(p, axis=0, keepdims=True)
            v_blk = v_ref[pl.ds(pl.multiple_of(kt * lb, lb), lb), g * A_HEAD_DIM:(g + 1) * A_HEAD_DIM]
            acc_scr[g] = acc_scr[g] * alpha + lax.dot_general(
                v_blk, p.astype(BF16), (((0,), (0,)), ((), ())), preferred_element_type=F32)
            m_scr[g] = m_new

    score_keys(0, s_a)

    def attend_pair(j, carry):
        kt = 2 * j
        score_keys(kt + 1, s_b)
        softmax_pv(kt, s_a)
        score_keys(jnp.minimum(kt + 2, n_kt), s_a)
        softmax_pv(kt + 1, s_b)
        return carry

    lax.fori_loop(0, (n_kt + 1) // 2, attend_pair, 0)
    for g in range(A_KV_HEADS):
        o_t = acc_scr[g] / l_scr[g]
        for r in range(A_GROUP):
            h = g * A_GROUP + r
            cols = slice(h * A_HEAD_DIM, (h + 1) * A_HEAD_DIM)
            o_scr[:, cols] = o_t[:, r * nq:(r + 1) * nq].T * sg_ref[:, cols]


def _attn_scratch(lpad):
    gw = A_GROUP * Q_BLOCK
    return [
        pltpu.VMEM((A_HEADS * Q_BLOCK, A_HEAD_DIM + Q_BLOCK), BF16),
        pltpu.VMEM((IDX_HEADS * Q_BLOCK, IDX_DIM), BF16),
        pltpu.VMEM((lpad, Q_BLOCK), I32),
        pltpu.VMEM((lpad, Q_BLOCK), BF16),
        pltpu.VMEM((A_KV_HEADS, KEY_BLOCK, gw), F32),
        pltpu.VMEM((A_KV_HEADS, KEY_BLOCK, gw), F32),
        pltpu.VMEM((A_KV_HEADS, 1, gw), F32),
        pltpu.VMEM((A_KV_HEADS, 1, gw), F32),
        pltpu.VMEM((A_KV_HEADS, A_HEAD_DIM, gw), F32),
        pltpu.VMEM((Q_BLOCK, A_WIDTH), F32),
    ]


def _query_specs(q_index):
    qspec = pl.BlockSpec((Q_BLOCK, A_WIDTH), lambda s: (q_index(s), 0))
    return [qspec, qspec, pl.BlockSpec((Q_BLOCK, V7X_LANES), lambda s: (q_index(s), 0)), qspec]


def _prompt_attn_call(qs, qi, wi, sg, ki_seq, k_seq, v_seq, *, s_len, meta_block, topk):
    lpad = ki_seq.shape[0]
    n_qb = s_len // Q_BLOCK
    full = lambda a: pl.BlockSpec(a.shape, lambda s: (0, 0))
    return pl.pallas_call(
        functools.partial(_prompt_attn_kernel, topk=topk),
        grid=(1 + n_qb,),
        in_specs=_query_specs(lambda s: jnp.where(s == 0, meta_block, s - 1))
        + [full(ki_seq), full(k_seq), full(v_seq)],
        out_specs=pl.BlockSpec((Q_BLOCK, A_WIDTH), lambda s: (jnp.where(s == 0, n_qb, s - 1), 0)),
        out_shape=jax.ShapeDtypeStruct((s_len + Q_BLOCK, A_WIDTH), BF16),
        scratch_shapes=_attn_scratch(lpad),
        compiler_params=_params(("arbitrary",)),
        name="prompt_attention",
    )(qs, qi, wi, sg, ki_seq, k_seq, v_seq)


def _sample_attn_call(qs, qi, wi, sg, cache_ki, cache_k, cache_v, ki16, k16, v16, *, s_len, topk):
    ndb, past, _ = cache_ki.shape
    new = SEQ_TILE
    lpad = _round_up(past + new, KEY_BLOCK) + KEY_BLOCK
    cache_spec = lambda a: pl.BlockSpec((None,) + a.shape[1:], lambda s: (s, 0, 0))
    new_spec = lambda a: pl.BlockSpec((new, a.shape[1]), lambda s: (s_len // new + s, 0))
    return pl.pallas_call(
        functools.partial(_sample_attn_kernel, topk=topk),
        grid=(ndb,),
        in_specs=_query_specs(lambda s: s_len // Q_BLOCK + s // (Q_BLOCK // new))
        + [cache_spec(cache_ki), cache_spec(cache_k), cache_spec(cache_v),
           new_spec(ki16), new_spec(k16), new_spec(v16)],
        out_specs=pl.BlockSpec((new, A_WIDTH), lambda s: (s, 0)),
        out_shape=jax.ShapeDtypeStruct((ndb * new, A_WIDTH), BF16),
        scratch_shapes=[pltpu.VMEM((lpad, IDX_DIM), BF16), pltpu.VMEM((lpad, A_KV_WIDTH), BF16),
                        pltpu.VMEM((lpad, A_KV_WIDTH), BF16)] + _attn_scratch(lpad),
        compiler_params=_params(("arbitrary",)),
        name="sample_attention",
    )(qs, qi, wi, sg, cache_ki, cache_k, cache_v, ki16, k16, v16)


def _load_history(step, n_prompt_tiles, halo, ext, state_ref, cur_ref):
    @pl.when(step == 0)
    def _():
        ext[0:halo, :] = jnp.zeros((halo, ext.shape[1]), F32)

    @pl.when(step > n_prompt_tiles)
    def _():
        ext[0:halo, :] = state_ref[...]

    ext[halo:halo + SEQ_TILE, :] = cur_ref[...]


def _save_history(step, halo, ext):
    @pl.when(step == 0)
    def _():
        if halo > N_META:
            ext[0:halo - N_META, :] = jnp.zeros((halo - N_META, ext.shape[1]), F32)
        keep = min(halo, N_META)
        ext[halo - keep:halo, :] = ext[halo + N_META - keep:halo + N_META, :]

    @pl.when(step > 0)
    def _():
        ext[0:halo, :] = ext[SEQ_TILE:SEQ_TILE + halo, :]


def _causal_taps(ext, w_ref, halo, width, cols):
    first = halo - (width - 1)
    x = ext[:, cols]
    rows = x.shape[0]
    rotated = {0: x}
    for j in range(width):
        r = (first + j) % V7X_SUBLANES
        if r not in rotated:
            rotated[r] = pltpu.roll(x, rows - r, axis=0)
    acc = None
    for j in range(width):
        r = (first + j) % V7X_SUBLANES
        base = first + j - r
        term = rotated[r][base:base + SEQ_TILE, :] * w_ref[j:j + 1, cols]
        acc = term if acc is None else acc + term
    return acc


def _bconv_kernel(u_ref, st_ref, sg_ref, w_ref, cb_ref, g_ref, b_ref, o_ref, ext, y_scr,
                  *, n_prompt_tiles):
    step = pl.program_id(0)
    width = u_ref.shape[1]
    _load_history(step, n_prompt_tiles, B_HALO, ext, st_ref, u_ref)
    total = jnp.zeros((SEQ_TILE, V7X_LANES), F32)
    for c in range(width // V7X_LANES):
        cols = slice(c * V7X_LANES, (c + 1) * V7X_LANES)
        y = _causal_taps(ext, w_ref, B_HALO, B_CONV, cols) + cb_ref[:, cols]
        y_scr[:, cols] = y
        total = total + y
    mean = jnp.sum(total, axis=-1, keepdims=True) / width
    sq = jnp.zeros((SEQ_TILE, V7X_LANES), F32)
    for c in range(width // V7X_LANES):
        cols = slice(c * V7X_LANES, (c + 1) * V7X_LANES)
        yc = y_scr[:, cols] - mean
        sq = sq + yc * yc
    inv = lax.rsqrt(jnp.sum(sq, axis=-1, keepdims=True) / width + EPS)
    for c in range(width // V7X_LANES):
        cols = slice(c * V7X_LANES, (c + 1) * V7X_LANES)
        yn = (y_scr[:, cols] - mean) * inv * g_ref[:, cols] + b_ref[:, cols]
        o_ref[:, cols] = (_silu(yn) * sg_ref[:, cols]).astype(o_ref.dtype)
    _save_history(step, B_HALO, ext)


def _cconv_kernel(z_ref, st_ref, bgs_ref, w_ref, cb_ref, o_ref, ext, *, n_prompt_tiles):
    step = pl.program_id(0)
    width = z_ref.shape[1]
    _load_history(step, n_prompt_tiles, C_HALO, ext, st_ref, z_ref)
    for c in range(width // V7X_LANES):
        cols = slice(c * V7X_LANES, (c + 1) * V7X_LANES)
        y = _causal_taps(ext, w_ref, C_HALO, C_CONV, cols) + cb_ref[:, cols]
        o_ref[:, cols] = (bgs_ref[:, cols] * y).astype(o_ref.dtype)
    _save_history(step, C_HALO, ext)


def _stream_conv_call(body, x, state, gate, gate_col_block, vecs, *, halo, n_prompt, n_sample,
                      scratch, name):
    rows, width = x.shape
    n_tiles = n_prompt + n_sample
    assert rows == (n_tiles + 2) * SEQ_TILE
    tile = lambda s: jnp.where(s == 0, n_tiles, jnp.where(s > n_tiles, s, s - 1))
    row_spec = pl.BlockSpec((SEQ_TILE, width), lambda s: (tile(s), 0))
    vec_specs = [pl.BlockSpec(v.shape, lambda s: (0, 0)) for v in vecs]
    return pl.pallas_call(
        functools.partial(body, n_prompt_tiles=n_prompt),
        grid=(2 + n_tiles,),
        in_specs=[
            row_spec,
            pl.BlockSpec((None, halo, width), lambda s: (jnp.clip(s - 1 - n_prompt, 0, n_sample - 1), 0, 0)),
            pl.BlockSpec((SEQ_TILE, width), lambda s: (tile(s), gate_col_block)),
        ] + vec_specs,
        out_specs=row_spec,
        out_shape=jax.ShapeDtypeStruct((rows, width), BF16),
        scratch_shapes=[pltpu.VMEM((halo + SEQ_TILE, width), F32)] + scratch,
        compiler_params=_params(("arbitrary",)),
        name=name,
    )(x, state, gate, *vecs)


def _pad_rows_front(a, rows):
    return jnp.pad(a, ((0, 0), (rows - a.shape[1], 0), (0, 0)))


def kernel(x_prompt, x_sample, cache_a_k, cache_a_v, cache_a_kidx, state_b_conv, state_c_conv, meta,
           e_norm_pre, e_w_in, e_kidx_ln_g, e_kidx_ln_b, e_bconv_w, e_bconv_b, e_bln_g, e_bln_b,
           e_w_out, e_norm_post, o_norm_pre, o_w_in, o_conv_w, o_conv_b, o_w_out, o_norm_post):
    nb, s_len, d = x_prompt.shape
    ndb, nds, _ = x_sample.shape
    past = cache_a_k.shape[2]
    bw = d // 2
    assert nb == 1 and e_w_in.shape[0] == 1 and o_w_in.shape[0] == 1
    assert nds == SEQ_TILE and s_len % Q_BLOCK == 0 and (ndb * nds) % Q_BLOCK == 0
    assert s_len >= B_CONV - 1 and A_WIDTH % bw == 0 and bw % V7X_LANES == 0
    ns = ndb * nds
    meta_row = s_len + ns
    rows = meta_row + Q_BLOCK
    topk_p = min(TOPK_MAX, s_len // 4)
    topk_s = min(TOPK_MAX, (past + nds) // 4)

    sources = (x_prompt[0], x_sample.reshape(ns, d), meta)

    w_in = e_w_in[0].astype(BF16)
    lead = (A_WIDTH, A_KV_WIDTH, A_KV_WIDTH, IDX_Q_WIDTH, IDX_DIM)
    starts = [sum(lead[:i]) for i in range(len(lead))]
    w_q, w_k, w_v, w_qi, w_ki = [_WeightView(w_in, 0, d, s, w) for s, w in zip(starts, lead)]
    wi0 = sum(lead)
    w_wi = _WeightView(jnp.pad(w_in[:, wi0:wi0 + IDX_HEADS], ((0, 0), (0, V7X_LANES - IDX_HEADS))),
                       0, d, 0, V7X_LANES)
    w_tail = w_in[:, wi0 + IDX_HEADS:]
    w_val = _WeightView(w_tail, 0, d, 0, bw)
    w_glu = _WeightView(w_tail, 0, d, bw, bw)
    w_gate = _WeightView(w_tail, 0, d, 2 * bw, A_WIDTH + bw)
    w_out = e_w_out[0].astype(BF16)
    w_out_a = _WeightView(w_out, 0, A_WIDTH, 0, d)
    w_out_b = _WeightView(w_out, A_WIDTH, bw, 0, d)

    h0, = _pool_rowwise_call(_prenorm_kernel, *sources, [], [e_norm_pre[0]], [BF16], "even_prenorm")
    q_s, q_i = _matmul_call([h0], [w_q, w_qi], [(0, 0), (0, 1)], _ep_q_qi, [BF16, BF16], bn=512,
                            name="even_q_qidx")
    k32, v32, k16, v16 = _matmul_call([h0], [w_k, w_v], [(0, 0), (0, 1)], _ep_kv,
                                      [F32, F32, BF16, BF16], bn=512, name="even_kv")
    ki32, ki16, wi = _matmul_call([h0], [w_ki, w_wi], [(0, 0), (0, 1)], _ep_kidx, [F32, BF16, F32],
                                  bn=V7X_LANES, auxs=[e_kidx_ln_g[0], e_kidx_ln_b[0]], name="even_kidx")
    u, = _matmul_call([h0], [w_val, w_glu], [(0, 0), (0, 1)], _ep_glu, [F32], bn=512, name="even_glu")
    sg, = _matmul_call([h0], [w_gate], [(0, 0)], _ep_silu, [F32], bn=512, name="even_gate")

    lpad_p = _round_up(Q_BLOCK + s_len, KEY_BLOCK) + KEY_BLOCK
    seq = lambda a: jnp.concatenate(
        [a[meta_row:meta_row + Q_BLOCK], a[:s_len],
         jnp.zeros((lpad_p - Q_BLOCK - s_len, a.shape[1]), a.dtype)], axis=0)
    oa_p = _prompt_attn_call(q_s, q_i, wi, sg, seq(ki16), seq(k16), seq(v16),
                             s_len=s_len, meta_block=meta_row // Q_BLOCK, topk=topk_p)
    oa_s = _sample_attn_call(q_s, q_i, wi, sg, cache_a_kidx[0],
                             cache_a_k[0].reshape(ndb, past * A_KV_HEADS, A_HEAD_DIM),
                             cache_a_v[0].reshape(ndb, past * A_KV_HEADS, A_HEAD_DIM), ki16, k16, v16,
                             s_len=s_len, topk=topk_s)
    oa = jnp.concatenate([oa_p[:s_len], oa_s, oa_p[s_len:]], axis=0)

    n_prompt_tiles = s_len // SEQ_TILE
    bconv_w = jnp.pad(e_bconv_w[0], ((0, B_HALO - B_CONV), (0, 0)))
    ob = _stream_conv_call(
        _bconv_kernel, u, _pad_rows_front(state_b_conv[0], B_HALO), sg, A_WIDTH // bw,
        [bconv_w, e_bconv_b[0].reshape(1, bw), e_bln_g[0].reshape(1, bw), e_bln_b[0].reshape(1, bw)],
        halo=B_HALO, n_prompt=n_prompt_tiles, n_sample=ndb,
        scratch=[pltpu.VMEM((SEQ_TILE, bw), F32)], name="conformer_conv")

    y0, = _matmul_call([oa, ob], [w_out_a, w_out_b], [(0, 0), (1, 1)], _ep_sum, [F32],
                       bn=512, name="even_out")
    x1, h1 = _pool_rowwise_call(_postnorm_prenorm_kernel, *sources, [y0],
                                [e_norm_post[0], o_norm_pre[0]], [F32, BF16], "even_postnorm_odd_prenorm")

    w_odd = o_w_in[0].astype(BF16)
    w_hx, w_cg, w_bg, w_og = [_WeightView(w_odd, 0, d, i * d, d) for i in range(4)]
    z, bgs = _matmul_call([h1], [w_hx, w_cg, w_bg, w_og], [(0, 0), (0, 1), (0, 2), (0, 3)], _ep_odd,
                          [F32, F32], bn=256, name="odd_in")
    cconv_w = jnp.pad(o_conv_w[0], ((0, C_HALO - C_CONV), (0, 0)))
    o1 = _stream_conv_call(
        _cconv_kernel, z, _pad_rows_front(state_c_conv[0], C_HALO), bgs, 0,
        [cconv_w, o_conv_b[0].reshape(1, d)],
        halo=C_HALO, n_prompt=n_prompt_tiles, n_sample=ndb, scratch=[], name="short_conv")
    y1, = _matmul_call([o1], [_WeightView(o_w_out[0].astype(BF16), 0, d, 0, d)], [(0, 0)],
                       _ep_identity, [F32], bn=512, name="odd_out")
    y_prompt = _final_postnorm_call(x1, y1, o_norm_post[0], 0, s_len // Q_BLOCK, "odd_postnorm_prompt")
    y_sample = _final_postnorm_call(x1, y1, o_norm_post[0], s_len // Q_BLOCK, ns // Q_BLOCK,
                                    "odd_postnorm_sample")

    def prompt_rows(a):
        return jnp.concatenate([a[meta_row:meta_row + N_META], a[:s_len]], axis=0)

    def sample_rows(a):
        return a[s_len:meta_row].reshape(ndb, nds, a.shape[1])

    kv_shape = (A_KV_HEADS, A_HEAD_DIM)
    y_prompt = y_prompt[None]
    y_sample = y_sample.reshape(ndb, nds, d)
    prompt_a_k = prompt_rows(k32).reshape(1, 1, N_META + s_len, *kv_shape)
    prompt_a_v = prompt_rows(v32).reshape(1, 1, N_META + s_len, *kv_shape)
    prompt_a_kidx = prompt_rows(ki32)[None, None]
    prompt_b_conv = u[s_len - (B_CONV - 1):s_len][None, None]
    prompt_c_conv = z[s_len - (C_CONV - 1):s_len][None, None]
    sample_a_k = sample_rows(k32).reshape(1, ndb, nds, *kv_shape)
    sample_a_v = sample_rows(v32).reshape(1, ndb, nds, *kv_shape)
    sample_a_kidx = sample_rows(ki32)[None]
    sample_b_conv = sample_rows(u)[None, :, nds - (B_CONV - 1):]
    sample_c_conv = sample_rows(z)[None, :, nds - (C_CONV - 1):]
    return (y_prompt, y_sample, prompt_a_k, prompt_a_v, prompt_a_kidx, prompt_b_conv, prompt_c_conv,
            sample_a_k, sample_a_v, sample_a_kidx, sample_b_conv, sample_c_conv)
```

```python
import functools
from typing import NamedTuple

import jax
import jax.numpy as jnp
from jax import lax
from jax.experimental import pallas as pl
from jax.experimental.pallas import tpu as pltpu

CHUNK = 64
N_META = 16
EPS = 1e-6
A_HEADS = 16
A_KV_HEADS = 4
A_HEAD_DIM = 128
A_WIDTH = A_HEADS * A_HEAD_DIM
A_KV_WIDTH = A_KV_HEADS * A_HEAD_DIM
A_GROUP = A_HEADS // A_KV_HEADS
IDX_HEADS = 16
IDX_DIM = 128
IDX_Q_WIDTH = IDX_HEADS * IDX_DIM
INDEX_SCALE = (IDX_HEADS * IDX_DIM) ** -0.5
TOPK_MAX = 256
B_CONV = 31
C_CONV = 3

V7X_LANES = 128
V7X_SUBLANES = 8
V7X_VMEM_LIMIT_BYTES = 56 * 1024 * 1024

Q_BLOCK = 128
KEY_BLOCK = 512
COUNT_ROWS = 512
SEQ_TILE = 64
B_HALO = 32
C_HALO = 8
INT_MIN = -2 ** 31
MASKED = -1e30
LOG2_E = 1.4426950408889634

F32 = jnp.float32
BF16 = jnp.bfloat16
I32 = jnp.int32


def _round_up(x, m):
    return (x + m - 1) // m * m


def _row_tile(rows, cap):
    best = 16
    for t in range(16, cap + 1, 16):
        if rows % t == 0:
            best = t
    return best


def _params(sem):
    return pltpu.CompilerParams(dimension_semantics=sem, vmem_limit_bytes=V7X_VMEM_LIMIT_BYTES)


def _rms(x, g):
    return x * lax.rsqrt(jnp.mean(x * x, axis=-1, keepdims=True) + EPS) * g


def _gather_pool_block(xp_ref, xs_ref, meta_ref, x_scr, n_prompt, n_sample):
    i = pl.program_id(0)

    @pl.when(i < n_prompt)
    def _():
        x_scr[...] = xp_ref[...]

    @pl.when((i >= n_prompt) & (i < n_prompt + n_sample))
    def _():
        x_scr[...] = xs_ref[...]

    @pl.when(i == n_prompt + n_sample)
    def _():
        x_scr[0:N_META, :] = meta_ref[...]
        x_scr[N_META:, :] = jnp.zeros((Q_BLOCK - N_META, x_scr.shape[1]), F32)


def _prenorm_kernel(xp_ref, xs_ref, meta_ref, g_ref, h_ref, x_scr, *, n_prompt, n_sample):
    _gather_pool_block(xp_ref, xs_ref, meta_ref, x_scr, n_prompt, n_sample)
    h_ref[...] = _rms(x_scr[...], g_ref[...]).astype(h_ref.dtype)


def _postnorm_prenorm_kernel(xp_ref, xs_ref, meta_ref, y_ref, gpost_ref, gpre_ref, xo_ref, h_ref, x_scr,
                             *, n_prompt, n_sample):
    _gather_pool_block(xp_ref, xs_ref, meta_ref, x_scr, n_prompt, n_sample)
    x1 = x_scr[...] + _rms(y_ref[...].astype(F32), gpost_ref[...])
    xo_ref[...] = x1
    h_ref[...] = _rms(x1, gpre_ref[...]).astype(h_ref.dtype)


def _postnorm_kernel(x_ref, y_ref, gpost_ref, xo_ref):
    xo_ref[...] = x_ref[...] + _rms(y_ref[...].astype(F32), gpost_ref[...])


def _pool_rowwise_call(body, x_prompt, x_sample, meta, pool_inputs, vec_inputs, out_dtypes, name):
    s_len, d = x_prompt.shape
    n_prompt, n_sample = s_len // Q_BLOCK, x_sample.shape[0] // Q_BLOCK
    n_blocks = n_prompt + n_sample + 1
    pool_spec = pl.BlockSpec((Q_BLOCK, d), lambda i: (i, 0))
    vec_spec = pl.BlockSpec((1, d), lambda i: (0, 0))
    return pl.pallas_call(
        functools.partial(body, n_prompt=n_prompt, n_sample=n_sample),
        grid=(n_blocks,),
        in_specs=[
            pl.BlockSpec((Q_BLOCK, d), lambda i: (jnp.minimum(i, n_prompt - 1), 0)),
            pl.BlockSpec((Q_BLOCK, d), lambda i: (jnp.clip(i - n_prompt, 0, n_sample - 1), 0)),
            pl.BlockSpec((N_META, d), lambda i: (0, 0)),
        ] + [pool_spec] * len(pool_inputs) + [vec_spec] * len(vec_inputs),
        out_specs=[pool_spec] * len(out_dtypes),
        out_shape=[jax.ShapeDtypeStruct((n_blocks * Q_BLOCK, d), dt) for dt in out_dtypes],
        scratch_shapes=[pltpu.VMEM((Q_BLOCK, d), F32)],
        compiler_params=_params(("arbitrary",)),
        name=name,
    )(x_prompt, x_sample, meta, *pool_inputs, *[v.reshape(1, d) for v in vec_inputs])


def _final_postnorm_call(x1, y1, g, first_block, n_blocks, name):
    d = x1.shape[1]
    in_spec = pl.BlockSpec((Q_BLOCK, d), lambda i: (i + first_block, 0))
    return pl.pallas_call(
        _postnorm_kernel,
        grid=(n_blocks,),
        in_specs=[in_spec, in_spec, pl.BlockSpec((1, d), lambda i: (0, 0))],
        out_specs=pl.BlockSpec((Q_BLOCK, d), lambda i: (i, 0)),
        out_shape=jax.ShapeDtypeStruct((n_blocks * Q_BLOCK, d), F32),
        compiler_params=_params(("parallel",)),
        name=name,
    )(x1, y1, g.reshape(1, d))


def _matmul_kernel(*refs, n_x, n_w, n_aux, terms, epilogue):
    xs = refs[:n_x]
    ws = refs[n_x:n_x + n_w]
    auxs = refs[n_x + n_w:n_x + n_w + n_aux]
    outs = refs[n_x + n_w + n_aux:]
    accs = [jnp.dot(xs[a][...], ws[b][...], preferred_element_type=F32) for a, b in terms]
    results = epilogue(*accs, *[r[...] for r in auxs])
    for o_ref, r in zip(outs, results):
        o_ref[...] = r.astype(o_ref.dtype)


class _WeightView(NamedTuple):
    array: jax.Array
    row0: int
    rows: int
    col0: int
    cols: int


def _matmul_call(xs, ws, terms, epilogue, out_dtypes, *, bn, auxs=(), name):
    rows = xs[0].shape[0]
    n = ws[0].cols
    bm = _row_tile(rows, 1536)
    bn = max(t for t in range(V7X_LANES, min(bn, n) + 1, V7X_LANES)
             if n % t == 0 and all(w.col0 % t == 0 for w in ws))
    assert all(w.cols == n and w.row0 % w.rows == 0 for w in ws)
    in_specs = [pl.BlockSpec((bm, x.shape[1]), lambda i, j: (i, 0)) for x in xs]
    in_specs += [pl.BlockSpec((w.rows, bn), lambda i, j, w=w: (w.row0 // w.rows, w.col0 // bn + j))
                 for w in ws]
    in_specs += [pl.BlockSpec((1, bn), lambda i, j: (0, j)) for _ in auxs]
    out_spec = pl.BlockSpec((bm, bn), lambda i, j: (i, j))
    body = functools.partial(_matmul_kernel, n_x=len(xs), n_w=len(ws), n_aux=len(auxs),
                             terms=terms, epilogue=epilogue)
    return pl.pallas_call(
        body,
        grid=(rows // bm, n // bn),
        in_specs=in_specs,
        out_specs=[out_spec] * len(out_dtypes),
        out_shape=[jax.ShapeDtypeStruct((rows, n), dt) for dt in out_dtypes],
        compiler_params=_params(("parallel", "arbitrary")),
        name=name,
    )(*xs, *[w.array for w in ws], *[a.reshape(1, n) for a in auxs])


def _silu(x):
    return x * jax.nn.sigmoid(x)


def _ep_q_qi(q, qi):
    return q * (A_HEAD_DIM ** -0.5 * LOG2_E), qi


def _ep_kv(k, v):
    return k, v, k, v


def _ep_kidx(ki, wi, g, b):
    kc = ki - jnp.mean(ki, axis=-1, keepdims=True)
    kn = kc * lax.rsqrt(jnp.mean(kc * kc, axis=-1, keepdims=True) + EPS) * g + b
    return kn, kn, wi * INDEX_SCALE


def _ep_glu(val, glu):
    return (val * jax.nn.sigmoid(glu),)


def _ep_silu(gate):
    return (_silu(gate),)


def _ep_sum(a, b):
    return (a + b,)


def _ep_identity(a):
    return (a,)


def _ep_odd(hx, cg, bg, gate):
    return cg * hx, bg * _silu(gate)


def _prompt_attn_kernel(qs_ref, qi_ref, wi_ref, sg_ref, ki_ref, k_ref, v_ref, o_ref, *scratch,
                        topk, n_prompt_blocks):
    step = pl.program_id(0)

    @pl.when(step <= n_prompt_blocks)
    def _():
        frame = (step - 1) * Q_BLOCK + lax.broadcasted_iota(I32, (1, Q_BLOCK), 1)
        limit = jnp.where(step == 0, 0, Q_BLOCK + CHUNK * (frame // CHUNK + 1))
        n_kt = jnp.where(step == 0, 1, pl.cdiv(Q_BLOCK * (step + 1), KEY_BLOCK))
        _attend(qs_ref, qi_ref, wi_ref, sg_ref, ki_ref, k_ref, v_ref, *scratch,
                topk=topk, n_lo=N_META, limit=limit, n_kt=n_kt)
        o_ref[...] = scratch[-1][...].astype(o_ref.dtype)

    @pl.when(step > n_prompt_blocks)
    def _():
        o_ref[...] = jnp.zeros(o_ref.shape, o_ref.dtype)


def _sample_attn_kernel(qs_ref, qi_ref, wi_ref, sg_ref, cki_ref, ck_ref, cv_ref, nki_ref, nk_ref, nv_ref,
                        pool_ref, o_ref, ki_buf, k_buf, v_buf, *scratch, topk):
    del pool_ref
    step = pl.program_id(0)
    past, new = cki_ref.shape[0], nk_ref.shape[0]
    for buf, cache_ref, new_ref in ((ki_buf, cki_ref, nki_ref), (k_buf, ck_ref, nk_ref), (v_buf, cv_ref, nv_ref)):
        heads = cache_ref.shape[0] // past
        for r in range(0, past, KEY_BLOCK):
            n = min(KEY_BLOCK, past - r)
            for g in range(heads):
                buf[r:r + n, g * A_HEAD_DIM:(g + 1) * A_HEAD_DIM] = cache_ref[
                    pl.ds(r * heads + g, n, stride=heads), :].astype(BF16)
        buf[past:past + new, :] = new_ref[...]
        buf[past + new:, :] = jnp.zeros((buf.shape[0] - past - new, buf.shape[1]), BF16)
    _attend(qs_ref, qi_ref, wi_ref, sg_ref, ki_buf, k_buf, v_buf, *scratch,
            topk=topk, n_lo=past + new, limit=jnp.zeros((1, Q_BLOCK), I32),
            n_kt=ki_buf.shape[0] // KEY_BLOCK - 1)
    half = pl.multiple_of((step % (Q_BLOCK // new)) * new, new)
    o_ref[...] = scratch[-1][pl.ds(half, new), :].astype(o_ref.dtype)


def _attend(qs_ref, qi_ref, wi_ref, sg_ref, ki_ref, k_ref, v_ref,
            qrows, qirows, keys, bias, s_a, s_b, m_scr, l_scr, acc_scr, o_scr,
            *, topk, n_lo, limit, n_kt):
    nq = Q_BLOCK
    lb = KEY_BLOCK

    one_hot = (lax.broadcasted_iota(I32, (nq, nq), 0) == lax.broadcasted_iota(I32, (nq, nq), 1))
    for h in range(A_HEADS):
        qrows[h * nq:(h + 1) * nq, 0:A_HEAD_DIM] = qs_ref[:, h * A_HEAD_DIM:(h + 1) * A_HEAD_DIM]
        qrows[h * nq:(h + 1) * nq, A_HEAD_DIM:A_HEAD_DIM + nq] = one_hot.astype(BF16)
    for h in range(IDX_HEADS):
        qirows[h * nq:(h + 1) * nq, :] = qi_ref[:, h * IDX_DIM:(h + 1) * IDX_DIM]
    w_t = wi_ref[...].T

    def score_block(kt, carry):
        r0 = pl.multiple_of(kt * lb, lb)
        d = lax.dot_general(ki_ref[pl.ds(r0, lb), :], qirows[...], (((1,), (1,)), ((), ())),
                            preferred_element_type=F32)
        sc = jnp.zeros((lb, nq), F32)
        for h in range(IDX_HEADS):
            sc = sc + jnp.maximum(d[:, h * nq:(h + 1) * nq], 0.0) * w_t[h:h + 1, :]
        pos = r0 + lax.broadcasted_iota(I32, (lb, nq), 0)
        adm = (pos < n_lo) | ((pos >= Q_BLOCK) & (pos < limit))
        bits = lax.bitcast_convert_type(sc, I32)
        key = bits ^ ((bits >> 31) & 0x7FFFFFFF)
        keys[pl.ds(r0, lb), :] = jnp.where(adm, key, INT_MIN)
        return carry

    lax.fori_loop(0, n_kt, score_block, 0)

    n_ct = n_kt * (lb // COUNT_ROWS)

    def count_where(pred):
        def body(c, acc):
            r0 = pl.multiple_of(c * COUNT_ROWS, COUNT_ROWS)
            pos = r0 + lax.broadcasted_iota(I32, (COUNT_ROWS, nq), 0)
            hit = jnp.where(pred(keys[pl.ds(r0, COUNT_ROWS), :], pos), 1.0, 0.0)
            parts = [hit[i * V7X_SUBLANES:(i + 1) * V7X_SUBLANES, :]
                     for i in range(COUNT_ROWS // V7X_SUBLANES)]
            while len(parts) > 1:
                parts = [a + b for a, b in zip(parts[::2], parts[1::2])]
            return acc + parts[0]
        acc = lax.fori_loop(0, n_ct, body, jnp.zeros((V7X_SUBLANES, nq), F32))
        return jnp.sum(acc, axis=0, keepdims=True)

    def search_bit(i, prefix):
        trial = prefix | lax.shift_left(jnp.int32(1), 31 - i)
        cand = trial ^ INT_MIN
        return jnp.where(count_where(lambda k, pos: k >= cand) >= float(topk), trial, prefix)

    prefix = lax.fori_loop(0, 32, search_bit, jnp.zeros((1, nq), I32))
    thr = jnp.maximum(prefix ^ INT_MIN, INT_MIN + 1)

    def write_bias(selected):
        def body(c, carry):
            r0 = pl.multiple_of(c * COUNT_ROWS, COUNT_ROWS)
            pos = r0 + lax.broadcasted_iota(I32, (COUNT_ROWS, nq), 0)
            bias[pl.ds(r0, COUNT_ROWS), :] = jnp.where(
                selected(keys[pl.ds(r0, COUNT_ROWS), :], pos), 0.0, MASKED).astype(BF16)
            return carry
        lax.fori_loop(0, n_ct, body, 0)

    n_above = count_where(lambda k, pos: k > thr)
    n_tied = count_where(lambda k, pos: k == thr)
    keep = float(topk) - n_above
    surplus = jnp.max(jnp.where(n_tied > keep, 1.0, 0.0))

    @pl.when(surplus == 0.0)
    def _():
        write_bias(lambda k, pos: k >= thr)

    @pl.when(surplus > 0.0)
    def _():
        position_bits = keys.shape[0].bit_length()

        def cut_bit(i, prefix):
            trial = prefix | lax.shift_left(jnp.int32(1), position_bits - 1 - i)
            admitted = count_where(lambda k, pos: (k == thr) & (pos < trial))
            return jnp.where(admitted < keep, trial, prefix)
        cut = lax.fori_loop(0, position_bits, cut_bit, jnp.zeros((1, nq), I32)) + 1
        write_bias(lambda k, pos: (k > thr) | ((k == thr) & (pos < cut)))

    gw = A_GROUP * nq
    m_scr[...] = jnp.full(m_scr.shape, MASKED, F32)
    l_scr[...] = jnp.zeros(l_scr.shape, F32)
    acc_scr[...] = jnp.zeros(acc_scr.shape, F32)

    bias[pl.ds(pl.multiple_of(n_kt * lb, lb), lb), :] = jnp.full((lb, nq), MASKED, BF16)

    def score_keys(kt, s_ref, g):
        r0 = pl.multiple_of(kt * lb, lb)
        k_aug = jnp.concatenate(
            [k_ref[pl.ds(r0, lb), g * A_HEAD_DIM:(g + 1) * A_HEAD_DIM], bias[pl.ds(r0, lb), :]], axis=1)
        s_ref[g] = lax.dot_general(k_aug, qrows[g * gw:(g + 1) * gw, :], (((1,), (1,)), ((), ())),
                                   preferred_element_type=F32)

    def softmax_pv(kt, s_ref, g):
        s = s_ref[g]
        m_old = m_scr[g]
        m_new = jnp.maximum(m_old, jnp.max(s, axis=0, keepdims=True))
        alpha = jnp.exp2(m_old - m_new)
        p = jnp.exp2(s - m_new)
        l_scr[g] = l_scr[g] * alpha + jnp.sum(p, axis=0, keepdims=True)
        v_blk = v_ref[pl.ds(pl.multiple_of(kt * lb, lb), lb), g * A_HEAD_DIM:(g + 1) * A_HEAD_DIM]
        acc_scr[g] = acc_scr[g] * alpha + lax.dot_general(
            v_blk, p.astype(BF16), (((0,), (0,)), ((), ())), preferred_element_type=F32)
        m_scr[g] = m_new

    for g in range(A_KV_HEADS):
        score_keys(0, s_a, g)

    def attend_pair(j, carry):
        kt = 2 * j
        for g in range(A_KV_HEADS):
            score_keys(kt + 1, s_b, g)
        for g in range(A_KV_HEADS):
            softmax_pv(kt, s_a, g)
        for g in range(A_KV_HEADS):
            score_keys(jnp.minimum(kt + 2, n_kt), s_a, g)
        for g in range(A_KV_HEADS):
            softmax_pv(kt + 1, s_b, g)
        return carry

    lax.fori_loop(0, (n_kt + 1) // 2, attend_pair, 0)
    for g in range(A_KV_HEADS):
        o_t = acc_scr[g] / l_scr[g]
        for r in range(A_GROUP):
            h = g * A_GROUP + r
            cols = slice(h * A_HEAD_DIM, (h + 1) * A_HEAD_DIM)
            o_scr[:, cols] = o_t[:, r * nq:(r + 1) * nq].T * sg_ref[:, cols]


def _attn_scratch(lpad):
    gw = A_GROUP * Q_BLOCK
    return [
        pltpu.VMEM((A_HEADS * Q_BLOCK, A_HEAD_DIM + Q_BLOCK), BF16),
        pltpu.VMEM((IDX_HEADS * Q_BLOCK, IDX_DIM), BF16),
        pltpu.VMEM((lpad, Q_BLOCK), I32),
        pltpu.VMEM((lpad, Q_BLOCK), BF16),
        pltpu.VMEM((A_KV_HEADS, KEY_BLOCK, gw), F32),
        pltpu.VMEM((A_KV_HEADS, KEY_BLOCK, gw), F32),
        pltpu.VMEM((A_KV_HEADS, 1, gw), F32),
        pltpu.VMEM((A_KV_HEADS, 1, gw), F32),
        pltpu.VMEM((A_KV_HEADS, A_HEAD_DIM, gw), F32),
        pltpu.VMEM((Q_BLOCK, A_WIDTH), F32),
    ]


def _query_specs(q_index):
    qspec = pl.BlockSpec((Q_BLOCK, A_WIDTH), lambda s: (q_index(s), 0))
    return [qspec, qspec, pl.BlockSpec((Q_BLOCK, V7X_LANES), lambda s: (q_index(s), 0)), qspec]


def _prompt_attn_call(qs, qi, wi, sg, ki_seq, k_seq, v_seq, *, s_len, meta_block, topk):
    lpad = ki_seq.shape[0]
    n_qb = s_len // Q_BLOCK
    full = lambda a: pl.BlockSpec(a.shape, lambda s: (0, 0))
    return pl.pallas_call(
        functools.partial(_prompt_attn_kernel, topk=topk, n_prompt_blocks=n_qb),
        grid=(1 + meta_block,),
        in_specs=_query_specs(lambda s: jnp.where(s == 0, meta_block, s - 1))
        + [full(ki_seq), full(k_seq), full(v_seq)],
        out_specs=pl.BlockSpec((Q_BLOCK, A_WIDTH), lambda s: (jnp.where(s == 0, meta_block, s - 1), 0)),
        out_shape=jax.ShapeDtypeStruct(((meta_block + 1) * Q_BLOCK, A_WIDTH), BF16),
        scratch_shapes=_attn_scratch(lpad),
        compiler_params=_params(("arbitrary",)),
        name="prompt_attention",
    )(qs, qi, wi, sg, ki_seq, k_seq, v_seq)


def _sample_attn_call(qs, qi, wi, sg, cache_ki, cache_k, cache_v, ki16, k16, v16, pool_out,
                      *, s_len, topk):
    ndb, past, _ = cache_ki.shape
    new = SEQ_TILE
    lpad = _round_up(past + new, KEY_BLOCK) + KEY_BLOCK
    cache_spec = lambda a: pl.BlockSpec((None,) + a.shape[1:], lambda s: (s, 0, 0))
    new_spec = lambda a: pl.BlockSpec((new, a.shape[1]), lambda s: (s_len // new + s, 0))
    return pl.pallas_call(
        functools.partial(_sample_attn_kernel, topk=topk),
        grid=(ndb,),
        in_specs=_query_specs(lambda s: s_len // Q_BLOCK + s // (Q_BLOCK // new))
        + [cache_spec(cache_ki), cache_spec(cache_k), cache_spec(cache_v),
           new_spec(ki16), new_spec(k16), new_spec(v16), pl.BlockSpec(memory_space=pl.ANY)],
        out_specs=pl.BlockSpec((new, A_WIDTH), lambda s: (s_len // new + s, 0)),
        out_shape=jax.ShapeDtypeStruct(pool_out.shape, BF16),
        input_output_aliases={10: 0},
        scratch_shapes=[pltpu.VMEM((lpad, IDX_DIM), BF16), pltpu.VMEM((lpad, A_KV_WIDTH), BF16),
                        pltpu.VMEM((lpad, A_KV_WIDTH), BF16)] + _attn_scratch(lpad),
        compiler_params=_params(("arbitrary",)),
        name="sample_attention",
    )(qs, qi, wi, sg, cache_ki, cache_k, cache_v, ki16, k16, v16, pool_out)


def _load_history(step, n_prompt_tiles, halo, ext, state_ref, cur_ref):
    @pl.when(step == 0)
    def _():
        ext[0:halo, :] = jnp.zeros((halo, ext.shape[1]), F32)

    @pl.when(step > n_prompt_tiles)
    def _():
        ext[0:halo, :] = state_ref[...]

    ext[halo:halo + SEQ_TILE, :] = cur_ref[...]


def _save_history(step, halo, ext):
    @pl.when(step == 0)
    def _():
        if halo > N_META:
            ext[0:halo - N_META, :] = jnp.zeros((halo - N_META, ext.shape[1]), F32)
        keep = min(halo, N_META)
        ext[halo - keep:halo, :] = ext[halo + N_META - keep:halo + N_META, :]

    @pl.when(step > 0)
    def _():
        ext[0:halo, :] = ext[SEQ_TILE:SEQ_TILE + halo, :]


def _causal_taps(ext, w_ref, halo, width, cols):
    first = halo - (width - 1)
    x = ext[:, cols]
    rows = x.shape[0]
    rotated = {0: x}
    for j in range(width):
        r = (first + j) % V7X_SUBLANES
        if r not in rotated:
            rotated[r] = pltpu.roll(x, rows - r, axis=0)
    acc = None
    for j in range(width):
        r = (first + j) % V7X_SUBLANES
        base = first + j - r
        term = rotated[r][base:base + SEQ_TILE, :] * w_ref[j:j + 1, cols]
        acc = term if acc is None else acc + term
    return acc


def _bconv_kernel(u_ref, st_ref, sg_ref, w_ref, cb_ref, g_ref, b_ref, o_ref, ext, y_scr,
                  *, n_prompt_tiles):
    step = pl.program_id(0)
    width = u_ref.shape[1]
    _load_history(step, n_prompt_tiles, B_HALO, ext, st_ref, u_ref)
    total = jnp.zeros((SEQ_TILE, V7X_LANES), F32)
    for c in range(width // V7X_LANES):
        cols = slice(c * V7X_LANES, (c + 1) * V7X_LANES)
        y = _causal_taps(ext, w_ref, B_HALO, B_CONV, cols) + cb_ref[:, cols]
        y_scr[:, cols] = y
        total = total + y
    mean = jnp.sum(total, axis=-1, keepdims=True) / width
    sq = jnp.zeros((SEQ_TILE, V7X_LANES), F32)
    for c in range(width // V7X_LANES):
        cols = slice(c * V7X_LANES, (c + 1) * V7X_LANES)
        yc = y_scr[:, cols] - mean
        sq = sq + yc * yc
    inv = lax.rsqrt(jnp.sum(sq, axis=-1, keepdims=True) / width + EPS)
    for c in range(width // V7X_LANES):
        cols = slice(c * V7X_LANES, (c + 1) * V7X_LANES)
        yn = (y_scr[:, cols] - mean) * inv * g_ref[:, cols] + b_ref[:, cols]
        o_ref[:, cols] = (_silu(yn) * sg_ref[:, cols]).astype(o_ref.dtype)
    _save_history(step, B_HALO, ext)


def _cconv_kernel(z_ref, st_ref, bgs_ref, w_ref, cb_ref, o_ref, ext, *, n_prompt_tiles):
    step = pl.program_id(0)
    width = z_ref.shape[1]
    _load_history(step, n_prompt_tiles, C_HALO, ext, st_ref, z_ref)
    for c in range(width // V7X_LANES):
        cols = slice(c * V7X_LANES, (c + 1) * V7X_LANES)
        y = _causal_taps(ext, w_ref, C_HALO, C_CONV, cols) + cb_ref[:, cols]
        o_ref[:, cols] = (bgs_ref[:, cols] * y).astype(o_ref.dtype)
    _save_history(step, C_HALO, ext)


def _stream_conv_call(body, x, state, gate, gate_col_block, vecs, *, halo, n_prompt, n_sample,
                      scratch, name):
    rows, width = x.shape
    n_tiles = n_prompt + n_sample
    assert rows == (n_tiles + 2) * SEQ_TILE
    tile = lambda s: jnp.where(s == 0, n_tiles, jnp.where(s > n_tiles, s, s - 1))
    row_spec = pl.BlockSpec((SEQ_TILE, width), lambda s: (tile(s), 0))
    vec_specs = [pl.BlockSpec(v.shape, lambda s: (0, 0)) for v in vecs]
    return pl.pallas_call(
        functools.partial(body, n_prompt_tiles=n_prompt),
        grid=(2 + n_tiles,),
        in_specs=[
            row_spec,
            pl.BlockSpec((None, halo, width), lambda s: (jnp.clip(s - 1 - n_prompt, 0, n_sample - 1), 0, 0)),
            pl.BlockSpec((SEQ_TILE, width), lambda s: (tile(s), gate_col_block)),
        ] + vec_specs,
        out_specs=row_spec,
        out_shape=jax.ShapeDtypeStruct((rows, width), BF16),
        scratch_shapes=[pltpu.VMEM((halo + SEQ_TILE, width), F32)] + scratch,
        compiler_params=_params(("arbitrary",)),
        name=name,
    )(x, state, gate, *vecs)


def _pad_rows_front(a, rows):
    return jnp.pad(a, ((0, 0), (rows - a.shape[1], 0), (0, 0)))


def kernel(x_prompt, x_sample, cache_a_k, cache_a_v, cache_a_kidx, state_b_conv, state_c_conv, meta,
           e_norm_pre, e_w_in, e_kidx_ln_g, e_kidx_ln_b, e_bconv_w, e_bconv_b, e_bln_g, e_bln_b,
           e_w_out, e_norm_post, o_norm_pre, o_w_in, o_conv_w, o_conv_b, o_w_out, o_norm_post):
    nb, s_len, d = x_prompt.shape
    ndb, nds, _ = x_sample.shape
    past = cache_a_k.shape[2]
    bw = d // 2
    assert nb == 1 and e_w_in.shape[0] == 1 and o_w_in.shape[0] == 1
    assert nds == SEQ_TILE and s_len % Q_BLOCK == 0 and (ndb * nds) % Q_BLOCK == 0
    assert s_len >= B_CONV - 1 and A_WIDTH % bw == 0 and bw % V7X_LANES == 0
    ns = ndb * nds
    meta_row = s_len + ns
    rows = meta_row + Q_BLOCK
    topk_p = min(TOPK_MAX, s_len // 4)
    topk_s = min(TOPK_MAX, (past + nds) // 4)

    sources = (x_prompt[0], x_sample.reshape(ns, d), meta)

    w_in = e_w_in[0].astype(BF16)
    lead = (A_WIDTH, A_KV_WIDTH, A_KV_WIDTH, IDX_Q_WIDTH, IDX_DIM)
    starts = [sum(lead[:i]) for i in range(len(lead))]
    w_q, w_k, w_v, w_qi, w_ki = [_WeightView(w_in, 0, d, s, w) for s, w in zip(starts, lead)]
    wi0 = sum(lead)
    w_wi = _WeightView(jnp.pad(w_in[:, wi0:wi0 + IDX_HEADS], ((0, 0), (0, V7X_LANES - IDX_HEADS))),
                       0, d, 0, V7X_LANES)
    w_tail = w_in[:, wi0 + IDX_HEADS:]
    w_val = _WeightView(w_tail, 0, d, 0, bw)
    w_glu = _WeightView(w_tail, 0, d, bw, bw)
    w_gate = _WeightView(w_tail, 0, d, 2 * bw, A_WIDTH + bw)
    w_out = e_w_out[0].astype(BF16)
    w_out_a = _WeightView(w_out, 0, A_WIDTH, 0, d)
    w_out_b = _WeightView(w_out, A_WIDTH, bw, 0, d)

    h0, = _pool_rowwise_call(_prenorm_kernel, *sources, [], [e_norm_pre[0]], [BF16], "even_prenorm")
    q_s, q_i = _matmul_call([h0], [w_q, w_qi], [(0, 0), (0, 1)], _ep_q_qi, [BF16, BF16], bn=512,
                            name="even_q_qidx")
    k32, v32, k16, v16 = _matmul_call([h0], [w_k, w_v], [(0, 0), (0, 1)], _ep_kv,
                                      [F32, F32, BF16, BF16], bn=512, name="even_kv")
    ki32, ki16, wi = _matmul_call([h0], [w_ki, w_wi], [(0, 0), (0, 1)], _ep_kidx, [F32, BF16, F32],
                                  bn=V7X_LANES, auxs=[e_kidx_ln_g[0], e_kidx_ln_b[0]], name="even_kidx")
    u, = _matmul_call([h0], [w_val, w_glu], [(0, 0), (0, 1)], _ep_glu, [F32], bn=512, name="even_glu")
    sg, = _matmul_call([h0], [w_gate], [(0, 0)], _ep_silu, [F32], bn=1024, name="even_gate")

    lpad_p = _round_up(Q_BLOCK + s_len, KEY_BLOCK) + KEY_BLOCK
    seq = lambda a: jnp.concatenate(
        [a[meta_row:meta_row + Q_BLOCK], a[:s_len],
         jnp.zeros((lpad_p - Q_BLOCK - s_len, a.shape[1]), a.dtype)], axis=0)
    oa = _prompt_attn_call(q_s, q_i, wi, sg, seq(ki16), seq(k16), seq(v16),
                           s_len=s_len, meta_block=meta_row // Q_BLOCK, topk=topk_p)
    oa = _sample_attn_call(q_s, q_i, wi, sg, cache_a_kidx[0],
                           cache_a_k[0].reshape(ndb, past * A_KV_HEADS, A_HEAD_DIM),
                           cache_a_v[0].reshape(ndb, past * A_KV_HEADS, A_HEAD_DIM), ki16, k16, v16, oa,
                           s_len=s_len, topk=topk_s)

    n_prompt_tiles = s_len // SEQ_TILE
    bconv_w = jnp.pad(e_bconv_w[0], ((0, B_HALO - B_CONV), (0, 0)))
    ob = _stream_conv_call(
        _bconv_kernel, u, _pad_rows_front(state_b_conv[0], B_HALO), sg, A_WIDTH // bw,
        [bconv_w, e_bconv_b[0].reshape(1, bw), e_bln_g[0].reshape(1, bw), e_bln_b[0].reshape(1, bw)],
        halo=B_HALO, n_prompt=n_prompt_tiles, n_sample=ndb,
        scratch=[pltpu.VMEM((SEQ_TILE, bw), F32)], name="conformer_conv")

    y0, = _matmul_call([oa, ob], [w_out_a, w_out_b], [(0, 0), (1, 1)], _ep_sum, [BF16],
                       bn=1024, name="even_out")
    x1, h1 = _pool_rowwise_call(_postnorm_prenorm_kernel, *sources, [y0],
                                [e_norm_post[0], o_norm_pre[0]], [F32, BF16], "even_postnorm_odd_prenorm")

    w_odd = o_w_in[0].astype(BF16)
    w_hx, w_cg, w_bg, w_og = [_WeightView(w_odd, 0, d, i * d, d) for i in range(4)]
    z, bgs = _matmul_call([h1], [w_hx, w_cg, w_bg, w_og], [(0, 0), (0, 1), (0, 2), (0, 3)], _ep_odd,
                          [F32, BF16], bn=256, name="odd_in")
    cconv_w = jnp.pad(o_conv_w[0], ((0, C_HALO - C_CONV), (0, 0)))
    o1 = _stream_conv_call(
        _cconv_kernel, z, _pad_rows_front(state_c_conv[0], C_HALO), bgs, 0,
        [cconv_w, o_conv_b[0].reshape(1, d)],
        halo=C_HALO, n_prompt=n_prompt_tiles, n_sample=ndb, scratch=[], name="short_conv")
    y1, = _matmul_call([o1], [_WeightView(o_w_out[0].astype(BF16), 0, d, 0, d)], [(0, 0)],
                       _ep_identity, [BF16], bn=1024, name="odd_out")
    y_prompt = _final_postnorm_call(x1, y1, o_norm_post[0], 0, s_len // Q_BLOCK, "odd_postnorm_prompt")
    y_sample = _final_postnorm_call(x1, y1, o_norm_post[0], s_len // Q_BLOCK, ns // Q_BLOCK,
                                    "odd_postnorm_sample")

    def prompt_rows(a):
        return jnp.concatenate([a[meta_row:meta_row + N_META], a[:s_len]], axis=0)

    def sample_rows(a):
        return a[s_len:meta_row].reshape(ndb, nds, a.shape[1])

    kv_shape = (A_KV_HEADS, A_HEAD_DIM)
    y_prompt = y_prompt[None]
    y_sample = y_sample.reshape(ndb, nds, d)
    prompt_a_k = prompt_rows(k32).reshape(1, 1, N_META + s_len, *kv_shape)
    prompt_a_v = prompt_rows(v32).reshape(1, 1, N_META + s_len, *kv_shape)
    prompt_a_kidx = prompt_rows(ki32)[None, None]
    prompt_b_conv = u[s_len - (B_CONV - 1):s_len][None, None]
    prompt_c_conv = z[s_len - (C_CONV - 1):s_len][None, None]
    sample_a_k = sample_rows(k32).reshape(1, ndb, nds, *kv_shape)
    sample_a_v = sample_rows(v32).reshape(1, ndb, nds, *kv_shape)
    sample_a_kidx = sample_rows(ki32)[None]
    sample_b_conv = sample_rows(u)[None, :, nds - (B_CONV - 1):]
    sample_c_conv = sample_rows(z)[None, :, nds - (C_CONV - 1):]
    return (y_prompt, y_sample, prompt_a_k, prompt_a_v, prompt_a_kidx, prompt_b_conv, prompt_c_conv,
            sample_a_k, sample_a_v, sample_a_kidx, sample_b_conv, sample_c_conv)
```

```python
import functools
from typing import NamedTuple

import jax
import jax.numpy as jnp
from jax import lax
from jax.experimental import pallas as pl
from jax.experimental.pallas import tpu as pltpu

CHUNK = 64
N_META = 16
EPS = 1e-6
A_HEADS = 16
A_KV_HEADS = 4
A_HEAD_DIM = 128
A_WIDTH = A_HEADS * A_HEAD_DIM
A_KV_WIDTH = A_KV_HEADS * A_HEAD_DIM
A_GROUP = A_HEADS // A_KV_HEADS
IDX_HEADS = 16
IDX_DIM = 128
IDX_Q_WIDTH = IDX_HEADS * IDX_DIM
INDEX_SCALE = (IDX_HEADS * IDX_DIM) ** -0.5
TOPK_MAX = 256
B_CONV = 31
C_CONV = 3

V7X_LANES = 128
V7X_SUBLANES = 8
V7X_VMEM_LIMIT_BYTES = 56 * 1024 * 1024

Q_BLOCK = 128
KEY_BLOCK = 512
COUNT_ROWS = 512
SEQ_TILE = 64
B_HALO = 32
C_HALO = 8
INT_MIN = -2 ** 31
MASKED = -1e30
LOG2_E = 1.4426950408889634

F32 = jnp.float32
BF16 = jnp.bfloat16
I32 = jnp.int32


def _round_up(x, m):
    return (x + m - 1) // m * m


def _row_tile(rows, cap):
    best = 16
    for t in range(16, cap + 1, 16):
        if rows % t == 0:
            best = t
    return best


def _params(sem):
    return pltpu.CompilerParams(dimension_semantics=sem, vmem_limit_bytes=V7X_VMEM_LIMIT_BYTES)


def _rms(x, g):
    return x * lax.rsqrt(jnp.mean(x * x, axis=-1, keepdims=True) + EPS) * g


def _gather_pool_block(xp_ref, xs_ref, meta_ref, x_scr, n_prompt, n_sample):
    i = pl.program_id(0)

    @pl.when(i < n_prompt)
    def _():
        x_scr[...] = xp_ref[...]

    @pl.when((i >= n_prompt) & (i < n_prompt + n_sample))
    def _():
        x_scr[...] = xs_ref[...]

    @pl.when(i == n_prompt + n_sample)
    def _():
        x_scr[0:N_META, :] = meta_ref[...]
        x_scr[N_META:, :] = jnp.zeros((Q_BLOCK - N_META, x_scr.shape[1]), F32)


def _prenorm_kernel(xp_ref, xs_ref, meta_ref, g_ref, h_ref, x_scr, *, n_prompt, n_sample):
    _gather_pool_block(xp_ref, xs_ref, meta_ref, x_scr, n_prompt, n_sample)
    h_ref[...] = _rms(x_scr[...], g_ref[...]).astype(h_ref.dtype)


def _postnorm_prenorm_kernel(xp_ref, xs_ref, meta_ref, y_ref, gpost_ref, gpre_ref, xo_ref, h_ref, x_scr,
                             *, n_prompt, n_sample):
    _gather_pool_block(xp_ref, xs_ref, meta_ref, x_scr, n_prompt, n_sample)
    x1 = x_scr[...] + _rms(y_ref[...].astype(F32), gpost_ref[...])
    xo_ref[...] = x1
    h_ref[...] = _rms(x1, gpre_ref[...]).astype(h_ref.dtype)


def _postnorm_kernel(x_ref, y_ref, gpost_ref, xo_ref):
    xo_ref[...] = x_ref[...] + _rms(y_ref[...].astype(F32), gpost_ref[...])


def _pool_rowwise_call(body, x_prompt, x_sample, meta, pool_inputs, vec_inputs, out_dtypes, name):
    s_len, d = x_prompt.shape
    n_prompt, n_sample = s_len // Q_BLOCK, x_sample.shape[0] // Q_BLOCK
    n_blocks = n_prompt + n_sample + 1
    pool_spec = pl.BlockSpec((Q_BLOCK, d), lambda i: (i, 0))
    vec_spec = pl.BlockSpec((1, d), lambda i: (0, 0))
    return pl.pallas_call(
        functools.partial(body, n_prompt=n_prompt, n_sample=n_sample),
        grid=(n_blocks,),
        in_specs=[
            pl.BlockSpec((Q_BLOCK, d), lambda i: (jnp.minimum(i, n_prompt - 1), 0)),
            pl.BlockSpec((Q_BLOCK, d), lambda i: (jnp.clip(i - n_prompt, 0, n_sample - 1), 0)),
            pl.BlockSpec((N_META, d), lambda i: (0, 0)),
        ] + [pool_spec] * len(pool_inputs) + [vec_spec] * len(vec_inputs),
        out_specs=[pool_spec] * len(out_dtypes),
        out_shape=[jax.ShapeDtypeStruct((n_blocks * Q_BLOCK, d), dt) for dt in out_dtypes],
        scratch_shapes=[pltpu.VMEM((Q_BLOCK, d), F32)],
        compiler_params=_params(("arbitrary",)),
        name=name,
    )(x_prompt, x_sample, meta, *pool_inputs, *[v.reshape(1, d) for v in vec_inputs])


def _final_postnorm_call(x1, y1, g, first_block, n_blocks, name):
    d = x1.shape[1]
    in_spec = pl.BlockSpec((Q_BLOCK, d), lambda i: (i + first_block, 0))
    return pl.pallas_call(
        _postnorm_kernel,
        grid=(n_blocks,),
        in_specs=[in_spec, in_spec, pl.BlockSpec((1, d), lambda i: (0, 0))],
        out_specs=pl.BlockSpec((Q_BLOCK, d), lambda i: (i, 0)),
        out_shape=jax.ShapeDtypeStruct((n_blocks * Q_BLOCK, d), F32),
        compiler_params=_params(("parallel",)),
        name=name,
    )(x1, y1, g.reshape(1, d))


def _matmul_kernel(*refs, n_x, n_w, n_aux, terms, epilogue):
    xs = refs[:n_x]
    ws = refs[n_x:n_x + n_w]
    auxs = refs[n_x + n_w:n_x + n_w + n_aux]
    outs = refs[n_x + n_w + n_aux:]
    accs = [jnp.dot(xs[a][...], ws[b][...], preferred_element_type=F32) for a, b in terms]
    results = epilogue(*accs, *[r[...] for r in auxs])
    for o_ref, r in zip(outs, results):
        o_ref[...] = r.astype(o_ref.dtype)


class _WeightView(NamedTuple):
    array: jax.Array
    row0: int
    rows: int
    col0: int
    cols: int


def _matmul_call(xs, ws, terms, epilogue, out_dtypes, *, bn, auxs=(), name):
    rows = xs[0].shape[0]
    n = ws[0].cols
    bm = _row_tile(rows, 1536)
    bn = max(t for t in range(V7X_LANES, min(bn, n) + 1, V7X_LANES)
             if n % t == 0 and all(w.col0 % t == 0 for w in ws))
    assert all(w.cols == n and w.row0 % w.rows == 0 for w in ws)
    in_specs = [pl.BlockSpec((bm, x.shape[1]), lambda i, j: (i, 0)) for x in xs]
    in_specs += [pl.BlockSpec((w.rows, bn), lambda i, j, w=w: (w.row0 // w.rows, w.col0 // bn + j))
                 for w in ws]
    in_specs += [pl.BlockSpec((1, bn), lambda i, j: (0, j)) for _ in auxs]
    out_spec = pl.BlockSpec((bm, bn), lambda i, j: (i, j))
    body = functools.partial(_matmul_kernel, n_x=len(xs), n_w=len(ws), n_aux=len(auxs),
                             terms=terms, epilogue=epilogue)
    return pl.pallas_call(
        body,
        grid=(rows // bm, n // bn),
        in_specs=in_specs,
        out_specs=[out_spec] * len(out_dtypes),
        out_shape=[jax.ShapeDtypeStruct((rows, n), dt) for dt in out_dtypes],
        compiler_params=_params(("parallel", "arbitrary")),
        name=name,
    )(*xs, *[w.array for w in ws], *[a.reshape(1, n) for a in auxs])


def _silu(x):
    return x * jax.nn.sigmoid(x)


def _ep_q_qi(q, qi):
    return q * (A_HEAD_DIM ** -0.5 * LOG2_E), qi


def _ep_kv(k, v):
    return k, v, k, v


def _ep_kidx(ki, wi, g, b):
    kc = ki - jnp.mean(ki, axis=-1, keepdims=True)
    kn = kc * lax.rsqrt(jnp.mean(kc * kc, axis=-1, keepdims=True) + EPS) * g + b
    return kn, kn, wi * INDEX_SCALE


def _ep_glu(val, glu):
    return (val * jax.nn.sigmoid(glu),)


def _ep_silu(gate):
    return (_silu(gate),)


def _ep_sum(a, b):
    return (a + b,)


def _ep_identity(a):
    return (a,)


def _ep_odd(hx, cg, bg, gate):
    return cg * hx, bg * _silu(gate)


def _prompt_attn_kernel(qs_ref, qi_ref, wi_ref, sg_ref, ki_ref, k_ref, v_ref, o_ref, *scratch,
                        topk, n_prompt_blocks):
    step = pl.program_id(0)

    @pl.when(step <= n_prompt_blocks)
    def _():
        frame = (step - 1) * Q_BLOCK + lax.broadcasted_iota(I32, (1, Q_BLOCK), 1)
        limit = jnp.where(step == 0, 0, Q_BLOCK + CHUNK * (frame // CHUNK + 1))
        n_kt = jnp.where(step == 0, 1, pl.cdiv(Q_BLOCK * (step + 1), KEY_BLOCK))
        _attend(qs_ref, qi_ref, wi_ref, sg_ref, ki_ref, k_ref, v_ref, *scratch,
                topk=topk, n_lo=N_META, limit=limit, n_kt=n_kt)
        o_ref[...] = scratch[-1][...].astype(o_ref.dtype)

    @pl.when(step > n_prompt_blocks)
    def _():
        o_ref[...] = jnp.zeros(o_ref.shape, o_ref.dtype)


def _sample_attn_kernel(qs_ref, qi_ref, wi_ref, sg_ref, cki_ref, ck_ref, cv_ref, nki_ref, nk_ref, nv_ref,
                        pool_ref, o_ref, ki_buf, k_buf, v_buf, *scratch, topk):
    del pool_ref
    step = pl.program_id(0)
    past, new = cki_ref.shape[0], nk_ref.shape[0]
    for buf, cache_ref, new_ref in ((ki_buf, cki_ref, nki_ref), (k_buf, ck_ref, nk_ref), (v_buf, cv_ref, nv_ref)):
        heads = cache_ref.shape[0] // past
        for r in range(0, past, KEY_BLOCK):
            n = min(KEY_BLOCK, past - r)
            for g in range(heads):
                buf[r:r + n, g * A_HEAD_DIM:(g + 1) * A_HEAD_DIM] = cache_ref[
                    pl.ds(r * heads + g, n, stride=heads), :].astype(BF16)
        buf[past:past + new, :] = new_ref[...]
        buf[past + new:, :] = jnp.zeros((buf.shape[0] - past - new, buf.shape[1]), BF16)
    _attend(qs_ref, qi_ref, wi_ref, sg_ref, ki_buf, k_buf, v_buf, *scratch,
            topk=topk, n_lo=past + new, limit=jnp.zeros((1, Q_BLOCK), I32),
            n_kt=ki_buf.shape[0] // KEY_BLOCK - 1)
    half = pl.multiple_of((step % (Q_BLOCK // new)) * new, new)
    o_ref[...] = scratch[-1][pl.ds(half, new), :].astype(o_ref.dtype)


def _attend(qs_ref, qi_ref, wi_ref, sg_ref, ki_ref, k_ref, v_ref,
            qrows, qirows, keys, bias, s_a, s_b, m_scr, l_scr, acc_scr, o_scr,
            *, topk, n_lo, limit, n_kt):
    nq = Q_BLOCK
    lb = KEY_BLOCK

    one_hot = (lax.broadcasted_iota(I32, (nq, nq), 0) == lax.broadcasted_iota(I32, (nq, nq), 1))
    for h in range(A_HEADS):
        qrows[h * nq:(h + 1) * nq, 0:A_HEAD_DIM] = qs_ref[:, h * A_HEAD_DIM:(h + 1) * A_HEAD_DIM]
        qrows[h * nq:(h + 1) * nq, A_HEAD_DIM:A_HEAD_DIM + nq] = one_hot.astype(BF16)
    for h in range(IDX_HEADS):
        qirows[h * nq:(h + 1) * nq, :] = qi_ref[:, h * IDX_DIM:(h + 1) * IDX_DIM]
    w_t = wi_ref[...].T

    ib = 2 * lb

    def score_block(kt, carry):
        r0 = pl.multiple_of(kt * ib, ib)
        d = lax.dot_general(ki_ref[pl.ds(r0, ib), :], qirows[...], (((1,), (1,)), ((), ())),
                            preferred_element_type=F32)
        sc = jnp.zeros((ib, nq), F32)
        for h in range(IDX_HEADS):
            sc = sc + jnp.maximum(d[:, h * nq:(h + 1) * nq], 0.0) * w_t[h:h + 1, :]
        pos = r0 + lax.broadcasted_iota(I32, (ib, nq), 0)
        adm = (pos < n_lo) | ((pos >= Q_BLOCK) & (pos < limit))
        bits = lax.bitcast_convert_type(sc, I32)
        key = bits ^ ((bits >> 31) & 0x7FFFFFFF)
        keys[pl.ds(r0, ib), :] = jnp.where(adm, key, INT_MIN)
        return carry

    lax.fori_loop(0, (n_kt + 1) // 2, score_block, 0)

    n_ct = n_kt * (lb // COUNT_ROWS)

    def count_where(pred):
        def body(c, acc):
            r0 = pl.multiple_of(c * COUNT_ROWS, COUNT_ROWS)
            pos = r0 + lax.broadcasted_iota(I32, (COUNT_ROWS, nq), 0)
            hit = jnp.where(pred(keys[pl.ds(r0, COUNT_ROWS), :], pos), 1.0, 0.0)
            parts = [hit[i * V7X_SUBLANES:(i + 1) * V7X_SUBLANES, :]
                     for i in range(COUNT_ROWS // V7X_SUBLANES)]
            while len(parts) > 1:
                parts = [a + b for a, b in zip(parts[::2], parts[1::2])]
            return acc + parts[0]
        acc = lax.fori_loop(0, n_ct, body, jnp.zeros((V7X_SUBLANES, nq), F32))
        return jnp.sum(acc, axis=0, keepdims=True)

    def search_bit(i, prefix):
        trial = prefix | lax.shift_left(jnp.int32(1), 31 - i)
        cand = trial ^ INT_MIN
        return jnp.where(count_where(lambda k, pos: k >= cand) >= float(topk), trial, prefix)

    prefix = lax.fori_loop(0, 32, search_bit, jnp.zeros((1, nq), I32))
    thr = jnp.maximum(prefix ^ INT_MIN, INT_MIN + 1)

    def write_bias(selected):
        def body(c, carry):
            r0 = pl.multiple_of(c * COUNT_ROWS, COUNT_ROWS)
            pos = r0 + lax.broadcasted_iota(I32, (COUNT_ROWS, nq), 0)
            bias[pl.ds(r0, COUNT_ROWS), :] = jnp.where(
                selected(keys[pl.ds(r0, COUNT_ROWS), :], pos), 0.0, MASKED).astype(BF16)
            return carry
        lax.fori_loop(0, n_ct, body, 0)

    n_above = count_where(lambda k, pos: k > thr)
    n_tied = count_where(lambda k, pos: k == thr)
    keep = float(topk) - n_above
    surplus = jnp.max(jnp.where(n_tied > keep, 1.0, 0.0))

    @pl.when(surplus == 0.0)
    def _():
        write_bias(lambda k, pos: k >= thr)

    @pl.when(surplus > 0.0)
    def _():
        position_bits = keys.shape[0].bit_length()

        def cut_bit(i, prefix):
            trial = prefix | lax.shift_left(jnp.int32(1), position_bits - 1 - i)
            admitted = count_where(lambda k, pos: (k == thr) & (pos < trial))
            return jnp.where(admitted < keep, trial, prefix)
        cut = lax.fori_loop(0, position_bits, cut_bit, jnp.zeros((1, nq), I32)) + 1
        write_bias(lambda k, pos: (k > thr) | ((k == thr) & (pos < cut)))

    gw = A_GROUP * nq
    m_scr[...] = jnp.full(m_scr.shape, MASKED, F32)
    l_scr[...] = jnp.zeros(l_scr.shape, F32)
    acc_scr[...] = jnp.zeros(acc_scr.shape, F32)

    bias[pl.ds(pl.multiple_of(n_kt * lb, lb), lb), :] = jnp.full((lb, nq), MASKED, BF16)

    def score_keys(kt, s_ref, g):
        r0 = pl.multiple_of(kt * lb, lb)
        k_aug = jnp.concatenate(
            [k_ref[pl.ds(r0, lb), g * A_HEAD_DIM:(g + 1) * A_HEAD_DIM], bias[pl.ds(r0, lb), :]], axis=1)
        s_ref[g] = lax.dot_general(k_aug, qrows[g * gw:(g + 1) * gw, :], (((1,), (1,)), ((), ())),
                                   preferred_element_type=F32)

    def softmax_pv(kt, s_ref, g):
        s = s_ref[g]
        m_old = m_scr[g]
        m_new = jnp.maximum(m_old, jnp.max(s, axis=0, keepdims=True))
        alpha = jnp.exp2(m_old - m_new)
        p = jnp.exp2(s - m_new)
        l_scr[g] = l_scr[g] * alpha + jnp.sum(p, axis=0, keepdims=True)
        v_blk = v_ref[pl.ds(pl.multiple_of(kt * lb, lb), lb), g * A_HEAD_DIM:(g + 1) * A_HEAD_DIM]
        acc_scr[g] = acc_scr[g] * alpha + lax.dot_general(
            v_blk, p.astype(BF16), (((0,), (0,)), ((), ())), preferred_element_type=F32)
        m_scr[g] = m_new

    for g in range(A_KV_HEADS):
        score_keys(0, s_a, g)

    def attend_pair(j, carry):
        kt = 2 * j
        for g in range(A_KV_HEADS):
            score_keys(kt + 1, s_b, g)
        for g in range(A_KV_HEADS):
            softmax_pv(kt, s_a, g)
        for g in range(A_KV_HEADS):
            score_keys(jnp.minimum(kt + 2, n_kt), s_a, g)
        for g in range(A_KV_HEADS):
            softmax_pv(kt + 1, s_b, g)
        return carry

    lax.fori_loop(0, (n_kt + 1) // 2, attend_pair, 0)
    for g in range(A_KV_HEADS):
        o_t = acc_scr[g] / l_scr[g]
        for r in range(A_GROUP):
            h = g * A_GROUP + r
            cols = slice(h * A_HEAD_DIM, (h + 1) * A_HEAD_DIM)
            o_scr[:, cols] = o_t[:, r * nq:(r + 1) * nq].T * sg_ref[:, cols]


def _attn_scratch(lpad):
    gw = A_GROUP * Q_BLOCK
    return [
        pltpu.VMEM((A_HEADS * Q_BLOCK, A_HEAD_DIM + Q_BLOCK), BF16),
        pltpu.VMEM((IDX_HEADS * Q_BLOCK, IDX_DIM), BF16),
        pltpu.VMEM((lpad, Q_BLOCK), I32),
        pltpu.VMEM((lpad, Q_BLOCK), BF16),
        pltpu.VMEM((A_KV_HEADS, KEY_BLOCK, gw), F32),
        pltpu.VMEM((A_KV_HEADS, KEY_BLOCK, gw), F32),
        pltpu.VMEM((A_KV_HEADS, 1, gw), F32),
        pltpu.VMEM((A_KV_HEADS, 1, gw), F32),
        pltpu.VMEM((A_KV_HEADS, A_HEAD_DIM, gw), F32),
        pltpu.VMEM((Q_BLOCK, A_WIDTH), F32),
    ]


def _query_specs(q_index):
    qspec = pl.BlockSpec((Q_BLOCK, A_WIDTH), lambda s: (q_index(s), 0))
    return [qspec, qspec, pl.BlockSpec((Q_BLOCK, V7X_LANES), lambda s: (q_index(s), 0)), qspec]


def _prompt_attn_call(qs, qi, wi, sg, ki_seq, k_seq, v_seq, *, s_len, meta_block, topk, casts=()):
    lpad = ki_seq.shape[0]
    n_qb = s_len // Q_BLOCK
    full = lambda a: pl.BlockSpec(a.shape, lambda s: (0, 0))
    cast_specs, cast_shapes = _cast_specs(casts, 1, n_qb)
    body = functools.partial(_prompt_attn_kernel, topk=topk, n_prompt_blocks=n_qb)
    return pl.pallas_call(
        _with_casts(body, 7, len(casts)),
        grid=(1 + meta_block,),
        in_specs=_query_specs(lambda s: jnp.where(s == 0, meta_block, s - 1))
        + [full(ki_seq), full(k_seq), full(v_seq)] + cast_specs,
        out_specs=[pl.BlockSpec((Q_BLOCK, A_WIDTH), lambda s: (jnp.where(s == 0, meta_block, s - 1), 0))]
        + cast_specs,
        out_shape=[jax.ShapeDtypeStruct(((meta_block + 1) * Q_BLOCK, A_WIDTH), BF16)] + cast_shapes,
        scratch_shapes=_attn_scratch(lpad),
        compiler_params=_params(("arbitrary",)),
        name="prompt_attention",
    )(qs, qi, wi, sg, ki_seq, k_seq, v_seq, *casts)


def _sample_attn_call(qs, qi, wi, sg, cache_ki, cache_k, cache_v, ki16, k16, v16, pool_out,
                      *, s_len, topk):
    ndb, past, _ = cache_ki.shape
    new = SEQ_TILE
    lpad = _round_up(past + new, KEY_BLOCK) + KEY_BLOCK
    cache_spec = lambda a: pl.BlockSpec((None,) + a.shape[1:], lambda s: (s, 0, 0))
    new_spec = lambda a: pl.BlockSpec((new, a.shape[1]), lambda s: (s_len // new + s, 0))
    return pl.pallas_call(
        functools.partial(_sample_attn_kernel, topk=topk),
        grid=(ndb,),
        in_specs=_query_specs(lambda s: s_len // Q_BLOCK + s // (Q_BLOCK // new))
        + [cache_spec(cache_ki), cache_spec(cache_k), cache_spec(cache_v),
           new_spec(ki16), new_spec(k16), new_spec(v16), pl.BlockSpec(memory_space=pl.ANY)],
        out_specs=pl.BlockSpec((new, A_WIDTH), lambda s: (s_len // new + s, 0)),
        out_shape=jax.ShapeDtypeStruct(pool_out.shape, BF16),
        input_output_aliases={10: 0},
        scratch_shapes=[pltpu.VMEM((lpad, IDX_DIM), BF16), pltpu.VMEM((lpad, A_KV_WIDTH), BF16),
                        pltpu.VMEM((lpad, A_KV_WIDTH), BF16)] + _attn_scratch(lpad),
        compiler_params=_params(("arbitrary",)),
        name="sample_attention",
    )(qs, qi, wi, sg, cache_ki, cache_k, cache_v, ki16, k16, v16, pool_out)


def _load_history(step, n_prompt_tiles, halo, ext, state_ref, cur_ref):
    @pl.when(step == 0)
    def _():
        ext[0:halo, :] = jnp.zeros((halo, ext.shape[1]), F32)

    @pl.when(step > n_prompt_tiles)
    def _():
        ext[0:halo, :] = state_ref[...]

    ext[halo:halo + SEQ_TILE, :] = cur_ref[...]


def _save_history(step, halo, ext):
    @pl.when(step == 0)
    def _():
        if halo > N_META:
            ext[0:halo - N_META, :] = jnp.zeros((halo - N_META, ext.shape[1]), F32)
        keep = min(halo, N_META)
        ext[halo - keep:halo, :] = ext[halo + N_META - keep:halo + N_META, :]

    @pl.when(step > 0)
    def _():
        ext[0:halo, :] = ext[SEQ_TILE:SEQ_TILE + halo, :]


def _causal_taps(ext, w_ref, halo, width, cols):
    first = halo - (width - 1)
    x = ext[:, cols]
    rows = x.shape[0]
    rotated = {0: x}
    for j in range(width):
        r = (first + j) % V7X_SUBLANES
        if r not in rotated:
            rotated[r] = pltpu.roll(x, rows - r, axis=0)
    acc = None
    for j in range(width):
        r = (first + j) % V7X_SUBLANES
        base = first + j - r
        term = rotated[r][base:base + SEQ_TILE, :] * w_ref[j:j + 1, cols]
        acc = term if acc is None else acc + term
    return acc


def _bconv_kernel(u_ref, st_ref, sg_ref, w_ref, cb_ref, g_ref, b_ref, o_ref, ext, y_scr,
                  *, n_prompt_tiles):
    step = pl.program_id(0)
    width = u_ref.shape[1]
    _load_history(step, n_prompt_tiles, B_HALO, ext, st_ref, u_ref)
    total = jnp.zeros((SEQ_TILE, V7X_LANES), F32)
    for c in range(width // V7X_LANES):
        cols = slice(c * V7X_LANES, (c + 1) * V7X_LANES)
        y = _causal_taps(ext, w_ref, B_HALO, B_CONV, cols) + cb_ref[:, cols]
        y_scr[:, cols] = y
        total = total + y
    mean = jnp.sum(total, axis=-1, keepdims=True) / width
    sq = jnp.zeros((SEQ_TILE, V7X_LANES), F32)
    for c in range(width // V7X_LANES):
        cols = slice(c * V7X_LANES, (c + 1) * V7X_LANES)
        yc = y_scr[:, cols] - mean
        sq = sq + yc * yc
    inv = lax.rsqrt(jnp.sum(sq, axis=-1, keepdims=True) / width + EPS)
    for c in range(width // V7X_LANES):
        cols = slice(c * V7X_LANES, (c + 1) * V7X_LANES)
        yn = (y_scr[:, cols] - mean) * inv * g_ref[:, cols] + b_ref[:, cols]
        o_ref[:, cols] = (_silu(yn) * sg_ref[:, cols]).astype(o_ref.dtype)
    _save_history(step, B_HALO, ext)


def _cconv_kernel(z_ref, st_ref, bgs_ref, w_ref, cb_ref, o_ref, ext, *, n_prompt_tiles):
    step = pl.program_id(0)
    width = z_ref.shape[1]
    _load_history(step, n_prompt_tiles, C_HALO, ext, st_ref, z_ref)
    for c in range(width // V7X_LANES):
        cols = slice(c * V7X_LANES, (c + 1) * V7X_LANES)
        y = _causal_taps(ext, w_ref, C_HALO, C_CONV, cols) + cb_ref[:, cols]
        o_ref[:, cols] = (bgs_ref[:, cols] * y).astype(o_ref.dtype)
    _save_history(step, C_HALO, ext)


def _cast_specs(weights, first_step, n_steps):
    specs, shapes = [], []
    for w in weights:
        slab = w.shape[0] // n_steps
        assert w.shape[0] % n_steps == 0 and slab % (2 * V7X_SUBLANES) == 0
        specs.append(pl.BlockSpec((slab, w.shape[1]), lambda s: (jnp.clip(s - first_step, 0, n_steps - 1), 0)))
        shapes.append(jax.ShapeDtypeStruct(w.shape, BF16))
    return specs, shapes


def _with_casts(body, n_inputs, n_casts):
    def kernel_body(*refs):
        sources = refs[n_inputs:n_inputs + n_casts]
        results = refs[n_inputs + n_casts + 1:n_inputs + 2 * n_casts + 1]
        for src, dst in zip(sources, results):
            dst[...] = src[...].astype(dst.dtype)
        body(*refs[:n_inputs], refs[n_inputs + n_casts], *refs[n_inputs + 2 * n_casts + 1:])
    return kernel_body


def _stream_conv_call(body, x, state, gate, gate_col_block, vecs, *, halo, n_prompt, n_sample,
                      scratch, name, casts=()):
    rows, width = x.shape
    n_tiles = n_prompt + n_sample
    assert rows == (n_tiles + 2) * SEQ_TILE
    tile = lambda s: jnp.where(s == 0, n_tiles, jnp.where(s > n_tiles, s, s - 1))
    row_spec = pl.BlockSpec((SEQ_TILE, width), lambda s: (tile(s), 0))
    vec_specs = [pl.BlockSpec(v.shape, lambda s: (0, 0)) for v in vecs]
    cast_specs, cast_shapes = _cast_specs(casts, 1, n_prompt)
    return pl.pallas_call(
        _with_casts(functools.partial(body, n_prompt_tiles=n_prompt), 3 + len(vecs), len(casts)),
        grid=(2 + n_tiles,),
        in_specs=[
            row_spec,
            pl.BlockSpec((None, halo, width), lambda s: (jnp.clip(s - 1 - n_prompt, 0, n_sample - 1), 0, 0)),
            pl.BlockSpec((SEQ_TILE, width), lambda s: (tile(s), gate_col_block)),
        ] + vec_specs + cast_specs,
        out_specs=[row_spec] + cast_specs,
        out_shape=[jax.ShapeDtypeStruct((rows, width), BF16)] + cast_shapes,
        scratch_shapes=[pltpu.VMEM((halo + SEQ_TILE, width), F32)] + scratch,
        compiler_params=_params(("arbitrary",)),
        name=name,
    )(x, state, gate, *vecs, *casts)


def _pad_rows_front(a, rows):
    return jnp.pad(a, ((0, 0), (rows - a.shape[1], 0), (0, 0)))


def kernel(x_prompt, x_sample, cache_a_k, cache_a_v, cache_a_kidx, state_b_conv, state_c_conv, meta,
           e_norm_pre, e_w_in, e_kidx_ln_g, e_kidx_ln_b, e_bconv_w, e_bconv_b, e_bln_g, e_bln_b,
           e_w_out, e_norm_post, o_norm_pre, o_w_in, o_conv_w, o_conv_b, o_w_out, o_norm_post):
    nb, s_len, d = x_prompt.shape
    ndb, nds, _ = x_sample.shape
    past = cache_a_k.shape[2]
    bw = d // 2
    assert nb == 1 and e_w_in.shape[0] == 1 and o_w_in.shape[0] == 1
    assert nds == SEQ_TILE and s_len % Q_BLOCK == 0 and (ndb * nds) % Q_BLOCK == 0
    assert s_len >= B_CONV - 1 and A_WIDTH % bw == 0 and bw % V7X_LANES == 0
    ns = ndb * nds
    meta_row = s_len + ns
    rows = meta_row + Q_BLOCK
    topk_p = min(TOPK_MAX, s_len // 4)
    topk_s = min(TOPK_MAX, (past + nds) // 4)

    sources = (x_prompt[0], x_sample.reshape(ns, d), meta)

    w_in = e_w_in[0].astype(BF16)
    lead = (A_WIDTH, A_KV_WIDTH, A_KV_WIDTH, IDX_Q_WIDTH, IDX_DIM)
    starts = [sum(lead[:i]) for i in range(len(lead))]
    w_q, w_k, w_v, w_qi, w_ki = [_WeightView(w_in, 0, d, s, w) for s, w in zip(starts, lead)]
    wi0 = sum(lead)
    w_wi = _WeightView(jnp.pad(w_in[:, wi0:wi0 + IDX_HEADS], ((0, 0), (0, V7X_LANES - IDX_HEADS))),
                       0, d, 0, V7X_LANES)
    w_tail = w_in[:, wi0 + IDX_HEADS:]
    w_val = _WeightView(w_tail, 0, d, 0, bw)
    w_glu = _WeightView(w_tail, 0, d, bw, bw)
    w_gate = _WeightView(w_tail, 0, d, 2 * bw, A_WIDTH + bw)

    h0, = _pool_rowwise_call(_prenorm_kernel, *sources, [], [e_norm_pre[0]], [BF16], "even_prenorm")
    q_s, q_i = _matmul_call([h0], [w_q, w_qi], [(0, 0), (0, 1)], _ep_q_qi, [BF16, BF16], bn=512,
                            name="even_q_qidx")
    k32, v32, k16, v16 = _matmul_call([h0], [w_k, w_v], [(0, 0), (0, 1)], _ep_kv,
                                      [F32, F32, BF16, BF16], bn=512, name="even_kv")
    ki32, ki16, wi = _matmul_call([h0], [w_ki, w_wi], [(0, 0), (0, 1)], _ep_kidx, [F32, BF16, F32],
                                  bn=V7X_LANES, auxs=[e_kidx_ln_g[0], e_kidx_ln_b[0]], name="even_kidx")
    u, = _matmul_call([h0], [w_val, w_glu], [(0, 0), (0, 1)], _ep_glu, [F32], bn=512, name="even_glu")
    sg, = _matmul_call([h0], [w_gate], [(0, 0)], _ep_silu, [F32], bn=1024, name="even_gate")

    lpad_p = _round_up(Q_BLOCK + s_len, KEY_BLOCK) + KEY_BLOCK
    seq = lambda a: jnp.concatenate(
        [a[meta_row:meta_row + Q_BLOCK], a[:s_len],
         jnp.zeros((lpad_p - Q_BLOCK - s_len, a.shape[1]), a.dtype)], axis=0)
    oa, w_out, w_out_odd = _prompt_attn_call(
        q_s, q_i, wi, sg, seq(ki16), seq(k16), seq(v16), s_len=s_len,
        meta_block=meta_row // Q_BLOCK, topk=topk_p, casts=(e_w_out[0], o_w_out[0]))
    w_out_a = _WeightView(w_out, 0, A_WIDTH, 0, d)
    w_out_b = _WeightView(w_out, A_WIDTH, bw, 0, d)
    oa = _sample_attn_call(q_s, q_i, wi, sg, cache_a_kidx[0],
                           cache_a_k[0].reshape(ndb, past * A_KV_HEADS, A_HEAD_DIM),
                           cache_a_v[0].reshape(ndb, past * A_KV_HEADS, A_HEAD_DIM), ki16, k16, v16, oa,
                           s_len=s_len, topk=topk_s)

    n_prompt_tiles = s_len // SEQ_TILE
    bconv_w = jnp.pad(e_bconv_w[0], ((0, B_HALO - B_CONV), (0, 0)))
    ob, w_odd = _stream_conv_call(
        _bconv_kernel, u, _pad_rows_front(state_b_conv[0], B_HALO), sg, A_WIDTH // bw,
        [bconv_w, e_bconv_b[0].reshape(1, bw), e_bln_g[0].reshape(1, bw), e_bln_b[0].reshape(1, bw)],
        halo=B_HALO, n_prompt=n_prompt_tiles, n_sample=ndb,
        scratch=[pltpu.VMEM((SEQ_TILE, bw), F32)], name="conformer_conv", casts=(o_w_in[0],))

    y0, = _matmul_call([oa, ob], [w_out_a, w_out_b], [(0, 0), (1, 1)], _ep_sum, [BF16],
                       bn=1024, name="even_out")
    x1, h1 = _pool_rowwise_call(_postnorm_prenorm_kernel, *sources, [y0],
                                [e_norm_post[0], o_norm_pre[0]], [F32, BF16], "even_postnorm_odd_prenorm")

    w_hx, w_cg, w_bg, w_og = [_WeightView(w_odd, 0, d, i * d, d) for i in range(4)]
    z, bgs = _matmul_call([h1], [w_hx, w_cg, w_bg, w_og], [(0, 0), (0, 1), (0, 2), (0, 3)], _ep_odd,
                          [F32, BF16], bn=256, name="odd_in")
    cconv_w = jnp.pad(o_conv_w[0], ((0, C_HALO - C_CONV), (0, 0)))
    o1, = _stream_conv_call(
        _cconv_kernel, z, _pad_rows_front(state_c_conv[0], C_HALO), bgs, 0,
        [cconv_w, o_conv_b[0].reshape(1, d)],
        halo=C_HALO, n_prompt=n_prompt_tiles, n_sample=ndb, scratch=[], name="short_conv")
    y1, = _matmul_call([o1], [_WeightView(w_out_odd, 0, d, 0, d)], [(0, 0)],
                       _ep_identity, [BF16], bn=1024, name="odd_out")
    y_prompt = _final_postnorm_call(x1, y1, o_norm_post[0], 0, s_len // Q_BLOCK, "odd_postnorm_prompt")
    y_sample = _final_postnorm_call(x1, y1, o_norm_post[0], s_len // Q_BLOCK, ns // Q_BLOCK,
                                    "odd_postnorm_sample")

    def prompt_rows(a):
        return jnp.concatenate([a[meta_row:meta_row + N_META], a[:s_len]], axis=0)

    def sample_rows(a):
        return a[s_len:meta_row].reshape(ndb, nds, a.shape[1])

    kv_shape = (A_KV_HEADS, A_HEAD_DIM)
    y_prompt = y_prompt[None]
    y_sample = y_sample.reshape(ndb, nds, d)
    prompt_a_k = prompt_rows(k32).reshape(1, 1, N_META + s_len, *kv_shape)
    prompt_a_v = prompt_rows(v32).reshape(1, 1, N_META + s_len, *kv_shape)
    prompt_a_kidx = prompt_rows(ki32)[None, None]
    prompt_b_conv = u[s_len - (B_CONV - 1):s_len][None, None]
    prompt_c_conv = z[s_len - (C_CONV - 1):s_len][None, None]
    sample_a_k = sample_rows(k32).reshape(1, ndb, nds, *kv_shape)
    sample_a_v = sample_rows(v32).reshape(1, ndb, nds, *kv_shape)
    sample_a_kidx = sample_rows(ki32)[None]
    sample_b_conv = sample_rows(u)[None, :, nds - (B_CONV - 1):]
    sample_c_conv = sample_rows(z)[None, :, nds - (C_CONV - 1):]
    return (y_prompt, y_sample, prompt_a_k, prompt_a_v, prompt_a_kidx, prompt_b_conv, prompt_c_conv,
            sample_a_k, sample_a_v, sample_a_kidx, sample_b_conv, sample_c_conv)
```

```python
import functools
from typing import NamedTuple

import jax
import jax.numpy as jnp
from jax import lax
from jax.experimental import pallas as pl
from jax.experimental.pallas import tpu as pltpu

CHUNK = 64
N_META = 16
EPS = 1e-6
A_HEADS = 16
A_KV_HEADS = 4
A_HEAD_DIM = 128
A_WIDTH = A_HEADS * A_HEAD_DIM
A_KV_WIDTH = A_KV_HEADS * A_HEAD_DIM
A_GROUP = A_HEADS // A_KV_HEADS
IDX_HEADS = 16
IDX_DIM = 128
IDX_Q_WIDTH = IDX_HEADS * IDX_DIM
INDEX_SCALE = (IDX_HEADS * IDX_DIM) ** -0.5
TOPK_MAX = 256
B_CONV = 31
C_CONV = 3

V7X_LANES = 128
V7X_SUBLANES = 8
V7X_VMEM_LIMIT_BYTES = 56 * 1024 * 1024

Q_BLOCK = 128
KEY_BLOCK = 512
COUNT_ROWS = 512
SEARCH_GROUP = 4
SEQ_TILE = 64
B_HALO = 32
C_HALO = 8
INT_MIN = -2 ** 31
MASKED = -1e30
LOG2_E = 1.4426950408889634

F32 = jnp.float32
BF16 = jnp.bfloat16
I32 = jnp.int32


def _round_up(x, m):
    return (x + m - 1) // m * m


def _row_tile(rows, cap):
    best = 16
    for t in range(16, cap + 1, 16):
        if rows % t == 0:
            best = t
    return best


def _params(sem):
    return pltpu.CompilerParams(dimension_semantics=sem, vmem_limit_bytes=V7X_VMEM_LIMIT_BYTES)


def _rms(x, g):
    return x * lax.rsqrt(jnp.mean(x * x, axis=-1, keepdims=True) + EPS) * g


def _gather_pool_block(xp_ref, xs_ref, meta_ref, x_scr, n_prompt, n_sample):
    i = pl.program_id(0)

    @pl.when(i < n_prompt)
    def _():
        x_scr[...] = xp_ref[...]

    @pl.when((i >= n_prompt) & (i < n_prompt + n_sample))
    def _():
        x_scr[...] = xs_ref[...]

    @pl.when(i == n_prompt + n_sample)
    def _():
        x_scr[0:N_META, :] = meta_ref[...]
        x_scr[N_META:, :] = jnp.zeros((Q_BLOCK - N_META, x_scr.shape[1]), F32)


def _prenorm_kernel(xp_ref, xs_ref, meta_ref, g_ref, h_ref, x_scr, *, n_prompt, n_sample):
    _gather_pool_block(xp_ref, xs_ref, meta_ref, x_scr, n_prompt, n_sample)
    h_ref[...] = _rms(x_scr[...], g_ref[...]).astype(h_ref.dtype)


def _postnorm_prenorm_kernel(xp_ref, xs_ref, meta_ref, y_ref, gpost_ref, gpre_ref, xo_ref, h_ref, x_scr,
                             *, n_prompt, n_sample):
    _gather_pool_block(xp_ref, xs_ref, meta_ref, x_scr, n_prompt, n_sample)
    x1 = x_scr[...] + _rms(y_ref[...].astype(F32), gpost_ref[...])
    xo_ref[...] = x1
    h_ref[...] = _rms(x1, gpre_ref[...]).astype(h_ref.dtype)


def _postnorm_kernel(x_ref, y_ref, gpost_ref, xo_ref):
    xo_ref[...] = x_ref[...] + _rms(y_ref[...].astype(F32), gpost_ref[...])


def _pool_rowwise_call(body, x_prompt, x_sample, meta, pool_inputs, vec_inputs, out_dtypes, name):
    s_len, d = x_prompt.shape
    n_prompt, n_sample = s_len // Q_BLOCK, x_sample.shape[0] // Q_BLOCK
    n_blocks = n_prompt + n_sample + 1
    pool_spec = pl.BlockSpec((Q_BLOCK, d), lambda i: (i, 0))
    vec_spec = pl.BlockSpec((1, d), lambda i: (0, 0))
    return pl.pallas_call(
        functools.partial(body, n_prompt=n_prompt, n_sample=n_sample),
        grid=(n_blocks,),
        in_specs=[
            pl.BlockSpec((Q_BLOCK, d), lambda i: (jnp.minimum(i, n_prompt - 1), 0)),
            pl.BlockSpec((Q_BLOCK, d), lambda i: (jnp.clip(i - n_prompt, 0, n_sample - 1), 0)),
            pl.BlockSpec((N_META, d), lambda i: (0, 0)),
        ] + [pool_spec] * len(pool_inputs) + [vec_spec] * len(vec_inputs),
        out_specs=[pool_spec] * len(out_dtypes),
        out_shape=[jax.ShapeDtypeStruct((n_blocks * Q_BLOCK, d), dt) for dt in out_dtypes],
        scratch_shapes=[pltpu.VMEM((Q_BLOCK, d), F32)],
        compiler_params=_params(("arbitrary",)),
        name=name,
    )(x_prompt, x_sample, meta, *pool_inputs, *[v.reshape(1, d) for v in vec_inputs])


def _final_postnorm_call(x1, y1, g, first_block, n_blocks, name):
    d = x1.shape[1]
    in_spec = pl.BlockSpec((Q_BLOCK, d), lambda i: (i + first_block, 0))
    return pl.pallas_call(
        _postnorm_kernel,
        grid=(n_blocks,),
        in_specs=[in_spec, in_spec, pl.BlockSpec((1, d), lambda i: (0, 0))],
        out_specs=pl.BlockSpec((Q_BLOCK, d), lambda i: (i, 0)),
        out_shape=jax.ShapeDtypeStruct((n_blocks * Q_BLOCK, d), F32),
        compiler_params=_params(("parallel",)),
        name=name,
    )(x1, y1, g.reshape(1, d))


def _matmul_kernel(*refs, n_x, n_w, n_aux, terms, epilogue, transposed):
    xs = refs[:n_x]
    ws = refs[n_x:n_x + n_w]
    auxs = refs[n_x + n_w:n_x + n_w + n_aux]
    outs = refs[n_x + n_w + n_aux:]
    accs = [lax.dot_general(xs[a][...], ws[b][...],
                            (((1,), (1 if transposed[b] else 0,)), ((), ())),
                            preferred_element_type=F32) for a, b in terms]
    results = epilogue(*accs, *[r[...] for r in auxs])
    for o_ref, r in zip(outs, results):
        o_ref[...] = r.astype(o_ref.dtype)


class _WeightView(NamedTuple):
    array: jax.Array
    row0: int
    rows: int
    col0: int
    cols: int
    transposed: bool = False


def _weight_spec(w, bn):
    if w.transposed:
        return pl.BlockSpec(
            (pl.Element(bn), pl.Element(w.rows)),
            lambda i, j: (pl.multiple_of(w.col0 + j * bn, 2 * V7X_SUBLANES), w.row0))
    return pl.BlockSpec((w.rows, bn), lambda i, j: (w.row0 // w.rows, w.col0 // bn + j))


def _matmul_call(xs, ws, terms, epilogue, out_dtypes, *, bn, auxs=(), name):
    rows = xs[0].shape[0]
    n = ws[0].cols
    bm = _row_tile(rows, 1536)
    bn = max(t for t in range(V7X_LANES, min(bn, n) + 1, V7X_LANES)
             if n % t == 0 and all(w.transposed or w.col0 % t == 0 for w in ws))
    assert all(w.cols == n and w.row0 % w.rows == 0 for w in ws)
    assert all(w.col0 % (2 * V7X_SUBLANES) == 0 for w in ws if w.transposed)
    in_specs = [pl.BlockSpec((bm, x.shape[1]), lambda i, j: (i, 0)) for x in xs]
    in_specs += [_weight_spec(w, bn) for w in ws]
    in_specs += [pl.BlockSpec((1, bn), lambda i, j: (0, j)) for _ in auxs]
    out_spec = pl.BlockSpec((bm, bn), lambda i, j: (i, j))
    body = functools.partial(_matmul_kernel, n_x=len(xs), n_w=len(ws), n_aux=len(auxs),
                             terms=terms, epilogue=epilogue,
                             transposed=tuple(w.transposed for w in ws))
    return pl.pallas_call(
        body,
        grid=(rows // bm, n // bn),
        in_specs=in_specs,
        out_specs=[out_spec] * len(out_dtypes),
        out_shape=[jax.ShapeDtypeStruct((rows, n), dt) for dt in out_dtypes],
        compiler_params=_params(("parallel", "arbitrary")),
        name=name,
    )(*xs, *[w.array for w in ws], *[a.reshape(1, n) for a in auxs])


def _silu(x):
    return x * jax.nn.sigmoid(x)


def _ep_q_qi(q, qi):
    return q * (A_HEAD_DIM ** -0.5 * LOG2_E), qi


def _ep_kv(k, v):
    return k, v, k, v


def _ep_kidx(ki, wi, g, b):
    kc = ki - jnp.mean(ki, axis=-1, keepdims=True)
    kn = kc * lax.rsqrt(jnp.mean(kc * kc, axis=-1, keepdims=True) + EPS) * g + b
    return kn, kn, wi * INDEX_SCALE


def _ep_glu(val, glu):
    return (val * jax.nn.sigmoid(glu),)


def _ep_silu(gate):
    return (_silu(gate),)


def _ep_sum(a, b):
    return (a + b,)


def _ep_identity(a):
    return (a,)


def _ep_odd(hx, cg, bg, gate):
    return cg * hx, bg * _silu(gate)


def _prompt_attn_kernel(qs_ref, qi_ref, wi_ref, sg_ref, ki_ref, k_ref, v_ref, o_ref, *scratch,
                        topk, n_prompt_blocks):
    step = pl.program_id(0)

    @pl.when(step <= n_prompt_blocks)
    def _():
        frame = (step - 1) * Q_BLOCK + lax.broadcasted_iota(I32, (1, Q_BLOCK), 1)
        limit = jnp.where(step == 0, 0, Q_BLOCK + CHUNK * (frame // CHUNK + 1))
        n_kt = jnp.where(step == 0, 1, pl.cdiv(Q_BLOCK * (step + 1), KEY_BLOCK))
        _attend(qs_ref, qi_ref, wi_ref, sg_ref, ki_ref, k_ref, v_ref, *scratch,
                topk=topk, n_lo=N_META, limit=limit, n_kt=n_kt)
        o_ref[...] = scratch[-1][...].astype(o_ref.dtype)

    @pl.when(step > n_prompt_blocks)
    def _():
        o_ref[...] = jnp.zeros(o_ref.shape, o_ref.dtype)


def _sample_attn_kernel(qs_ref, qi_ref, wi_ref, sg_ref, cki_ref, ck_ref, cv_ref, nki_ref, nk_ref, nv_ref,
                        pool_ref, o_ref, ki_buf, k_buf, v_buf, *scratch, topk):
    del pool_ref
    step = pl.program_id(0)
    past, new = cki_ref.shape[0], nk_ref.shape[0]
    for buf, cache_ref, new_ref in ((ki_buf, cki_ref, nki_ref), (k_buf, ck_ref, nk_ref), (v_buf, cv_ref, nv_ref)):
        heads = cache_ref.shape[0] // past
        for r in range(0, past, KEY_BLOCK):
            n = min(KEY_BLOCK, past - r)
            for g in range(heads):
                buf[r:r + n, g * A_HEAD_DIM:(g + 1) * A_HEAD_DIM] = cache_ref[
                    pl.ds(r * heads + g, n, stride=heads), :].astype(BF16)
        buf[past:past + new, :] = new_ref[...]
        buf[past + new:, :] = jnp.zeros((buf.shape[0] - past - new, buf.shape[1]), BF16)
    _attend(qs_ref, qi_ref, wi_ref, sg_ref, ki_buf, k_buf, v_buf, *scratch,
            topk=topk, n_lo=past + new, limit=jnp.zeros((1, Q_BLOCK), I32),
            n_kt=ki_buf.shape[0] // KEY_BLOCK - 1)
    half = pl.multiple_of((step % (Q_BLOCK // new)) * new, new)
    o_ref[...] = scratch[-1][pl.ds(half, new), :].astype(o_ref.dtype)


def _attend(qs_ref, qi_ref, wi_ref, sg_ref, ki_ref, k_ref, v_ref,
            qrows, qirows, keys, bias, s_a, s_b, m_scr, l_scr, acc_scr, o_scr,
            *, topk, n_lo, limit, n_kt):
    nq = Q_BLOCK
    lb = KEY_BLOCK

    one_hot = (lax.broadcasted_iota(I32, (nq, nq), 0) == lax.broadcasted_iota(I32, (nq, nq), 1))
    for h in range(A_HEADS):
        qrows[h * nq:(h + 1) * nq, 0:A_HEAD_DIM] = qs_ref[:, h * A_HEAD_DIM:(h + 1) * A_HEAD_DIM]
        qrows[h * nq:(h + 1) * nq, A_HEAD_DIM:A_HEAD_DIM + nq] = one_hot.astype(BF16)
    for h in range(IDX_HEADS):
        qirows[h * nq:(h + 1) * nq, :] = qi_ref[:, h * IDX_DIM:(h + 1) * IDX_DIM]
    w_t = wi_ref[...].T

    ib = 2 * lb

    def score_block(kt, carry):
        r0 = pl.multiple_of(kt * ib, ib)
        d = lax.dot_general(ki_ref[pl.ds(r0, ib), :], qirows[...], (((1,), (1,)), ((), ())),
                            preferred_element_type=F32)
        sc = jnp.zeros((ib, nq), F32)
        for h in range(IDX_HEADS):
            sc = sc + jnp.maximum(d[:, h * nq:(h + 1) * nq], 0.0) * w_t[h:h + 1, :]
        pos = r0 + lax.broadcasted_iota(I32, (ib, nq), 0)
        adm = (pos < n_lo) | ((pos >= Q_BLOCK) & (pos < limit))
        bits = lax.bitcast_convert_type(sc, I32)
        key = bits ^ ((bits >> 31) & 0x7FFFFFFF)
        keys[pl.ds(r0, ib), :] = jnp.where(adm, key, INT_MIN)
        return carry

    lax.fori_loop(0, (n_kt + 1) // 2, score_block, 0)

    n_ct = n_kt * (lb // COUNT_ROWS)

    def count_where(pred):
        def body(c, acc):
            r0 = pl.multiple_of(c * COUNT_ROWS, COUNT_ROWS)
            pos = r0 + lax.broadcasted_iota(I32, (COUNT_ROWS, nq), 0)
            hit = jnp.where(pred(keys[pl.ds(r0, COUNT_ROWS), :], pos), 1.0, 0.0)
            parts = [hit[i * V7X_SUBLANES:(i + 1) * V7X_SUBLANES, :]
                     for i in range(COUNT_ROWS // V7X_SUBLANES)]
            while len(parts) > 1:
                parts = [a + b for a, b in zip(parts[::2], parts[1::2])]
            return acc + parts[0]
        acc = lax.fori_loop(0, n_ct, body, jnp.zeros((V7X_SUBLANES, nq), F32))
        return jnp.sum(acc, axis=0, keepdims=True)

    def search_bit(i, state):
        prefix, n_sel = state
        trial = prefix | lax.shift_left(jnp.int32(1), 31 - i)
        cand = trial ^ INT_MIN
        n_trial = count_where(lambda k, pos: k >= cand)
        take = n_trial >= float(topk)
        return jnp.where(take, trial, prefix), jnp.where(take, n_trial, n_sel)

    def search_bits(state):
        group, prefix, n_sel = state
        prefix, n_sel = lax.fori_loop(
            group * SEARCH_GROUP, (group + 1) * SEARCH_GROUP, search_bit, (prefix, n_sel))
        return group + 1, prefix, n_sel

    def search_open(state):
        group, _, n_sel = state
        return (group < 32 // SEARCH_GROUP) & (jnp.max(n_sel) > float(topk))

    n_admissible = (n_lo + jnp.maximum(limit - Q_BLOCK, 0)).astype(F32)
    _, prefix, _ = lax.while_loop(search_open, search_bits,
                                  (jnp.int32(0), jnp.zeros((1, nq), I32), n_admissible))
    thr = jnp.maximum(prefix ^ INT_MIN, INT_MIN + 1)

    def write_bias(selected):
        def body(c, carry):
            r0 = pl.multiple_of(c * COUNT_ROWS, COUNT_ROWS)
            pos = r0 + lax.broadcasted_iota(I32, (COUNT_ROWS, nq), 0)
            bias[pl.ds(r0, COUNT_ROWS), :] = jnp.where(
                selected(keys[pl.ds(r0, COUNT_ROWS), :], pos), 0.0, MASKED).astype(BF16)
            return carry
        lax.fori_loop(0, n_ct, body, 0)

    n_above = count_where(lambda k, pos: k > thr)
    n_tied = count_where(lambda k, pos: k == thr)
    keep = float(topk) - n_above
    surplus = jnp.max(jnp.where(n_tied > keep, 1.0, 0.0))

    @pl.when(surplus == 0.0)
    def _():
        write_bias(lambda k, pos: k >= thr)

    @pl.when(surplus > 0.0)
    def _():
        position_bits = keys.shape[0].bit_length()

        def cut_bit(i, prefix):
            trial = prefix | lax.shift_left(jnp.int32(1), position_bits - 1 - i)
            admitted = count_where(lambda k, pos: (k == thr) & (pos < trial))
            return jnp.where(admitted < keep, trial, prefix)
        cut = lax.fori_loop(0, position_bits, cut_bit, jnp.zeros((1, nq), I32)) + 1
        write_bias(lambda k, pos: (k > thr) | ((k == thr) & (pos < cut)))

    gw = A_GROUP * nq
    m_scr[...] = jnp.full(m_scr.shape, MASKED, F32)
    l_scr[...] = jnp.zeros(l_scr.shape, F32)
    acc_scr[...] = jnp.zeros(acc_scr.shape, F32)

    bias[pl.ds(pl.multiple_of(n_kt * lb, lb), lb), :] = jnp.full((lb, nq), MASKED, BF16)

    def score_keys(kt, s_ref, g):
        r0 = pl.multiple_of(kt * lb, lb)
        k_aug = jnp.concatenate(
            [k_ref[pl.ds(r0, lb), g * A_HEAD_DIM:(g + 1) * A_HEAD_DIM], bias[pl.ds(r0, lb), :]], axis=1)
        s_ref[g] = lax.dot_general(k_aug, qrows[g * gw:(g + 1) * gw, :], (((1,), (1,)), ((), ())),
                                   preferred_element_type=F32)

    def softmax_pv(kt, s_ref, g):
        s = s_ref[g]
        m_old = m_scr[g]
        m_new = jnp.maximum(m_old, jnp.max(s, axis=0, keepdims=True))
        alpha = jnp.exp2(m_old - m_new)
        p = jnp.exp2(s - m_new)
        l_scr[g] = l_scr[g] * alpha + jnp.sum(p, axis=0, keepdims=True)
        v_blk = v_ref[pl.ds(pl.multiple_of(kt * lb, lb), lb), g * A_HEAD_DIM:(g + 1) * A_HEAD_DIM]
        acc_scr[g] = acc_scr[g] * alpha + lax.dot_general(
            v_blk, p.astype(BF16), (((0,), (0,)), ((), ())), preferred_element_type=F32)
        m_scr[g] = m_new

    for g in range(A_KV_HEADS):
        score_keys(0, s_a, g)

    def attend_pair(j, carry):
        kt = 2 * j
        for g in range(A_KV_HEADS):
            score_keys(kt + 1, s_b, g)
        for g in range(A_KV_HEADS):
            softmax_pv(kt, s_a, g)
        for g in range(A_KV_HEADS):
            score_keys(jnp.minimum(kt + 2, n_kt), s_a, g)
        for g in range(A_KV_HEADS):
            softmax_pv(kt + 1, s_b, g)
        return carry

    lax.fori_loop(0, (n_kt + 1) // 2, attend_pair, 0)
    for g in range(A_KV_HEADS):
        o_t = acc_scr[g] / l_scr[g]
        for r in range(A_GROUP):
            h = g * A_GROUP + r
            cols = slice(h * A_HEAD_DIM, (h + 1) * A_HEAD_DIM)
            o_scr[:, cols] = o_t[:, r * nq:(r + 1) * nq].T * sg_ref[:, cols]


def _attn_scratch(lpad):
    gw = A_GROUP * Q_BLOCK
    return [
        pltpu.VMEM((A_HEADS * Q_BLOCK, A_HEAD_DIM + Q_BLOCK), BF16),
        pltpu.VMEM((IDX_HEADS * Q_BLOCK, IDX_DIM), BF16),
        pltpu.VMEM((lpad, Q_BLOCK), I32),
        pltpu.VMEM((lpad, Q_BLOCK), BF16),
        pltpu.VMEM((A_KV_HEADS, KEY_BLOCK, gw), F32),
        pltpu.VMEM((A_KV_HEADS, KEY_BLOCK, gw), F32),
        pltpu.VMEM((A_KV_HEADS, 1, gw), F32),
        pltpu.VMEM((A_KV_HEADS, 1, gw), F32),
        pltpu.VMEM((A_KV_HEADS, A_HEAD_DIM, gw), F32),
        pltpu.VMEM((Q_BLOCK, A_WIDTH), F32),
    ]


def _query_specs(q_index):
    qspec = pl.BlockSpec((Q_BLOCK, A_WIDTH), lambda s: (q_index(s), 0))
    return [qspec, qspec, pl.BlockSpec((Q_BLOCK, V7X_LANES), lambda s: (q_index(s), 0)), qspec]


def _prompt_attn_call(qs, qi, wi, sg, ki_seq, k_seq, v_seq, *, s_len, meta_block, topk, casts=()):
    lpad = ki_seq.shape[0]
    n_qb = s_len // Q_BLOCK
    full = lambda a: pl.BlockSpec(a.shape, lambda s: (0, 0))
    cast_specs, cast_shapes = _cast_specs(casts, 1, n_qb)
    body = functools.partial(_prompt_attn_kernel, topk=topk, n_prompt_blocks=n_qb)
    return pl.pallas_call(
        _with_casts(body, 7, len(casts)),
        grid=(1 + meta_block,),
        in_specs=_query_specs(lambda s: jnp.where(s == 0, meta_block, s - 1))
        + [full(ki_seq), full(k_seq), full(v_seq)] + cast_specs,
        out_specs=[pl.BlockSpec((Q_BLOCK, A_WIDTH), lambda s: (jnp.where(s == 0, meta_block, s - 1), 0))]
        + cast_specs,
        out_shape=[jax.ShapeDtypeStruct(((meta_block + 1) * Q_BLOCK, A_WIDTH), BF16)] + cast_shapes,
        scratch_shapes=_attn_scratch(lpad),
        compiler_params=_params(("arbitrary",)),
        name="prompt_attention",
    )(qs, qi, wi, sg, ki_seq, k_seq, v_seq, *casts)


def _sample_attn_call(qs, qi, wi, sg, cache_ki, cache_k, cache_v, ki16, k16, v16, pool_out,
                      *, s_len, topk):
    ndb, past, _ = cache_ki.shape
    new = SEQ_TILE
    lpad = _round_up(past + new, KEY_BLOCK) + KEY_BLOCK
    cache_spec = lambda a: pl.BlockSpec((None,) + a.shape[1:], lambda s: (s, 0, 0))
    new_spec = lambda a: pl.BlockSpec((new, a.shape[1]), lambda s: (s_len // new + s, 0))
    return pl.pallas_call(
        functools.partial(_sample_attn_kernel, topk=topk),
        grid=(ndb,),
        in_specs=_query_specs(lambda s: s_len // Q_BLOCK + s // (Q_BLOCK // new))
        + [cache_spec(cache_ki), cache_spec(cache_k), cache_spec(cache_v),
           new_spec(ki16), new_spec(k16), new_spec(v16), pl.BlockSpec(memory_space=pl.ANY)],
        out_specs=pl.BlockSpec((new, A_WIDTH), lambda s: (s_len // new + s, 0)),
        out_shape=jax.ShapeDtypeStruct(pool_out.shape, BF16),
        input_output_aliases={10: 0},
        scratch_shapes=[pltpu.VMEM((lpad, IDX_DIM), BF16), pltpu.VMEM((lpad, A_KV_WIDTH), BF16),
                        pltpu.VMEM((lpad, A_KV_WIDTH), BF16)] + _attn_scratch(lpad),
        compiler_params=_params(("arbitrary",)),
        name="sample_attention",
    )(qs, qi, wi, sg, cache_ki, cache_k, cache_v, ki16, k16, v16, pool_out)


def _load_history(step, n_prompt_tiles, halo, ext, state_ref, cur_ref):
    @pl.when(step == 0)
    def _():
        ext[0:halo, :] = jnp.zeros((halo, ext.shape[1]), F32)

    @pl.when(step > n_prompt_tiles)
    def _():
        ext[0:halo, :] = state_ref[...]

    ext[halo:halo + SEQ_TILE, :] = cur_ref[...]


def _save_history(step, halo, ext):
    @pl.when(step == 0)
    def _():
        if halo > N_META:
            ext[0:halo - N_META, :] = jnp.zeros((halo - N_META, ext.shape[1]), F32)
        keep = min(halo, N_META)
        ext[halo - keep:halo, :] = ext[halo + N_META - keep:halo + N_META, :]

    @pl.when(step > 0)
    def _():
        ext[0:halo, :] = ext[SEQ_TILE:SEQ_TILE + halo, :]


def _causal_taps(ext, w_ref, halo, width, cols):
    first = halo - (width - 1)
    x = ext[:, cols]
    rows = x.shape[0]
    rotated = {0: x}
    for j in range(width):
        r = (first + j) % V7X_SUBLANES
        if r not in rotated:
            rotated[r] = pltpu.roll(x, rows - r, axis=0)
    acc = None
    for j in range(width):
        r = (first + j) % V7X_SUBLANES
        base = first + j - r
        term = rotated[r][base:base + SEQ_TILE, :] * w_ref[j:j + 1, cols]
        acc = term if acc is None else acc + term
    return acc


def _bconv_kernel(u_ref, st_ref, sg_ref, w_ref, cb_ref, g_ref, b_ref, o_ref, ext, y_scr,
                  *, n_prompt_tiles):
    step = pl.program_id(0)
    width = u_ref.shape[1]
    _load_history(step, n_prompt_tiles, B_HALO, ext, st_ref, u_ref)
    total = jnp.zeros((SEQ_TILE, V7X_LANES), F32)
    for c in range(width // V7X_LANES):
        cols = slice(c * V7X_LANES, (c + 1) * V7X_LANES)
        y = _causal_taps(ext, w_ref, B_HALO, B_CONV, cols) + cb_ref[:, cols]
        y_scr[:, cols] = y
        total = total + y
    mean = jnp.sum(total, axis=-1, keepdims=True) / width
    sq = jnp.zeros((SEQ_TILE, V7X_LANES), F32)
    for c in range(width // V7X_LANES):
        cols = slice(c * V7X_LANES, (c + 1) * V7X_LANES)
        yc = y_scr[:, cols] - mean
        sq = sq + yc * yc
    inv = lax.rsqrt(jnp.sum(sq, axis=-1, keepdims=True) / width + EPS)
    for c in range(width // V7X_LANES):
        cols = slice(c * V7X_LANES, (c + 1) * V7X_LANES)
        yn = (y_scr[:, cols] - mean) * inv * g_ref[:, cols] + b_ref[:, cols]
        o_ref[:, cols] = (_silu(yn) * sg_ref[:, cols]).astype(o_ref.dtype)
    _save_history(step, B_HALO, ext)


def _cconv_kernel(z_ref, st_ref, bgs_ref, w_ref, cb_ref, o_ref, ext, *, n_prompt_tiles):
    step = pl.program_id(0)
    width = z_ref.shape[1]
    _load_history(step, n_prompt_tiles, C_HALO, ext, st_ref, z_ref)
    for c in range(width // V7X_LANES):
        cols = slice(c * V7X_LANES, (c + 1) * V7X_LANES)
        y = _causal_taps(ext, w_ref, C_HALO, C_CONV, cols) + cb_ref[:, cols]
        o_ref[:, cols] = (bgs_ref[:, cols] * y).astype(o_ref.dtype)
    _save_history(step, C_HALO, ext)


def _cast_specs(weights, first_step, n_steps):
    specs, shapes = [], []
    for w in weights:
        slab = w.shape[0] // n_steps
        assert w.shape[0] % n_steps == 0 and slab % (2 * V7X_SUBLANES) == 0
        specs.append(pl.BlockSpec((slab, w.shape[1]), lambda s: (jnp.clip(s - first_step, 0, n_steps - 1), 0)))
        shapes.append(jax.ShapeDtypeStruct(w.shape, BF16))
    return specs, shapes


def _with_casts(body, n_inputs, n_casts):
    def kernel_body(*refs):
        sources = refs[n_inputs:n_inputs + n_casts]
        results = refs[n_inputs + n_casts + 1:n_inputs + 2 * n_casts + 1]
        for src, dst in zip(sources, results):
            dst[...] = src[...].astype(dst.dtype)
        body(*refs[:n_inputs], refs[n_inputs + n_casts], *refs[n_inputs + 2 * n_casts + 1:])
    return kernel_body


def _stream_conv_call(body, x, state, gate, gate_col_block, vecs, *, halo, n_prompt, n_sample,
                      scratch, name, casts=()):
    rows, width = x.shape
    n_tiles = n_prompt + n_sample
    assert rows == (n_tiles + 2) * SEQ_TILE
    tile = lambda s: jnp.where(s == 0, n_tiles, jnp.where(s > n_tiles, s, s - 1))
    row_spec = pl.BlockSpec((SEQ_TILE, width), lambda s: (tile(s), 0))
    vec_specs = [pl.BlockSpec(v.shape, lambda s: (0, 0)) for v in vecs]
    cast_specs, cast_shapes = _cast_specs(casts, 1, n_prompt)
    return pl.pallas_call(
        _with_casts(functools.partial(body, n_prompt_tiles=n_prompt), 3 + len(vecs), len(casts)),
        grid=(2 + n_tiles,),
        in_specs=[
            row_spec,
            pl.BlockSpec((None, halo, width), lambda s: (jnp.clip(s - 1 - n_prompt, 0, n_sample - 1), 0, 0)),
            pl.BlockSpec((SEQ_TILE, width), lambda s: (tile(s), gate_col_block)),
        ] + vec_specs + cast_specs,
        out_specs=[row_spec] + cast_specs,
        out_shape=[jax.ShapeDtypeStruct((rows, width), BF16)] + cast_shapes,
        scratch_shapes=[pltpu.VMEM((halo + SEQ_TILE, width), F32)] + scratch,
        compiler_params=_params(("arbitrary",)),
        name=name,
    )(x, state, gate, *vecs, *casts)


def _pad_rows_front(a, rows):
    return jnp.pad(a, ((0, 0), (rows - a.shape[1], 0), (0, 0)))


def kernel(x_prompt, x_sample, cache_a_k, cache_a_v, cache_a_kidx, state_b_conv, state_c_conv, meta,
           e_norm_pre, e_w_in, e_kidx_ln_g, e_kidx_ln_b, e_bconv_w, e_bconv_b, e_bln_g, e_bln_b,
           e_w_out, e_norm_post, o_norm_pre, o_w_in, o_conv_w, o_conv_b, o_w_out, o_norm_post):
    nb, s_len, d = x_prompt.shape
    ndb, nds, _ = x_sample.shape
    past = cache_a_k.shape[2]
    bw = d // 2
    assert nb == 1 and e_w_in.shape[0] == 1 and o_w_in.shape[0] == 1
    assert nds == SEQ_TILE and s_len % Q_BLOCK == 0 and (ndb * nds) % Q_BLOCK == 0
    assert s_len >= B_CONV - 1 and A_WIDTH % bw == 0 and bw % V7X_LANES == 0
    ns = ndb * nds
    meta_row = s_len + ns
    rows = meta_row + Q_BLOCK
    topk_p = min(TOPK_MAX, s_len // 4)
    topk_s = min(TOPK_MAX, (past + nds) // 4)

    sources = (x_prompt[0], x_sample.reshape(ns, d), meta)

    w_in = e_w_in[0].T.astype(BF16)
    widths = (A_WIDTH, A_KV_WIDTH, A_KV_WIDTH, IDX_Q_WIDTH, IDX_DIM, IDX_HEADS, bw, bw, A_WIDTH + bw)
    starts = [sum(widths[:i]) for i in range(len(widths))]
    w_q, w_k, w_v, w_qi, w_ki, w_wi, w_val, w_glu, w_gate = [
        _WeightView(w_in, 0, d, s, max(w, V7X_LANES), transposed=True) for s, w in zip(starts, widths)]

    h0, = _pool_rowwise_call(_prenorm_kernel, *sources, [], [e_norm_pre[0]], [BF16], "even_prenorm")
    q_s, q_i = _matmul_call([h0], [w_q, w_qi], [(0, 0), (0, 1)], _ep_q_qi, [BF16, BF16], bn=512,
                            name="even_q_qidx")
    k32, v32, k16, v16 = _matmul_call([h0], [w_k, w_v], [(0, 0), (0, 1)], _ep_kv,
                                      [F32, F32, BF16, BF16], bn=512, name="even_kv")
    ki32, ki16, wi = _matmul_call([h0], [w_ki, w_wi], [(0, 0), (0, 1)], _ep_kidx, [F32, BF16, F32],
                                  bn=V7X_LANES, auxs=[e_kidx_ln_g[0], e_kidx_ln_b[0]], name="even_kidx")
    u, = _matmul_call([h0], [w_val, w_glu], [(0, 0), (0, 1)], _ep_glu, [F32], bn=512, name="even_glu")
    sg, = _matmul_call([h0], [w_gate], [(0, 0)], _ep_silu, [F32], bn=1024, name="even_gate")

    lpad_p = _round_up(Q_BLOCK + s_len, KEY_BLOCK) + KEY_BLOCK
    seq = lambda a: jnp.concatenate(
        [a[meta_row:meta_row + Q_BLOCK], a[:s_len],
         jnp.zeros((lpad_p - Q_BLOCK - s_len, a.shape[1]), a.dtype)], axis=0)
    oa, w_out, w_out_odd = _prompt_attn_call(
        q_s, q_i, wi, sg, seq(ki16), seq(k16), seq(v16), s_len=s_len,
        meta_block=meta_row // Q_BLOCK, topk=topk_p, casts=(e_w_out[0], o_w_out[0]))
    w_out_a = _WeightView(w_out, 0, A_WIDTH, 0, d)
    w_out_b = _WeightView(w_out, A_WIDTH, bw, 0, d)
    oa = _sample_attn_call(q_s, q_i, wi, sg, cache_a_kidx[0],
                           cache_a_k[0].reshape(ndb, past * A_KV_HEADS, A_HEAD_DIM),
                           cache_a_v[0].reshape(ndb, past * A_KV_HEADS, A_HEAD_DIM), ki16, k16, v16, oa,
                           s_len=s_len, topk=topk_s)

    n_prompt_tiles = s_len // SEQ_TILE
    bconv_w = jnp.pad(e_bconv_w[0], ((0, B_HALO - B_CONV), (0, 0)))
    ob, w_odd = _stream_conv_call(
        _bconv_kernel, u, _pad_rows_front(state_b_conv[0], B_HALO), sg, A_WIDTH // bw,
        [bconv_w, e_bconv_b[0].reshape(1, bw), e_bln_g[0].reshape(1, bw), e_bln_b[0].reshape(1, bw)],
        halo=B_HALO, n_prompt=n_prompt_tiles, n_sample=ndb,
        scratch=[pltpu.VMEM((SEQ_TILE, bw), F32)], name="conformer_conv", casts=(o_w_in[0],))

    y0, = _matmul_call([oa, ob], [w_out_a, w_out_b], [(0, 0), (1, 1)], _ep_sum, [BF16],
                       bn=1024, name="even_out")
    x1, h1 = _pool_rowwise_call(_postnorm_prenorm_kernel, *sources, [y0],
                                [e_norm_post[0], o_norm_pre[0]], [F32, BF16], "even_postnorm_odd_prenorm")

    w_hx, w_cg, w_bg, w_og = [_WeightView(w_odd, 0, d, i * d, d) for i in range(4)]
    z, bgs = _matmul_call([h1], [w_hx, w_cg, w_bg, w_og], [(0, 0), (0, 1), (0, 2), (0, 3)], _ep_odd,
                          [F32, BF16], bn=256, name="odd_in")
    cconv_w = jnp.pad(o_conv_w[0], ((0, C_HALO - C_CONV), (0, 0)))
    o1, = _stream_conv_call(
        _cconv_kernel, z, _pad_rows_front(state_c_conv[0], C_HALO), bgs, 0,
        [cconv_w, o_conv_b[0].reshape(1, d)],
        halo=C_HALO, n_prompt=n_prompt_tiles, n_sample=ndb, scratch=[], name="short_conv")
    y1, = _matmul_call([o1], [_WeightView(w_out_odd, 0, d, 0, d)], [(0, 0)],
                       _ep_identity, [BF16], bn=1024, name="odd_out")
    y_prompt = _final_postnorm_call(x1, y1, o_norm_post[0], 0, s_len // Q_BLOCK, "odd_postnorm_prompt")
    y_sample = _final_postnorm_call(x1, y1, o_norm_post[0], s_len // Q_BLOCK, ns // Q_BLOCK,
                                    "odd_postnorm_sample")

    def prompt_rows(a):
        return jnp.concatenate([a[meta_row:meta_row + N_META], a[:s_len]], axis=0)

    def sample_rows(a):
        return a[s_len:meta_row].reshape(ndb, nds, a.shape[1])

    kv_shape = (A_KV_HEADS, A_HEAD_DIM)
    y_prompt = y_prompt[None]
    y_sample = y_sample.reshape(ndb, nds, d)
    prompt_a_k = prompt_rows(k32).reshape(1, 1, N_META + s_len, *kv_shape)
    prompt_a_v = prompt_rows(v32).reshape(1, 1, N_META + s_len, *kv_shape)
    prompt_a_kidx = prompt_rows(ki32)[None, None]
    prompt_b_conv = u[s_len - (B_CONV - 1):s_len][None, None]
    prompt_c_conv = z[s_len - (C_CONV - 1):s_len][None, None]
    sample_a_k = sample_rows(k32).reshape(1, ndb, nds, *kv_shape)
    sample_a_v = sample_rows(v32).reshape(1, ndb, nds, *kv_shape)
    sample_a_kidx = sample_rows(ki32)[None]
    sample_b_conv = sample_rows(u)[None, :, nds - (B_CONV - 1):]
    sample_c_conv = sample_rows(z)[None, :, nds - (C_CONV - 1):]
    return (y_prompt, y_sample, prompt_a_k, prompt_a_v, prompt_a_kidx, prompt_b_conv, prompt_c_conv,
            sample_a_k, sample_a_v, sample_a_kidx, sample_b_conv, sample_c_conv)
```

```python
import functools
from typing import NamedTuple

import jax
import jax.numpy as jnp
from jax import lax
from jax.experimental import pallas as pl
from jax.experimental.pallas import tpu as pltpu

CHUNK = 64
N_META = 16
EPS = 1e-6
A_HEADS = 16
A_KV_HEADS = 4
A_HEAD_DIM = 128
A_WIDTH = A_HEADS * A_HEAD_DIM
A_KV_WIDTH = A_KV_HEADS * A_HEAD_DIM
A_GROUP = A_HEADS // A_KV_HEADS
IDX_HEADS = 16
IDX_DIM = 128
IDX_Q_WIDTH = IDX_HEADS * IDX_DIM
INDEX_SCALE = (IDX_HEADS * IDX_DIM) ** -0.5
TOPK_MAX = 256
B_CONV = 31
C_CONV = 3

V7X_LANES = 128
V7X_SUBLANES = 8
V7X_VMEM_LIMIT_BYTES = 56 * 1024 * 1024

Q_BLOCK = 128
KEY_BLOCK = 512
COUNT_ROWS = 512
SEARCH_GROUP = 4
SEQ_TILE = 64
B_HALO = 32
C_HALO = 8
INT_MIN = -2 ** 31
MASKED = -1e30
LOG2_E = 1.4426950408889634

F32 = jnp.float32
BF16 = jnp.bfloat16
I32 = jnp.int32


def _round_up(x, m):
    return (x + m - 1) // m * m


def _row_tile(rows, cap):
    best = 16
    for t in range(16, cap + 1, 16):
        if rows % t == 0:
            best = t
    return best


def _params(sem):
    return pltpu.CompilerParams(dimension_semantics=sem, vmem_limit_bytes=V7X_VMEM_LIMIT_BYTES)


def _rms(x, g):
    return x * lax.rsqrt(jnp.mean(x * x, axis=-1, keepdims=True) + EPS) * g


def _gather_pool_block(xp_ref, xs_ref, meta_ref, x_scr, n_prompt, n_sample):
    i = pl.program_id(0)

    @pl.when(i < n_prompt)
    def _():
        x_scr[...] = xp_ref[...]

    @pl.when((i >= n_prompt) & (i < n_prompt + n_sample))
    def _():
        x_scr[...] = xs_ref[...]

    @pl.when(i == n_prompt + n_sample)
    def _():
        x_scr[0:N_META, :] = meta_ref[...]
        x_scr[N_META:, :] = jnp.zeros((Q_BLOCK - N_META, x_scr.shape[1]), F32)


def _prenorm_kernel(xp_ref, xs_ref, meta_ref, g_ref, h_ref, x_scr, *, n_prompt, n_sample):
    _gather_pool_block(xp_ref, xs_ref, meta_ref, x_scr, n_prompt, n_sample)
    h_ref[...] = _rms(x_scr[...], g_ref[...]).astype(h_ref.dtype)


def _postnorm_prenorm_kernel(xp_ref, xs_ref, meta_ref, y_ref, gpost_ref, gpre_ref, xo_ref, h_ref, x_scr,
                             *, n_prompt, n_sample):
    _gather_pool_block(xp_ref, xs_ref, meta_ref, x_scr, n_prompt, n_sample)
    x1 = x_scr[...] + _rms(y_ref[...].astype(F32), gpost_ref[...])
    xo_ref[...] = x1
    h_ref[...] = _rms(x1, gpre_ref[...]).astype(h_ref.dtype)


def _postnorm_kernel(x_ref, y_ref, gpost_ref, xo_ref):
    xo_ref[...] = x_ref[...] + _rms(y_ref[...].astype(F32), gpost_ref[...])


def _pool_rowwise_call(body, x_prompt, x_sample, meta, pool_inputs, vec_inputs, out_dtypes, name):
    s_len, d = x_prompt.shape
    n_prompt, n_sample = s_len // Q_BLOCK, x_sample.shape[0] // Q_BLOCK
    n_blocks = n_prompt + n_sample + 1
    pool_spec = pl.BlockSpec((Q_BLOCK, d), lambda i: (i, 0))
    vec_spec = pl.BlockSpec((1, d), lambda i: (0, 0))
    return pl.pallas_call(
        functools.partial(body, n_prompt=n_prompt, n_sample=n_sample),
        grid=(n_blocks,),
        in_specs=[
            pl.BlockSpec((Q_BLOCK, d), lambda i: (jnp.minimum(i, n_prompt - 1), 0)),
            pl.BlockSpec((Q_BLOCK, d), lambda i: (jnp.clip(i - n_prompt, 0, n_sample - 1), 0)),
            pl.BlockSpec((N_META, d), lambda i: (0, 0)),
        ] + [pool_spec] * len(pool_inputs) + [vec_spec] * len(vec_inputs),
        out_specs=[pool_spec] * len(out_dtypes),
        out_shape=[jax.ShapeDtypeStruct((n_blocks * Q_BLOCK, d), dt) for dt in out_dtypes],
        scratch_shapes=[pltpu.VMEM((Q_BLOCK, d), F32)],
        compiler_params=_params(("arbitrary",)),
        name=name,
    )(x_prompt, x_sample, meta, *pool_inputs, *[v.reshape(1, d) for v in vec_inputs])


def _final_postnorm_call(x1, y1, g, first_block, n_blocks, name):
    d = x1.shape[1]
    in_spec = pl.BlockSpec((Q_BLOCK, d), lambda i: (i + first_block, 0))
    return pl.pallas_call(
        _postnorm_kernel,
        grid=(n_blocks,),
        in_specs=[in_spec, in_spec, pl.BlockSpec((1, d), lambda i: (0, 0))],
        out_specs=pl.BlockSpec((Q_BLOCK, d), lambda i: (i, 0)),
        out_shape=jax.ShapeDtypeStruct((n_blocks * Q_BLOCK, d), F32),
        compiler_params=_params(("parallel",)),
        name=name,
    )(x1, y1, g.reshape(1, d))


def _matmul_kernel(*refs, n_x, n_w, n_aux, terms, epilogue, transposed):
    xs = refs[:n_x]
    ws = refs[n_x:n_x + n_w]
    auxs = refs[n_x + n_w:n_x + n_w + n_aux]
    outs = refs[n_x + n_w + n_aux:]
    accs = [lax.dot_general(xs[a][...], ws[b][...],
                            (((1,), (1 if transposed[b] else 0,)), ((), ())),
                            preferred_element_type=F32) for a, b in terms]
    results = epilogue(*accs, *[r[...] for r in auxs])
    for o_ref, r in zip(outs, results):
        o_ref[...] = r.astype(o_ref.dtype)


class _WeightView(NamedTuple):
    array: jax.Array
    row0: int
    rows: int
    col0: int
    cols: int
    transposed: bool = False


def _weight_spec(w, bn):
    if w.transposed:
        return pl.BlockSpec(
            (pl.Element(bn), pl.Element(w.rows)),
            lambda i, j: (pl.multiple_of(w.col0 + j * bn, 2 * V7X_SUBLANES), w.row0))
    return pl.BlockSpec((w.rows, bn), lambda i, j: (w.row0 // w.rows, w.col0 // bn + j))


def _matmul_call(xs, ws, terms, epilogue, out_dtypes, *, bn, auxs=(), name):
    rows = xs[0].shape[0]
    n = ws[0].cols
    bm = _row_tile(rows, 1536)
    bn = max(t for t in range(V7X_LANES, min(bn, n) + 1, V7X_LANES)
             if n % t == 0 and all(w.transposed or w.col0 % t == 0 for w in ws))
    assert all(w.cols == n and w.row0 % w.rows == 0 for w in ws)
    assert all(w.col0 % (2 * V7X_SUBLANES) == 0 for w in ws if w.transposed)
    in_specs = [pl.BlockSpec((bm, x.shape[1]), lambda i, j: (i, 0)) for x in xs]
    in_specs += [_weight_spec(w, bn) for w in ws]
    in_specs += [pl.BlockSpec((1, bn), lambda i, j: (0, j)) for _ in auxs]
    out_spec = pl.BlockSpec((bm, bn), lambda i, j: (i, j))
    body = functools.partial(_matmul_kernel, n_x=len(xs), n_w=len(ws), n_aux=len(auxs),
                             terms=terms, epilogue=epilogue,
                             transposed=tuple(w.transposed for w in ws))
    return pl.pallas_call(
        body,
        grid=(rows // bm, n // bn),
        in_specs=in_specs,
        out_specs=[out_spec] * len(out_dtypes),
        out_shape=[jax.ShapeDtypeStruct((rows, n), dt) for dt in out_dtypes],
        compiler_params=_params(("parallel", "arbitrary")),
        name=name,
    )(*xs, *[w.array for w in ws], *[a.reshape(1, n) for a in auxs])


def _silu(x):
    return x * jax.nn.sigmoid(x)


def _ep_q_qi(q, qi):
    return q * (A_HEAD_DIM ** -0.5 * LOG2_E), qi


def _ep_kv(k, v):
    return k, v, k, v


def _ep_kidx(ki, wi, g, b):
    kc = ki - jnp.mean(ki, axis=-1, keepdims=True)
    kn = kc * lax.rsqrt(jnp.mean(kc * kc, axis=-1, keepdims=True) + EPS) * g + b
    return kn, kn, wi * INDEX_SCALE


def _ep_glu(val, glu):
    return (val * jax.nn.sigmoid(glu),)


def _ep_silu(gate):
    return (_silu(gate),)


def _ep_sum(a, b):
    return (a + b,)


def _ep_identity(a):
    return (a,)


def _ep_odd(hx, cg, bg, gate):
    return cg * hx, bg * _silu(gate)


def _prompt_attn_kernel(qs_ref, qi_ref, wi_ref, sg_ref, ki_ref, k_ref, v_ref, o_ref, *scratch,
                        topk, n_prompt_blocks):
    step = pl.program_id(0)

    @pl.when(step <= n_prompt_blocks)
    def _():
        frame = (step - 1) * Q_BLOCK + lax.broadcasted_iota(I32, (1, Q_BLOCK), 1)
        limit = jnp.where(step == 0, 0, Q_BLOCK + CHUNK * (frame // CHUNK + 1))
        n_kt = jnp.where(step == 0, 1, pl.cdiv(Q_BLOCK * (step + 1), KEY_BLOCK))
        _attend(qs_ref, qi_ref, wi_ref, sg_ref, ki_ref, k_ref, v_ref, *scratch,
                topk=topk, n_lo=N_META, limit=limit, n_kt=n_kt)
        o_ref[...] = scratch[-1][...].astype(o_ref.dtype)

    @pl.when(step > n_prompt_blocks)
    def _():
        o_ref[...] = jnp.zeros(o_ref.shape, o_ref.dtype)


def _sample_attn_kernel(qs_ref, qi_ref, wi_ref, sg_ref, cki_ref, ck_ref, cv_ref, nki_ref, nk_ref, nv_ref,
                        pool_ref, o_ref, ki_buf, k_buf, v_buf, *scratch, topk):
    del pool_ref
    step = pl.program_id(0)
    past, new = cki_ref.shape[0], nk_ref.shape[0]
    for buf, cache_ref, new_ref in ((ki_buf, cki_ref, nki_ref), (k_buf, ck_ref, nk_ref), (v_buf, cv_ref, nv_ref)):
        heads = cache_ref.shape[0] // past
        for r in range(0, past, KEY_BLOCK):
            n = min(KEY_BLOCK, past - r)
            for g in range(heads):
                buf[r:r + n, g * A_HEAD_DIM:(g + 1) * A_HEAD_DIM] = cache_ref[
                    pl.ds(r * heads + g, n, stride=heads), :].astype(BF16)
        buf[past:past + new, :] = new_ref[...]
        buf[past + new:, :] = jnp.zeros((buf.shape[0] - past - new, buf.shape[1]), BF16)
    q0 = pl.multiple_of((step % (Q_BLOCK // new)) * new, new)
    _attend(qs_ref, qi_ref, wi_ref, sg_ref, ki_buf, k_buf, v_buf, *scratch,
            topk=topk, n_lo=past + new, limit=jnp.zeros((1, Q_BLOCK), I32),
            n_kt=ki_buf.shape[0] // KEY_BLOCK - 1, q0=q0, hq=new)
    o_ref[...] = scratch[-1][...].astype(o_ref.dtype)


def _attend(qs_ref, qi_ref, wi_ref, sg_ref, ki_ref, k_ref, v_ref,
            qrows, qirows, keys, bias, s_a, s_b, m_scr, l_scr, acc_scr, o_scr,
            *, topk, n_lo, limit, n_kt, q0=0, hq=Q_BLOCK):
    nq = Q_BLOCK
    lb = KEY_BLOCK
    fold = nq // hq
    assert fold in (1, 2)

    @pl.when(pl.program_id(0) == 0)
    def _():
        one_hot = (lax.broadcasted_iota(I32, (hq, nq), 0) == lax.broadcasted_iota(I32, (hq, nq), 1))
        for h in range(A_HEADS):
            qrows[h * hq:(h + 1) * hq, A_HEAD_DIM:A_HEAD_DIM + nq] = one_hot.astype(BF16)

    for h in range(A_HEADS):
        qrows[h * hq:(h + 1) * hq, 0:A_HEAD_DIM] = qs_ref[pl.ds(q0, hq), h * A_HEAD_DIM:(h + 1) * A_HEAD_DIM]
    for h in range(IDX_HEADS):
        qirows[h * hq:(h + 1) * hq, :] = qi_ref[pl.ds(q0, hq), h * IDX_DIM:(h + 1) * IDX_DIM]
    w_t = jnp.concatenate([wi_ref[pl.ds(q0, hq), :]] * fold, axis=0).T
    lane_head = lax.broadcasted_iota(I32, (1, nq), 1) // hq
    w_tiles = []
    for j in range(IDX_HEADS // fold):
        w_tile = w_t[j * fold:j * fold + 1, :]
        for f in range(1, fold):
            w_tile = jnp.where(lane_head == f, w_t[j * fold + f:j * fold + f + 1, :], w_tile)
        w_tiles.append(w_tile)

    ib = 2 * lb

    def score_block(kt, carry):
        r0 = pl.multiple_of(kt * ib, ib)
        d = lax.dot_general(ki_ref[pl.ds(r0, ib), :], qirows[...], (((1,), (1,)), ((), ())),
                            preferred_element_type=F32)
        sc = jnp.zeros((ib, nq), F32)
        for j, w_tile in enumerate(w_tiles):
            sc = sc + jnp.maximum(d[:, j * nq:(j + 1) * nq], 0.0) * w_tile
        for f in range(1, fold):
            sc = sc + pltpu.roll(sc, f * hq, axis=1)
        pos = r0 + lax.broadcasted_iota(I32, (ib, nq), 0)
        adm = (pos < n_lo) | ((pos >= Q_BLOCK) & (pos < limit))
        bits = lax.bitcast_convert_type(sc, I32)
        key = bits ^ ((bits >> 31) & 0x7FFFFFFF)
        keys[pl.ds(r0, ib), :] = jnp.where(adm, key, INT_MIN)
        return carry

    lax.fori_loop(0, (n_kt + 1) // 2, score_block, 0)

    n_ct = n_kt * (lb // COUNT_ROWS)

    def count_where(pred):
        def body(c, acc):
            r0 = pl.multiple_of(c * COUNT_ROWS, COUNT_ROWS)
            pos = r0 + lax.broadcasted_iota(I32, (COUNT_ROWS, nq), 0)
            hit = jnp.where(pred(keys[pl.ds(r0, COUNT_ROWS), :], pos), 1.0, 0.0)
            parts = [hit[i * V7X_SUBLANES:(i + 1) * V7X_SUBLANES, :]
                     for i in range(COUNT_ROWS // V7X_SUBLANES)]
            while len(parts) > 1:
                parts = [a + b for a, b in zip(parts[::2], parts[1::2])]
            return acc + parts[0]
        acc = lax.fori_loop(0, n_ct, body, jnp.zeros((V7X_SUBLANES, nq), F32))
        return jnp.sum(acc, axis=0, keepdims=True)

    def search_bit(i, state):
        prefix, n_sel = state
        trial = prefix | lax.shift_left(jnp.int32(1), 31 - i)
        cand = trial ^ INT_MIN
        n_trial = count_where(lambda k, pos: k >= cand)
        take = n_trial >= float(topk)
        return jnp.where(take, trial, prefix), jnp.where(take, n_trial, n_sel)

    def search_bits(state):
        group, prefix, n_sel = state
        prefix, n_sel = lax.fori_loop(
            group * SEARCH_GROUP, (group + 1) * SEARCH_GROUP, search_bit, (prefix, n_sel))
        return group + 1, prefix, n_sel

    def search_open(state):
        group, _, n_sel = state
        return (group < 32 // SEARCH_GROUP) & (jnp.max(n_sel) > float(topk))

    n_admissible = (n_lo + jnp.maximum(limit - Q_BLOCK, 0)).astype(F32)
    _, prefix, n_sel = lax.while_loop(search_open, search_bits,
                                      (jnp.int32(0), jnp.zeros((1, nq), I32), n_admissible))
    thr = jnp.maximum(prefix ^ INT_MIN, INT_MIN + 1)

    def write_bias(selected):
        def body(c, carry):
            r0 = pl.multiple_of(c * COUNT_ROWS, COUNT_ROWS)
            pos = r0 + lax.broadcasted_iota(I32, (COUNT_ROWS, nq), 0)
            bias[pl.ds(r0, COUNT_ROWS), :] = jnp.where(
                selected(keys[pl.ds(r0, COUNT_ROWS), :], pos), 0.0, MASKED).astype(BF16)
            return carry
        lax.fori_loop(0, n_ct, body, 0)

    surplus = jnp.max(n_sel) > float(topk)

    @pl.when(jnp.logical_not(surplus))
    def _():
        write_bias(lambda k, pos: k >= thr)

    @pl.when(surplus)
    def _():
        keep = float(topk) - count_where(lambda k, pos: k > thr)
        position_bits = keys.shape[0].bit_length()

        def cut_bit(i, prefix):
            trial = prefix | lax.shift_left(jnp.int32(1), position_bits - 1 - i)
            admitted = count_where(lambda k, pos: (k == thr) & (pos < trial))
            return jnp.where(admitted < keep, trial, prefix)
        cut = lax.fori_loop(0, position_bits, cut_bit, jnp.zeros((1, nq), I32)) + 1
        write_bias(lambda k, pos: (k > thr) | ((k == thr) & (pos < cut)))

    gw = A_GROUP * hq
    m_scr[...] = jnp.full(m_scr.shape, MASKED, F32)
    l_scr[...] = jnp.zeros(l_scr.shape, F32)
    acc_scr[...] = jnp.zeros(acc_scr.shape, F32)

    bias[pl.ds(pl.multiple_of(n_kt * lb, lb), lb), :] = jnp.full((lb, nq), MASKED, BF16)

    def score_keys(kt, s_ref, g):
        r0 = pl.multiple_of(kt * lb, lb)
        k_aug = jnp.concatenate(
            [k_ref[pl.ds(r0, lb), g * A_HEAD_DIM:(g + 1) * A_HEAD_DIM], bias[pl.ds(r0, lb), :]], axis=1)
        s_ref[g] = lax.dot_general(k_aug, qrows[g * gw:(g + 1) * gw, :], (((1,), (1,)), ((), ())),
                                   preferred_element_type=F32)

    def softmax_pv(kt, s_ref, g):
        s = s_ref[g]
        m_old = m_scr[g]
        m_new = jnp.maximum(m_old, jnp.max(s, axis=0, keepdims=True))
        alpha = jnp.exp2(m_old - m_new)
        p = jnp.exp2(s - m_new)
        l_scr[g] = l_scr[g] * alpha + jnp.sum(p, axis=0, keepdims=True)
        v_blk = v_ref[pl.ds(pl.multiple_of(kt * lb, lb), lb), g * A_HEAD_DIM:(g + 1) * A_HEAD_DIM]
        acc_scr[g] = acc_scr[g] * alpha + lax.dot_general(
            v_blk, p.astype(BF16), (((0,), (0,)), ((), ())), preferred_element_type=F32)
        m_scr[g] = m_new

    for g in range(A_KV_HEADS):
        score_keys(0, s_a, g)

    def attend_pair(j, carry):
        kt = 2 * j
        for g in range(A_KV_HEADS):
            score_keys(kt + 1, s_b, g)
        for g in range(A_KV_HEADS):
            softmax_pv(kt, s_a, g)
        for g in range(A_KV_HEADS):
            score_keys(jnp.minimum(kt + 2, n_kt), s_a, g)
        for g in range(A_KV_HEADS):
            softmax_pv(kt + 1, s_b, g)
        return carry

    lax.fori_loop(0, (n_kt + 1) // 2, attend_pair, 0)
    for g in range(A_KV_HEADS):
        o_g = (acc_scr[g] / l_scr[g]).T
        for r in range(A_GROUP):
            h = g * A_GROUP + r
            cols = slice(h * A_HEAD_DIM, (h + 1) * A_HEAD_DIM)
            o_scr[:, cols] = o_g[r * hq:(r + 1) * hq, :] * sg_ref[pl.ds(q0, hq), cols]


def _attn_scratch(lpad, hq=Q_BLOCK):
    gw = A_GROUP * hq
    return [
        pltpu.VMEM((A_HEADS * hq, A_HEAD_DIM + Q_BLOCK), BF16),
        pltpu.VMEM((IDX_HEADS * hq, IDX_DIM), BF16),
        pltpu.VMEM((lpad, Q_BLOCK), I32),
        pltpu.VMEM((lpad, Q_BLOCK), BF16),
        pltpu.VMEM((A_KV_HEADS, KEY_BLOCK, gw), F32),
        pltpu.VMEM((A_KV_HEADS, KEY_BLOCK, gw), F32),
        pltpu.VMEM((A_KV_HEADS, 1, gw), F32),
        pltpu.VMEM((A_KV_HEADS, 1, gw), F32),
        pltpu.VMEM((A_KV_HEADS, A_HEAD_DIM, gw), F32),
        pltpu.VMEM((hq, A_WIDTH), F32),
    ]


def _query_specs(q_index):
    qspec = pl.BlockSpec((Q_BLOCK, A_WIDTH), lambda s: (q_index(s), 0))
    return [qspec, qspec, pl.BlockSpec((Q_BLOCK, V7X_LANES), lambda s: (q_index(s), 0)), qspec]


def _prompt_attn_call(qs, qi, wi, sg, ki_seq, k_seq, v_seq, *, s_len, meta_block, topk, casts=()):
    lpad = ki_seq.shape[0]
    n_qb = s_len // Q_BLOCK
    full = lambda a: pl.BlockSpec(a.shape, lambda s: (0, 0))
    cast_specs, cast_shapes = _cast_specs(casts, 1, n_qb)
    body = functools.partial(_prompt_attn_kernel, topk=topk, n_prompt_blocks=n_qb)
    return pl.pallas_call(
        _with_casts(body, 7, len(casts)),
        grid=(1 + meta_block,),
        in_specs=_query_specs(lambda s: jnp.where(s == 0, meta_block, s - 1))
        + [full(ki_seq), full(k_seq), full(v_seq)] + cast_specs,
        out_specs=[pl.BlockSpec((Q_BLOCK, A_WIDTH), lambda s: (jnp.where(s == 0, meta_block, s - 1), 0))]
        + cast_specs,
        out_shape=[jax.ShapeDtypeStruct(((meta_block + 1) * Q_BLOCK, A_WIDTH), BF16)] + cast_shapes,
        scratch_shapes=_attn_scratch(lpad),
        compiler_params=_params(("arbitrary",)),
        name="prompt_attention",
    )(qs, qi, wi, sg, ki_seq, k_seq, v_seq, *casts)


def _sample_attn_call(qs, qi, wi, sg, cache_ki, cache_k, cache_v, ki16, k16, v16, pool_out,
                      *, s_len, topk):
    ndb, past, _ = cache_ki.shape
    new = SEQ_TILE
    lpad = _round_up(past + new, KEY_BLOCK) + KEY_BLOCK
    cache_spec = lambda a: pl.BlockSpec((None,) + a.shape[1:], lambda s: (s, 0, 0))
    new_spec = lambda a: pl.BlockSpec((new, a.shape[1]), lambda s: (s_len // new + s, 0))
    return pl.pallas_call(
        functools.partial(_sample_attn_kernel, topk=topk),
        grid=(ndb,),
        in_specs=_query_specs(lambda s: s_len // Q_BLOCK + s // (Q_BLOCK // new))
        + [cache_spec(cache_ki), cache_spec(cache_k), cache_spec(cache_v),
           new_spec(ki16), new_spec(k16), new_spec(v16), pl.BlockSpec(memory_space=pl.ANY)],
        out_specs=pl.BlockSpec((new, A_WIDTH), lambda s: (s_len // new + s, 0)),
        out_shape=jax.ShapeDtypeStruct(pool_out.shape, BF16),
        input_output_aliases={10: 0},
        scratch_shapes=[pltpu.VMEM((lpad, IDX_DIM), BF16), pltpu.VMEM((lpad, A_KV_WIDTH), BF16),
                        pltpu.VMEM((lpad, A_KV_WIDTH), BF16)] + _attn_scratch(lpad, new),
        compiler_params=_params(("arbitrary",)),
        name="sample_attention",
    )(qs, qi, wi, sg, cache_ki, cache_k, cache_v, ki16, k16, v16, pool_out)


def _load_history(step, n_prompt_tiles, halo, ext, state_ref, cur_ref):
    @pl.when(step == 0)
    def _():
        ext[0:halo, :] = jnp.zeros((halo, ext.shape[1]), F32)

    @pl.when(step > n_prompt_tiles)
    def _():
        ext[0:halo, :] = state_ref[...]

    ext[halo:halo + SEQ_TILE, :] = cur_ref[...]


def _save_history(step, halo, ext):
    @pl.when(step == 0)
    def _():
        if halo > N_META:
            ext[0:halo - N_META, :] = jnp.zeros((halo - N_META, ext.shape[1]), F32)
        keep = min(halo, N_META)
        ext[halo - keep:halo, :] = ext[halo + N_META - keep:halo + N_META, :]

    @pl.when(step > 0)
    def _():
        ext[0:halo, :] = ext[SEQ_TILE:SEQ_TILE + halo, :]


def _causal_taps(ext, w_ref, halo, width, cols):
    first = halo - (width - 1)
    x = ext[:, cols]
    rows = x.shape[0]
    rotated = {0: x}
    for j in range(width):
        r = (first + j) % V7X_SUBLANES
        if r not in rotated:
            rotated[r] = pltpu.roll(x, rows - r, axis=0)
    acc = None
    for j in range(width):
        r = (first + j) % V7X_SUBLANES
        base = first + j - r
        term = rotated[r][base:base + SEQ_TILE, :] * w_ref[j:j + 1, cols]
        acc = term if acc is None else acc + term
    return acc


def _bconv_kernel(u_ref, st_ref, sg_ref, w_ref, cb_ref, g_ref, b_ref, o_ref, ext, y_scr,
                  *, n_prompt_tiles):
    step = pl.program_id(0)
    width = u_ref.shape[1]
    _load_history(step, n_prompt_tiles, B_HALO, ext, st_ref, u_ref)
    total = jnp.zeros((SEQ_TILE, V7X_LANES), F32)
    for c in range(width // V7X_LANES):
        cols = slice(c * V7X_LANES, (c + 1) * V7X_LANES)
        y = _causal_taps(ext, w_ref, B_HALO, B_CONV, cols) + cb_ref[:, cols]
        y_scr[:, cols] = y
        total = total + y
    mean = jnp.sum(total, axis=-1, keepdims=True) / width
    sq = jnp.zeros((SEQ_TILE, V7X_LANES), F32)
    for c in range(width // V7X_LANES):
        cols = slice(c * V7X_LANES, (c + 1) * V7X_LANES)
        yc = y_scr[:, cols] - mean
        sq = sq + yc * yc
    inv = lax.rsqrt(jnp.sum(sq, axis=-1, keepdims=True) / width + EPS)
    for c in range(width // V7X_LANES):
        cols = slice(c * V7X_LANES, (c + 1) * V7X_LANES)
        yn = (y_scr[:, cols] - mean) * inv * g_ref[:, cols] + b_ref[:, cols]
        o_ref[:, cols] = (_silu(yn) * sg_ref[:, cols]).astype(o_ref.dtype)
    _save_history(step, B_HALO, ext)


def _cconv_kernel(z_ref, st_ref, bgs_ref, w_ref, cb_ref, o_ref, ext, *, n_prompt_tiles):
    step = pl.program_id(0)
    width = z_ref.shape[1]
    _load_history(step, n_prompt_tiles, C_HALO, ext, st_ref, z_ref)
    for c in range(width // V7X_LANES):
        cols = slice(c * V7X_LANES, (c + 1) * V7X_LANES)
        y = _causal_taps(ext, w_ref, C_HALO, C_CONV, cols) + cb_ref[:, cols]
        o_ref[:, cols] = (bgs_ref[:, cols] * y).astype(o_ref.dtype)
    _save_history(step, C_HALO, ext)


def _cast_specs(weights, first_step, n_steps):
    specs, shapes = [], []
    for w in weights:
        slab = w.shape[0] // n_steps
        assert w.shape[0] % n_steps == 0 and slab % (2 * V7X_SUBLANES) == 0
        specs.append(pl.BlockSpec((slab, w.shape[1]), lambda s: (jnp.clip(s - first_step, 0, n_steps - 1), 0)))
        shapes.append(jax.ShapeDtypeStruct(w.shape, BF16))
    return specs, shapes


def _with_casts(body, n_inputs, n_casts):
    def kernel_body(*refs):
        sources = refs[n_inputs:n_inputs + n_casts]
        results = refs[n_inputs + n_casts + 1:n_inputs + 2 * n_casts + 1]
        for src, dst in zip(sources, results):
            dst[...] = src[...].astype(dst.dtype)
        body(*refs[:n_inputs], refs[n_inputs + n_casts], *refs[n_inputs + 2 * n_casts + 1:])
    return kernel_body


def _stream_conv_call(body, x, state, gate, gate_col_block, vecs, *, halo, n_prompt, n_sample,
                      scratch, name, casts=()):
    rows, width = x.shape
    n_tiles = n_prompt + n_sample
    assert rows == (n_tiles + 2) * SEQ_TILE
    tile = lambda s: jnp.where(s == 0, n_tiles, jnp.where(s > n_tiles, s, s - 1))
    row_spec = pl.BlockSpec((SEQ_TILE, width), lambda s: (tile(s), 0))
    vec_specs = [pl.BlockSpec(v.shape, lambda s: (0, 0)) for v in vecs]
    cast_specs, cast_shapes = _cast_specs(casts, 1, n_prompt)
    return pl.pallas_call(
        _with_casts(functools.partial(body, n_prompt_tiles=n_prompt), 3 + len(vecs), len(casts)),
        grid=(2 + n_tiles,),
        in_specs=[
            row_spec,
            pl.BlockSpec((None, halo, width), lambda s: (jnp.clip(s - 1 - n_prompt, 0, n_sample - 1), 0, 0)),
            pl.BlockSpec((SEQ_TILE, width), lambda s: (tile(s), gate_col_block)),
        ] + vec_specs + cast_specs,
        out_specs=[row_spec] + cast_specs,
        out_shape=[jax.ShapeDtypeStruct((rows, width), BF16)] + cast_shapes,
        scratch_shapes=[pltpu.VMEM((halo + SEQ_TILE, width), F32)] + scratch,
        compiler_params=_params(("arbitrary",)),
        name=name,
    )(x, state, gate, *vecs, *casts)


def _pad_rows_front(a, rows):
    return jnp.pad(a, ((0, 0), (rows - a.shape[1], 0), (0, 0)))


def kernel(x_prompt, x_sample, cache_a_k, cache_a_v, cache_a_kidx, state_b_conv, state_c_conv, meta,
           e_norm_pre, e_w_in, e_kidx_ln_g, e_kidx_ln_b, e_bconv_w, e_bconv_b, e_bln_g, e_bln_b,
           e_w_out, e_norm_post, o_norm_pre, o_w_in, o_conv_w, o_conv_b, o_w_out, o_norm_post):
    nb, s_len, d = x_prompt.shape
    ndb, nds, _ = x_sample.shape
    past = cache_a_k.shape[2]
    bw = d // 2
    assert nb == 1 and e_w_in.shape[0] == 1 and o_w_in.shape[0] == 1
    assert nds == SEQ_TILE and s_len % Q_BLOCK == 0 and (ndb * nds) % Q_BLOCK == 0
    assert s_len >= B_CONV - 1 and A_WIDTH % bw == 0 and bw % V7X_LANES == 0
    ns = ndb * nds
    meta_row = s_len + ns
    rows = meta_row + Q_BLOCK
    topk_p = min(TOPK_MAX, s_len // 4)
    topk_s = min(TOPK_MAX, (past + nds) // 4)

    sources = (x_prompt[0], x_sample.reshape(ns, d), meta)

    w_in = e_w_in[0].T.astype(BF16)
    widths = (A_WIDTH, A_KV_WIDTH, A_KV_WIDTH, IDX_Q_WIDTH, IDX_DIM, IDX_HEADS, bw, bw, A_WIDTH + bw)
    starts = [sum(widths[:i]) for i in range(len(widths))]
    w_q, w_k, w_v, w_qi, w_ki, w_wi, w_val, w_glu, w_gate = [
        _WeightView(w_in, 0, d, s, max(w, V7X_LANES), transposed=True) for s, w in zip(starts, widths)]

    h0, = _pool_rowwise_call(_prenorm_kernel, *sources, [], [e_norm_pre[0]], [BF16], "even_prenorm")
    q_s, q_i = _matmul_call([h0], [w_q, w_qi], [(0, 0), (0, 1)], _ep_q_qi, [BF16, BF16], bn=512,
                            name="even_q_qidx")
    k32, v32, k16, v16 = _matmul_call([h0], [w_k, w_v], [(0, 0), (0, 1)], _ep_kv,
                                      [F32, F32, BF16, BF16], bn=512, name="even_kv")
    ki32, ki16, wi = _matmul_call([h0], [w_ki, w_wi], [(0, 0), (0, 1)], _ep_kidx, [F32, BF16, F32],
                                  bn=V7X_LANES, auxs=[e_kidx_ln_g[0], e_kidx_ln_b[0]], name="even_kidx")
    u, = _matmul_call([h0], [w_val, w_glu], [(0, 0), (0, 1)], _ep_glu, [F32], bn=512, name="even_glu")
    sg, = _matmul_call([h0], [w_gate], [(0, 0)], _ep_silu, [F32], bn=1024, name="even_gate")

    lpad_p = _round_up(Q_BLOCK + s_len, KEY_BLOCK) + KEY_BLOCK
    seq = lambda a: jnp.concatenate(
        [a[meta_row:meta_row + Q_BLOCK], a[:s_len],
         jnp.zeros((lpad_p - Q_BLOCK - s_len, a.shape[1]), a.dtype)], axis=0)
    oa, w_out, w_out_odd = _prompt_attn_call(
        q_s, q_i, wi, sg, seq(ki16), seq(k16), seq(v16), s_len=s_len,
        meta_block=meta_row // Q_BLOCK, topk=topk_p, casts=(e_w_out[0], o_w_out[0]))
    w_out_a = _WeightView(w_out, 0, A_WIDTH, 0, d)
    w_out_b = _WeightView(w_out, A_WIDTH, bw, 0, d)
    oa = _sample_attn_call(q_s, q_i, wi, sg, cache_a_kidx[0],
                           cache_a_k[0].reshape(ndb, past * A_KV_HEADS, A_HEAD_DIM),
                           cache_a_v[0].reshape(ndb, past * A_KV_HEADS, A_HEAD_DIM), ki16, k16, v16, oa,
                           s_len=s_len, topk=topk_s)

    n_prompt_tiles = s_len // SEQ_TILE
    bconv_w = jnp.pad(e_bconv_w[0], ((0, B_HALO - B_CONV), (0, 0)))
    ob, w_odd = _stream_conv_call(
        _bconv_kernel, u, _pad_rows_front(state_b_conv[0], B_HALO), sg, A_WIDTH // bw,
        [bconv_w, e_bconv_b[0].reshape(1, bw), e_bln_g[0].reshape(1, bw), e_bln_b[0].reshape(1, bw)],
        halo=B_HALO, n_prompt=n_prompt_tiles, n_sample=ndb,
        scratch=[pltpu.VMEM((SEQ_TILE, bw), F32)], name="conformer_conv", casts=(o_w_in[0],))

    y0, = _matmul_call([oa, ob], [w_out_a, w_out_b], [(0, 0), (1, 1)], _ep_sum, [BF16],
                       bn=1024, name="even_out")
    x1, h1 = _pool_rowwise_call(_postnorm_prenorm_kernel, *sources, [y0],
                                [e_norm_post[0], o_norm_pre[0]], [F32, BF16], "even_postnorm_odd_prenorm")

    w_hx, w_cg, w_bg, w_og = [_WeightView(w_odd, 0, d, i * d, d) for i in range(4)]
    z, bgs = _matmul_call([h1], [w_hx, w_cg, w_bg, w_og], [(0, 0), (0, 1), (0, 2), (0, 3)], _ep_odd,
                          [F32, BF16], bn=256, name="odd_in")
    cconv_w = jnp.pad(o_conv_w[0], ((0, C_HALO - C_CONV), (0, 0)))
    o1, = _stream_conv_call(
        _cconv_kernel, z, _pad_rows_front(state_c_conv[0], C_HALO), bgs, 0,
        [cconv_w, o_conv_b[0].reshape(1, d)],
        halo=C_HALO, n_prompt=n_prompt_tiles, n_sample=ndb, scratch=[], name="short_conv")
    y1, = _matmul_call([o1], [_WeightView(w_out_odd, 0, d, 0, d)], [(0, 0)],
                       _ep_identity, [BF16], bn=1024, name="odd_out")
    y_prompt = _final_postnorm_call(x1, y1, o_norm_post[0], 0, s_len // Q_BLOCK, "odd_postnorm_prompt")
    y_sample = _final_postnorm_call(x1, y1, o_norm_post[0], s_len // Q_BLOCK, ns // Q_BLOCK,
                                    "odd_postnorm_sample")

    def prompt_rows(a):
        return jnp.concatenate([a[meta_row:meta_row + N_META], a[:s_len]], axis=0)

    def sample_rows(a):
        return a[s_len:meta_row].reshape(ndb, nds, a.shape[1])

    kv_shape = (A_KV_HEADS, A_HEAD_DIM)
    y_prompt = y_prompt[None]
    y_sample = y_sample.reshape(ndb, nds, d)
    prompt_a_k = prompt_rows(k32).reshape(1, 1, N_META + s_len, *kv_shape)
    prompt_a_v = prompt_rows(v32).reshape(1, 1, N_META + s_len, *kv_shape)
    prompt_a_kidx = prompt_rows(ki32)[None, None]
    prompt_b_conv = u[s_len - (B_CONV - 1):s_len][None, None]
    prompt_c_conv = z[s_len - (C_CONV - 1):s_len][None, None]
    sample_a_k = sample_rows(k32).reshape(1, ndb, nds, *kv_shape)
    sample_a_v = sample_rows(v32).reshape(1, ndb, nds, *kv_shape)
    sample_a_kidx = sample_rows(ki32)[None]
    sample_b_conv = sample_rows(u)[None, :, nds - (B_CONV - 1):]
    sample_c_conv = sample_rows(z)[None, :, nds - (C_CONV - 1):]
    return (y_prompt, y_sample, prompt_a_k, prompt_a_v, prompt_a_kidx, prompt_b_conv, prompt_c_conv,
            sample_a_k, sample_a_v, sample_a_kidx, sample_b_conv, sample_c_conv)
```

```python
import functools
from typing import NamedTuple

import jax
import jax.numpy as jnp
from jax import lax
from jax.experimental import pallas as pl
from jax.experimental.pallas import tpu as pltpu

CHUNK = 64
N_META = 16
EPS = 1e-6
A_HEADS = 16
A_KV_HEADS = 4
A_HEAD_DIM = 128
A_WIDTH = A_HEADS * A_HEAD_DIM
A_KV_WIDTH = A_KV_HEADS * A_HEAD_DIM
A_GROUP = A_HEADS // A_KV_HEADS
IDX_HEADS = 16
IDX_DIM = 128
IDX_Q_WIDTH = IDX_HEADS * IDX_DIM
INDEX_SCALE = (IDX_HEADS * IDX_DIM) ** -0.5
TOPK_MAX = 256
B_CONV = 31
C_CONV = 3

V7X_LANES = 128
V7X_SUBLANES = 8
V7X_VMEM_LIMIT_BYTES = 56 * 1024 * 1024

Q_BLOCK = 128
KEY_BLOCK = 512
COUNT_ROWS = 512
SEARCH_GROUP = 4
SEQ_TILE = 64
B_HALO = 32
C_HALO = 8
INT_MIN = -2 ** 31
MASKED = -1e30
LOG2_E = 1.4426950408889634

F32 = jnp.float32
BF16 = jnp.bfloat16
I32 = jnp.int32


def _round_up(x, m):
    return (x + m - 1) // m * m


def _row_tile(rows, cap):
    best = 16
    for t in range(16, cap + 1, 16):
        if rows % t == 0:
            best = t
    return best


def _params(sem):
    return pltpu.CompilerParams(dimension_semantics=sem, vmem_limit_bytes=V7X_VMEM_LIMIT_BYTES)


def _rms(x, g):
    return x * lax.rsqrt(jnp.mean(x * x, axis=-1, keepdims=True) + EPS) * g


def _gather_pool_block(xp_ref, xs_ref, meta_ref, x_scr, n_prompt, n_sample):
    i = pl.program_id(0)

    @pl.when(i < n_prompt)
    def _():
        x_scr[...] = xp_ref[...]

    @pl.when((i >= n_prompt) & (i < n_prompt + n_sample))
    def _():
        x_scr[...] = xs_ref[...]

    @pl.when(i == n_prompt + n_sample)
    def _():
        x_scr[0:N_META, :] = meta_ref[...]
        x_scr[N_META:, :] = jnp.zeros((Q_BLOCK - N_META, x_scr.shape[1]), F32)


def _prenorm_kernel(xp_ref, xs_ref, meta_ref, g_ref, h_ref, x_scr, *, n_prompt, n_sample):
    _gather_pool_block(xp_ref, xs_ref, meta_ref, x_scr, n_prompt, n_sample)
    h_ref[...] = _rms(x_scr[...], g_ref[...]).astype(h_ref.dtype)


def _postnorm_prenorm_kernel(xp_ref, xs_ref, meta_ref, y_ref, gpost_ref, gpre_ref, xo_ref, h_ref, x_scr,
                             *, n_prompt, n_sample):
    _gather_pool_block(xp_ref, xs_ref, meta_ref, x_scr, n_prompt, n_sample)
    x1 = x_scr[...] + _rms(y_ref[...].astype(F32), gpost_ref[...])
    xo_ref[...] = x1
    h_ref[...] = _rms(x1, gpre_ref[...]).astype(h_ref.dtype)


def _postnorm_kernel(x_ref, y_ref, gpost_ref, xo_ref):
    xo_ref[...] = x_ref[...] + _rms(y_ref[...].astype(F32), gpost_ref[...])


def _pool_rowwise_call(body, x_prompt, x_sample, meta, pool_inputs, vec_inputs, out_dtypes, name):
    s_len, d = x_prompt.shape
    n_prompt, n_sample = s_len // Q_BLOCK, x_sample.shape[0] // Q_BLOCK
    n_blocks = n_prompt + n_sample + 1
    pool_spec = pl.BlockSpec((Q_BLOCK, d), lambda i: (i, 0))
    vec_spec = pl.BlockSpec((1, d), lambda i: (0, 0))
    return pl.pallas_call(
        functools.partial(body, n_prompt=n_prompt, n_sample=n_sample),
        grid=(n_blocks,),
        in_specs=[
            pl.BlockSpec((Q_BLOCK, d), lambda i: (jnp.minimum(i, n_prompt - 1), 0)),
            pl.BlockSpec((Q_BLOCK, d), lambda i: (jnp.clip(i - n_prompt, 0, n_sample - 1), 0)),
            pl.BlockSpec((N_META, d), lambda i: (0, 0)),
        ] + [pool_spec] * len(pool_inputs) + [vec_spec] * len(vec_inputs),
        out_specs=[pool_spec] * len(out_dtypes),
        out_shape=[jax.ShapeDtypeStruct((n_blocks * Q_BLOCK, d), dt) for dt in out_dtypes],
        scratch_shapes=[pltpu.VMEM((Q_BLOCK, d), F32)],
        compiler_params=_params(("arbitrary",)),
        name=name,
    )(x_prompt, x_sample, meta, *pool_inputs, *[v.reshape(1, d) for v in vec_inputs])


def _final_postnorm_call(x1, y1, g, first_block, n_blocks, name):
    d = x1.shape[1]
    in_spec = pl.BlockSpec((Q_BLOCK, d), lambda i: (i + first_block, 0))
    return pl.pallas_call(
        _postnorm_kernel,
        grid=(n_blocks,),
        in_specs=[in_spec, in_spec, pl.BlockSpec((1, d), lambda i: (0, 0))],
        out_specs=pl.BlockSpec((Q_BLOCK, d), lambda i: (i, 0)),
        out_shape=jax.ShapeDtypeStruct((n_blocks * Q_BLOCK, d), F32),
        compiler_params=_params(("parallel",)),
        name=name,
    )(x1, y1, g.reshape(1, d))


def _matmul_kernel(*refs, n_x, n_w, n_aux, terms, epilogue, transposed):
    xs = refs[:n_x]
    ws = refs[n_x:n_x + n_w]
    auxs = refs[n_x + n_w:n_x + n_w + n_aux]
    outs = refs[n_x + n_w + n_aux:]
    accs = [lax.dot_general(xs[a][...], ws[b][...],
                            (((1,), (1 if transposed[b] else 0,)), ((), ())),
                            preferred_element_type=F32) for a, b in terms]
    results = epilogue(*accs, *[r[...] for r in auxs])
    for o_ref, r in zip(outs, results):
        o_ref[...] = r.astype(o_ref.dtype)


class _WeightView(NamedTuple):
    array: jax.Array
    row0: int
    rows: int
    col0: int
    cols: int
    transposed: bool = False


def _weight_spec(w, bn):
    if w.transposed:
        return pl.BlockSpec(
            (pl.Element(bn), pl.Element(w.rows)),
            lambda i, j: (pl.multiple_of(w.col0 + j * bn, 2 * V7X_SUBLANES), w.row0))
    return pl.BlockSpec((w.rows, bn), lambda i, j: (w.row0 // w.rows, w.col0 // bn + j))


def _matmul_call(xs, ws, terms, epilogue, out_dtypes, *, bn, auxs=(), name):
    rows = xs[0].shape[0]
    n = ws[0].cols
    bm = _row_tile(rows, 1536)
    bn = max(t for t in range(V7X_LANES, min(bn, n) + 1, V7X_LANES)
             if n % t == 0 and all(w.transposed or w.col0 % t == 0 for w in ws))
    assert all(w.cols == n and w.row0 % w.rows == 0 for w in ws)
    assert all(w.col0 % (2 * V7X_SUBLANES) == 0 for w in ws if w.transposed)
    in_specs = [pl.BlockSpec((bm, x.shape[1]), lambda i, j: (i, 0)) for x in xs]
    in_specs += [_weight_spec(w, bn) for w in ws]
    in_specs += [pl.BlockSpec((1, bn), lambda i, j: (0, j)) for _ in auxs]
    out_spec = pl.BlockSpec((bm, bn), lambda i, j: (i, j))
    body = functools.partial(_matmul_kernel, n_x=len(xs), n_w=len(ws), n_aux=len(auxs),
                             terms=terms, epilogue=epilogue,
                             transposed=tuple(w.transposed for w in ws))
    return pl.pallas_call(
        body,
        grid=(rows // bm, n // bn),
        in_specs=in_specs,
        out_specs=[out_spec] * len(out_dtypes),
        out_shape=[jax.ShapeDtypeStruct((rows, n), dt) for dt in out_dtypes],
        compiler_params=_params(("parallel", "arbitrary")),
        name=name,
    )(*xs, *[w.array for w in ws], *[a.reshape(1, n) for a in auxs])


def _silu(x):
    return x * jax.nn.sigmoid(x)


def _ep_q_qi(q, qi):
    return q * (A_HEAD_DIM ** -0.5 * LOG2_E), qi


def _ep_kv(k, v):
    return k, v, k, v


def _ep_kidx(ki, wi, g, b):
    kc = ki - jnp.mean(ki, axis=-1, keepdims=True)
    kn = kc * lax.rsqrt(jnp.mean(kc * kc, axis=-1, keepdims=True) + EPS) * g + b
    return kn, kn, wi * INDEX_SCALE


def _ep_glu(val, glu):
    return (val * jax.nn.sigmoid(glu),)


def _ep_silu(gate):
    return (_silu(gate),)


def _ep_sum(a, b):
    return (a + b,)


def _ep_identity(a):
    return (a,)


def _ep_odd(hx, cg, bg, gate):
    return cg * hx, bg * _silu(gate)


def _prompt_attn_kernel(qs_ref, qi_ref, wi_ref, sg_ref, ki_ref, k_ref, v_ref, o_ref, *scratch,
                        topk, n_prompt_blocks):
    step = pl.program_id(0)

    @pl.when(step <= n_prompt_blocks)
    def _():
        frame = (step - 1) * Q_BLOCK + lax.broadcasted_iota(I32, (1, Q_BLOCK), 1)
        limit = jnp.where(step == 0, 0, Q_BLOCK + CHUNK * (frame // CHUNK + 1))
        n_kt = jnp.where(step == 0, 1, pl.cdiv(Q_BLOCK * (step + 1), KEY_BLOCK))
        _attend(qs_ref, qi_ref, wi_ref, sg_ref, ki_ref, k_ref, v_ref, *scratch,
                topk=topk, n_lo=N_META, limit=limit, n_kt=n_kt)
        o_ref[...] = scratch[-1][...].astype(o_ref.dtype)

    @pl.when(step > n_prompt_blocks)
    def _():
        o_ref[...] = jnp.zeros(o_ref.shape, o_ref.dtype)


def _sample_attn_kernel(qs_ref, qi_ref, wi_ref, sg_ref, cki_ref, ck_ref, cv_ref, nki_ref, nk_ref, nv_ref,
                        pool_ref, o_ref, ki_buf, k_buf, v_buf, *scratch, topk):
    del pool_ref
    step = pl.program_id(0)
    past, new = cki_ref.shape[0], nk_ref.shape[0]
    for buf, cache_ref, new_ref in ((ki_buf, cki_ref, nki_ref), (k_buf, ck_ref, nk_ref), (v_buf, cv_ref, nv_ref)):
        heads = cache_ref.shape[0] // past
        for r in range(0, past, KEY_BLOCK):
            n = min(KEY_BLOCK, past - r)
            for g in range(heads):
                buf[r:r + n, g * A_HEAD_DIM:(g + 1) * A_HEAD_DIM] = cache_ref[
                    pl.ds(r * heads + g, n, stride=heads), :].astype(BF16)
        buf[past:past + new, :] = new_ref[...]
        buf[past + new:, :] = jnp.zeros((buf.shape[0] - past - new, buf.shape[1]), BF16)
    q0 = pl.multiple_of((step % (Q_BLOCK // new)) * new, new)
    _attend(qs_ref, qi_ref, wi_ref, sg_ref, ki_buf, k_buf, v_buf, *scratch,
            topk=topk, n_lo=past + new, limit=jnp.zeros((1, Q_BLOCK), I32),
            n_kt=ki_buf.shape[0] // KEY_BLOCK - 1, q0=q0, hq=new)
    o_ref[...] = scratch[-1][...].astype(o_ref.dtype)


def _attend(qs_ref, qi_ref, wi_ref, sg_ref, ki_ref, k_ref, v_ref,
            qrows, qirows, keys, bias, s_a, s_b, m_scr, l_scr, acc_scr, o_scr,
            *, topk, n_lo, limit, n_kt, q0=0, hq=Q_BLOCK):
    nq = Q_BLOCK
    lb = KEY_BLOCK
    fold = nq // hq
    assert fold in (1, 2)

    @pl.when(pl.program_id(0) == 0)
    def _():
        one_hot = (lax.broadcasted_iota(I32, (hq, nq), 0) == lax.broadcasted_iota(I32, (hq, nq), 1))
        for h in range(A_HEADS):
            qrows[h * hq:(h + 1) * hq, A_HEAD_DIM:A_HEAD_DIM + nq] = one_hot.astype(BF16)

    for h in range(A_HEADS):
        qrows[h * hq:(h + 1) * hq, 0:A_HEAD_DIM] = qs_ref[pl.ds(q0, hq), h * A_HEAD_DIM:(h + 1) * A_HEAD_DIM]
    for h in range(IDX_HEADS):
        qirows[h * hq:(h + 1) * hq, :] = qi_ref[pl.ds(q0, hq), h * IDX_DIM:(h + 1) * IDX_DIM]
    w_t = jnp.concatenate([wi_ref[pl.ds(q0, hq), :]] * fold, axis=0).T
    lane_head = lax.broadcasted_iota(I32, (1, nq), 1) // hq
    w_tiles = []
    for j in range(IDX_HEADS // fold):
        w_tile = w_t[j * fold:j * fold + 1, :]
        for f in range(1, fold):
            w_tile = jnp.where(lane_head == f, w_t[j * fold + f:j * fold + f + 1, :], w_tile)
        w_tiles.append(w_tile)

    ib = 2 * lb

    def score_block(kt, carry):
        r0 = pl.multiple_of(kt * ib, ib)
        d = lax.dot_general(ki_ref[pl.ds(r0, ib), :], qirows[...], (((1,), (1,)), ((), ())),
                            preferred_element_type=F32)
        sc = jnp.zeros((ib, nq), F32)
        for j, w_tile in enumerate(w_tiles):
            sc = sc + jnp.maximum(d[:, j * nq:(j + 1) * nq], 0.0) * w_tile
        for f in range(1, fold):
            sc = sc + pltpu.roll(sc, f * hq, axis=1)
        pos = r0 + lax.broadcasted_iota(I32, (ib, nq), 0)
        adm = (pos < n_lo) | ((pos >= Q_BLOCK) & (pos < limit))
        bits = lax.bitcast_convert_type(sc, I32)
        key = bits ^ ((bits >> 31) & 0x7FFFFFFF)
        keys[pl.ds(r0, ib), :] = jnp.where(adm, key, INT_MIN)
        return carry

    lax.fori_loop(0, (n_kt + 1) // 2, score_block, 0)

    n_ct = n_kt * (lb // COUNT_ROWS)

    def count_where(pred):
        def body(c, acc):
            r0 = pl.multiple_of(c * COUNT_ROWS, COUNT_ROWS)
            pos = r0 + lax.broadcasted_iota(I32, (COUNT_ROWS, nq), 0)
            hit = jnp.where(pred(keys[pl.ds(r0, COUNT_ROWS), :], pos), 1.0, 0.0)
            parts = [hit[i * V7X_SUBLANES:(i + 1) * V7X_SUBLANES, :]
                     for i in range(COUNT_ROWS // V7X_SUBLANES)]
            while len(parts) > 1:
                parts = [a + b for a, b in zip(parts[::2], parts[1::2])]
            return acc + parts[0]
        acc = lax.fori_loop(0, n_ct, body, jnp.zeros((V7X_SUBLANES, nq), F32))
        return jnp.sum(acc, axis=0, keepdims=True)

    def search_bit(i, state):
        prefix, n_sel = state
        trial = prefix | lax.shift_left(jnp.int32(1), 31 - i)
        cand = trial ^ INT_MIN
        n_trial = count_where(lambda k, pos: k >= cand)
        take = n_trial >= float(topk)
        return jnp.where(take, trial, prefix), jnp.where(take, n_trial, n_sel)

    def search_bits(state):
        group, prefix, n_sel = state
        prefix, n_sel = lax.fori_loop(
            group * SEARCH_GROUP, (group + 1) * SEARCH_GROUP, search_bit, (prefix, n_sel))
        return group + 1, prefix, n_sel

    def search_open(state):
        group, _, n_sel = state
        return (group < 32 // SEARCH_GROUP) & (jnp.max(n_sel) > float(topk))

    n_admissible = (n_lo + jnp.maximum(limit - Q_BLOCK, 0)).astype(F32)
    _, prefix, n_sel = lax.while_loop(search_open, search_bits,
                                      (jnp.int32(0), jnp.zeros((1, nq), I32), n_admissible))
    thr = jnp.maximum(prefix ^ INT_MIN, INT_MIN + 1)

    def write_bias(selected):
        def body(c, carry):
            r0 = pl.multiple_of(c * COUNT_ROWS, COUNT_ROWS)
            pos = r0 + lax.broadcasted_iota(I32, (COUNT_ROWS, nq), 0)
            bias[pl.ds(r0, COUNT_ROWS), :] = jnp.where(
                selected(keys[pl.ds(r0, COUNT_ROWS), :], pos), 0.0, MASKED).astype(BF16)
            return carry
        lax.fori_loop(0, n_ct, body, 0)

    surplus = jnp.max(n_sel) > float(topk)

    @pl.when(jnp.logical_not(surplus))
    def _():
        write_bias(lambda k, pos: k >= thr)

    @pl.when(surplus)
    def _():
        keep = float(topk) - count_where(lambda k, pos: k > thr)
        position_bits = keys.shape[0].bit_length()

        def cut_bit(i, prefix):
            trial = prefix | lax.shift_left(jnp.int32(1), position_bits - 1 - i)
            admitted = count_where(lambda k, pos: (k == thr) & (pos < trial))
            return jnp.where(admitted < keep, trial, prefix)
        cut = lax.fori_loop(0, position_bits, cut_bit, jnp.zeros((1, nq), I32)) + 1
        write_bias(lambda k, pos: (k > thr) | ((k == thr) & (pos < cut)))

    gw = A_GROUP * hq
    m_scr[...] = jnp.full(m_scr.shape, MASKED, F32)
    l_scr[...] = jnp.zeros(l_scr.shape, F32)
    acc_scr[...] = jnp.zeros(acc_scr.shape, F32)

    bias[pl.ds(pl.multiple_of(n_kt * lb, lb), lb), :] = jnp.full((lb, nq), MASKED, BF16)

    def score_keys(kt, s_ref, g):
        r0 = pl.multiple_of(kt * lb, lb)
        k_aug = jnp.concatenate(
            [k_ref[pl.ds(r0, lb), g * A_HEAD_DIM:(g + 1) * A_HEAD_DIM], bias[pl.ds(r0, lb), :]], axis=1)
        s_ref[g] = lax.dot_general(k_aug, qrows[g * gw:(g + 1) * gw, :], (((1,), (1,)), ((), ())),
                                   preferred_element_type=F32)

    def softmax_pv(kt, s_ref, g):
        s = s_ref[g]
        m_old = m_scr[g]
        m_new = jnp.maximum(m_old, jnp.max(s, axis=0, keepdims=True))
        alpha = jnp.exp2(m_old - m_new)
        p = jnp.exp2(s - m_new)
        l_scr[g] = l_scr[g] * alpha + jnp.sum(p, axis=0, keepdims=True)
        v_blk = v_ref[pl.ds(pl.multiple_of(kt * lb, lb), lb), g * A_HEAD_DIM:(g + 1) * A_HEAD_DIM]
        acc_scr[g] = acc_scr[g] * alpha + lax.dot_general(
            v_blk, p.astype(BF16), (((0,), (0,)), ((), ())), preferred_element_type=F32)
        m_scr[g] = m_new

    for g in range(A_KV_HEADS):
        score_keys(0, s_a, g)

    def attend_pair(j, carry):
        kt = 2 * j
        for g in range(A_KV_HEADS):
            score_keys(kt + 1, s_b, g)
        for g in range(A_KV_HEADS):
            softmax_pv(kt, s_a, g)
        for g in range(A_KV_HEADS):
            score_keys(jnp.minimum(kt + 2, n_kt), s_a, g)
        for g in range(A_KV_HEADS):
            softmax_pv(kt + 1, s_b, g)
        return carry

    lax.fori_loop(0, (n_kt + 1) // 2, attend_pair, 0)
    for g in range(A_KV_HEADS):
        o_g = (acc_scr[g] / l_scr[g]).T
        for r in range(A_GROUP):
            h = g * A_GROUP + r
            cols = slice(h * A_HEAD_DIM, (h + 1) * A_HEAD_DIM)
            o_scr[:, cols] = o_g[r * hq:(r + 1) * hq, :] * sg_ref[pl.ds(q0, hq), cols]


def _attn_scratch(lpad, hq=Q_BLOCK):
    gw = A_GROUP * hq
    return [
        pltpu.VMEM((A_HEADS * hq, A_HEAD_DIM + Q_BLOCK), BF16),
        pltpu.VMEM((IDX_HEADS * hq, IDX_DIM), BF16),
        pltpu.VMEM((lpad, Q_BLOCK), I32),
        pltpu.VMEM((lpad, Q_BLOCK), BF16),
        pltpu.VMEM((A_KV_HEADS, KEY_BLOCK, gw), F32),
        pltpu.VMEM((A_KV_HEADS, KEY_BLOCK, gw), F32),
        pltpu.VMEM((A_KV_HEADS, 1, gw), F32),
        pltpu.VMEM((A_KV_HEADS, 1, gw), F32),
        pltpu.VMEM((A_KV_HEADS, A_HEAD_DIM, gw), F32),
        pltpu.VMEM((hq, A_WIDTH), F32),
    ]


def _query_specs(q_index):
    qspec = pl.BlockSpec((Q_BLOCK, A_WIDTH), lambda s: (q_index(s), 0))
    return [qspec, qspec, pl.BlockSpec((Q_BLOCK, V7X_LANES), lambda s: (q_index(s), 0)), qspec]


def _prompt_attn_call(qs, qi, wi, sg, ki_seq, k_seq, v_seq, *, s_len, meta_block, topk, casts=()):
    lpad = ki_seq.shape[0]
    n_qb = s_len // Q_BLOCK
    full = lambda a: pl.BlockSpec(a.shape, lambda s: (0, 0))
    cast_specs, cast_shapes = _cast_specs(casts, 1, n_qb)
    body = functools.partial(_prompt_attn_kernel, topk=topk, n_prompt_blocks=n_qb)
    return pl.pallas_call(
        _with_casts(body, 7, len(casts)),
        grid=(1 + meta_block,),
        in_specs=_query_specs(lambda s: jnp.where(s == 0, meta_block, s - 1))
        + [full(ki_seq), full(k_seq), full(v_seq)] + cast_specs,
        out_specs=[pl.BlockSpec((Q_BLOCK, A_WIDTH), lambda s: (jnp.where(s == 0, meta_block, s - 1), 0))]
        + cast_specs,
        out_shape=[jax.ShapeDtypeStruct(((meta_block + 1) * Q_BLOCK, A_WIDTH), BF16)] + cast_shapes,
        scratch_shapes=_attn_scratch(lpad),
        compiler_params=_params(("arbitrary",)),
        name="prompt_attention",
    )(qs, qi, wi, sg, ki_seq, k_seq, v_seq, *casts)


def _sample_attn_call(qs, qi, wi, sg, cache_ki, cache_k, cache_v, ki16, k16, v16, pool_out,
                      *, s_len, topk):
    ndb, past, _ = cache_ki.shape
    new = SEQ_TILE
    lpad = _round_up(past + new, KEY_BLOCK) + KEY_BLOCK
    cache_spec = lambda a: pl.BlockSpec((None,) + a.shape[1:], lambda s: (s, 0, 0))
    new_spec = lambda a: pl.BlockSpec((new, a.shape[1]), lambda s: (s_len // new + s, 0))
    return pl.pallas_call(
        functools.partial(_sample_attn_kernel, topk=topk),
        grid=(ndb,),
        in_specs=_query_specs(lambda s: s_len // Q_BLOCK + s // (Q_BLOCK // new))
        + [cache_spec(cache_ki), cache_spec(cache_k), cache_spec(cache_v),
           new_spec(ki16), new_spec(k16), new_spec(v16), pl.BlockSpec(memory_space=pl.ANY)],
        out_specs=pl.BlockSpec((new, A_WIDTH), lambda s: (s_len // new + s, 0)),
        out_shape=jax.ShapeDtypeStruct(pool_out.shape, BF16),
        input_output_aliases={10: 0},
        scratch_shapes=[pltpu.VMEM((lpad, IDX_DIM), BF16), pltpu.VMEM((lpad, A_KV_WIDTH), BF16),
                        pltpu.VMEM((lpad, A_KV_WIDTH), BF16)] + _attn_scratch(lpad, new),
        compiler_params=_params(("arbitrary",)),
        name="sample_attention",
    )(qs, qi, wi, sg, cache_ki, cache_k, cache_v, ki16, k16, v16, pool_out)


CONV_UNITS = Q_BLOCK // SEQ_TILE


def _stream_conv_step(n_prompt_steps, halo, ext, state_ref, cur_ref, unit_fn):
    step = pl.program_id(0)
    is_sample = step > n_prompt_steps

    @pl.when(step == 0)
    def _():
        ext[0:halo, :] = jnp.zeros((halo, ext.shape[1]), F32)

    ext[halo:halo + Q_BLOCK, :] = cur_ref[...]
    for u in range(CONV_UNITS):
        @pl.when(is_sample)
        def _():
            ext[u * SEQ_TILE:u * SEQ_TILE + halo, :] = state_ref[u]

        unit_fn(u)

    @pl.when(step == 0)
    def _():
        if halo > N_META:
            ext[0:halo - N_META, :] = jnp.zeros((halo - N_META, ext.shape[1]), F32)
        keep = min(halo, N_META)
        ext[halo - keep:halo, :] = ext[halo + N_META - keep:halo + N_META, :]

    @pl.when(step > 0)
    def _():
        ext[0:halo, :] = ext[Q_BLOCK:Q_BLOCK + halo, :]


def _causal_taps(ext, unit, w_ref, halo, width, cols):
    first = halo - (width - 1)
    x = ext[unit * SEQ_TILE:unit * SEQ_TILE + halo + SEQ_TILE, cols]
    rows = x.shape[0]
    rotated = {0: x}
    for j in range(width):
        r = (first + j) % V7X_SUBLANES
        if r not in rotated:
            rotated[r] = pltpu.roll(x, rows - r, axis=0)
    acc = None
    for j in range(width):
        r = (first + j) % V7X_SUBLANES
        base = first + j - r
        term = rotated[r][base:base + SEQ_TILE, :] * w_ref[j:j + 1, cols]
        acc = term if acc is None else acc + term
    return acc


def _bconv_kernel(u_ref, st_ref, sg_ref, w_ref, cb_ref, g_ref, b_ref, o_ref, ext, y_scr,
                  *, n_prompt_steps):
    width = u_ref.shape[1]

    def unit_fn(unit):
        rows = slice(unit * SEQ_TILE, (unit + 1) * SEQ_TILE)
        total = jnp.zeros((SEQ_TILE, V7X_LANES), F32)
        for c in range(width // V7X_LANES):
            cols = slice(c * V7X_LANES, (c + 1) * V7X_LANES)
            y = _causal_taps(ext, unit, w_ref, B_HALO, B_CONV, cols) + cb_ref[:, cols]
            y_scr[:, cols] = y
            total = total + y
        mean = jnp.sum(total, axis=-1, keepdims=True) / width
        sq = jnp.zeros((SEQ_TILE, V7X_LANES), F32)
        for c in range(width // V7X_LANES):
            cols = slice(c * V7X_LANES, (c + 1) * V7X_LANES)
            yc = y_scr[:, cols] - mean
            sq = sq + yc * yc
        inv = lax.rsqrt(jnp.sum(sq, axis=-1, keepdims=True) / width + EPS)
        for c in range(width // V7X_LANES):
            cols = slice(c * V7X_LANES, (c + 1) * V7X_LANES)
            yn = (y_scr[:, cols] - mean) * inv * g_ref[:, cols] + b_ref[:, cols]
            o_ref[rows, cols] = (_silu(yn) * sg_ref[rows, cols]).astype(o_ref.dtype)

    _stream_conv_step(n_prompt_steps, B_HALO, ext, st_ref, u_ref, unit_fn)


def _cconv_kernel(z_ref, st_ref, bgs_ref, w_ref, cb_ref, o_ref, ext, *, n_prompt_steps):
    width = z_ref.shape[1]

    def unit_fn(unit):
        rows = slice(unit * SEQ_TILE, (unit + 1) * SEQ_TILE)
        for c in range(width // V7X_LANES):
            cols = slice(c * V7X_LANES, (c + 1) * V7X_LANES)
            y = _causal_taps(ext, unit, w_ref, C_HALO, C_CONV, cols) + cb_ref[:, cols]
            o_ref[rows, cols] = (bgs_ref[rows, cols] * y).astype(o_ref.dtype)

    _stream_conv_step(n_prompt_steps, C_HALO, ext, st_ref, z_ref, unit_fn)


def _cast_specs(weights, first_step, n_steps):
    specs, shapes = [], []
    for w in weights:
        slab = w.shape[0] // n_steps
        assert w.shape[0] % n_steps == 0 and slab % (2 * V7X_SUBLANES) == 0
        specs.append(pl.BlockSpec((slab, w.shape[1]), lambda s: (jnp.clip(s - first_step, 0, n_steps - 1), 0)))
        shapes.append(jax.ShapeDtypeStruct(w.shape, BF16))
    return specs, shapes


def _with_casts(body, n_inputs, n_casts):
    def kernel_body(*refs):
        sources = refs[n_inputs:n_inputs + n_casts]
        results = refs[n_inputs + n_casts + 1:n_inputs + 2 * n_casts + 1]
        for src, dst in zip(sources, results):
            dst[...] = src[...].astype(dst.dtype)
        body(*refs[:n_inputs], refs[n_inputs + n_casts], *refs[n_inputs + 2 * n_casts + 1:])
    return kernel_body


def _stream_conv_call(body, x, state, gate, gate_col_block, vecs, *, halo, n_prompt, n_sample,
                      scratch, name, casts=()):
    rows, width = x.shape
    n_blocks = n_prompt + n_sample
    assert rows == (n_blocks + 1) * Q_BLOCK
    assert state.shape[0] == n_sample * CONV_UNITS
    state = state.reshape(n_sample, CONV_UNITS, halo, width)
    block = lambda s: jnp.where(s == 0, n_blocks, s - 1)
    row_spec = pl.BlockSpec((Q_BLOCK, width), lambda s: (block(s), 0))
    vec_specs = [pl.BlockSpec(v.shape, lambda s: (0, 0)) for v in vecs]
    cast_specs, cast_shapes = _cast_specs(casts, 1, n_prompt)
    return pl.pallas_call(
        _with_casts(functools.partial(body, n_prompt_steps=n_prompt), 3 + len(vecs), len(casts)),
        grid=(1 + n_blocks,),
        in_specs=[
            row_spec,
            pl.BlockSpec((None, CONV_UNITS, halo, width),
                         lambda s: (jnp.clip(s - 1 - n_prompt, 0, n_sample - 1), 0, 0, 0)),
            pl.BlockSpec((Q_BLOCK, width), lambda s: (block(s), gate_col_block)),
        ] + vec_specs + cast_specs,
        out_specs=[row_spec] + cast_specs,
        out_shape=[jax.ShapeDtypeStruct((rows, width), BF16)] + cast_shapes,
        scratch_shapes=[pltpu.VMEM((halo + Q_BLOCK, width), F32)] + scratch,
        compiler_params=_params(("arbitrary",)),
        name=name,
    )(x, state, gate, *vecs, *casts)


def _pad_rows_front(a, rows):
    return jnp.pad(a, ((0, 0), (rows - a.shape[1], 0), (0, 0)))


def kernel(x_prompt, x_sample, cache_a_k, cache_a_v, cache_a_kidx, state_b_conv, state_c_conv, meta,
           e_norm_pre, e_w_in, e_kidx_ln_g, e_kidx_ln_b, e_bconv_w, e_bconv_b, e_bln_g, e_bln_b,
           e_w_out, e_norm_post, o_norm_pre, o_w_in, o_conv_w, o_conv_b, o_w_out, o_norm_post):
    nb, s_len, d = x_prompt.shape
    ndb, nds, _ = x_sample.shape
    past = cache_a_k.shape[2]
    bw = d // 2
    assert nb == 1 and e_w_in.shape[0] == 1 and o_w_in.shape[0] == 1
    assert nds == SEQ_TILE and s_len % Q_BLOCK == 0 and (ndb * nds) % Q_BLOCK == 0
    assert s_len >= B_CONV - 1 and A_WIDTH % bw == 0 and bw % V7X_LANES == 0
    ns = ndb * nds
    meta_row = s_len + ns
    rows = meta_row + Q_BLOCK
    topk_p = min(TOPK_MAX, s_len // 4)
    topk_s = min(TOPK_MAX, (past + nds) // 4)

    sources = (x_prompt[0], x_sample.reshape(ns, d), meta)

    w_in = e_w_in[0].T.astype(BF16)
    widths = (A_WIDTH, A_KV_WIDTH, A_KV_WIDTH, IDX_Q_WIDTH, IDX_DIM, IDX_HEADS, bw, bw, A_WIDTH + bw)
    starts = [sum(widths[:i]) for i in range(len(widths))]
    w_q, w_k, w_v, w_qi, w_ki, w_wi, w_val, w_glu, w_gate = [
        _WeightView(w_in, 0, d, s, max(w, V7X_LANES), transposed=True) for s, w in zip(starts, widths)]

    h0, = _pool_rowwise_call(_prenorm_kernel, *sources, [], [e_norm_pre[0]], [BF16], "even_prenorm")
    q_s, q_i = _matmul_call([h0], [w_q, w_qi], [(0, 0), (0, 1)], _ep_q_qi, [BF16, BF16], bn=512,
                            name="even_q_qidx")
    k32, v32, k16, v16 = _matmul_call([h0], [w_k, w_v], [(0, 0), (0, 1)], _ep_kv,
                                      [F32, F32, BF16, BF16], bn=512, name="even_kv")
    ki32, ki16, wi = _matmul_call([h0], [w_ki, w_wi], [(0, 0), (0, 1)], _ep_kidx, [F32, BF16, F32],
                                  bn=V7X_LANES, auxs=[e_kidx_ln_g[0], e_kidx_ln_b[0]], name="even_kidx")
    u, = _matmul_call([h0], [w_val, w_glu], [(0, 0), (0, 1)], _ep_glu, [F32], bn=512, name="even_glu")
    sg, = _matmul_call([h0], [w_gate], [(0, 0)], _ep_silu, [F32], bn=1024, name="even_gate")

    lpad_p = _round_up(Q_BLOCK + s_len, KEY_BLOCK) + KEY_BLOCK
    seq = lambda a: jnp.concatenate(
        [a[meta_row:meta_row + Q_BLOCK], a[:s_len],
         jnp.zeros((lpad_p - Q_BLOCK - s_len, a.shape[1]), a.dtype)], axis=0)
    oa, w_out, w_out_odd = _prompt_attn_call(
        q_s, q_i, wi, sg, seq(ki16), seq(k16), seq(v16), s_len=s_len,
        meta_block=meta_row // Q_BLOCK, topk=topk_p, casts=(e_w_out[0], o_w_out[0]))
    w_out_a = _WeightView(w_out, 0, A_WIDTH, 0, d)
    w_out_b = _WeightView(w_out, A_WIDTH, bw, 0, d)
    oa = _sample_attn_call(q_s, q_i, wi, sg, cache_a_kidx[0],
                           cache_a_k[0].reshape(ndb, past * A_KV_HEADS, A_HEAD_DIM),
                           cache_a_v[0].reshape(ndb, past * A_KV_HEADS, A_HEAD_DIM), ki16, k16, v16, oa,
                           s_len=s_len, topk=topk_s)

    conv_steps = dict(n_prompt=s_len // Q_BLOCK, n_sample=ns // Q_BLOCK)
    bconv_w = jnp.pad(e_bconv_w[0], ((0, B_HALO - B_CONV), (0, 0)))
    ob, w_odd = _stream_conv_call(
        _bconv_kernel, u, _pad_rows_front(state_b_conv[0], B_HALO), sg, A_WIDTH // bw,
        [bconv_w, e_bconv_b[0].reshape(1, bw), e_bln_g[0].reshape(1, bw), e_bln_b[0].reshape(1, bw)],
        halo=B_HALO, scratch=[pltpu.VMEM((SEQ_TILE, bw), F32)], name="conformer_conv",
        casts=(o_w_in[0],), **conv_steps)

    y0, = _matmul_call([oa, ob], [w_out_a, w_out_b], [(0, 0), (1, 1)], _ep_sum, [BF16],
                       bn=1024, name="even_out")
    x1, h1 = _pool_rowwise_call(_postnorm_prenorm_kernel, *sources, [y0],
                                [e_norm_post[0], o_norm_pre[0]], [F32, BF16], "even_postnorm_odd_prenorm")

    w_hx, w_cg, w_bg, w_og = [_WeightView(w_odd, 0, d, i * d, d) for i in range(4)]
    z, bgs = _matmul_call([h1], [w_hx, w_cg, w_bg, w_og], [(0, 0), (0, 1), (0, 2), (0, 3)], _ep_odd,
                          [F32, BF16], bn=256, name="odd_in")
    cconv_w = jnp.pad(o_conv_w[0], ((0, C_HALO - C_CONV), (0, 0)))
    o1, = _stream_conv_call(
        _cconv_kernel, z, _pad_rows_front(state_c_conv[0], C_HALO), bgs, 0,
        [cconv_w, o_conv_b[0].reshape(1, d)],
        halo=C_HALO, scratch=[], name="short_conv", **conv_steps)
    y1, = _matmul_call([o1], [_WeightView(w_out_odd, 0, d, 0, d)], [(0, 0)],
                       _ep_identity, [BF16], bn=1024, name="odd_out")
    y_prompt = _final_postnorm_call(x1, y1, o_norm_post[0], 0, s_len // Q_BLOCK, "odd_postnorm_prompt")
    y_sample = _final_postnorm_call(x1, y1, o_norm_post[0], s_len // Q_BLOCK, ns // Q_BLOCK,
                                    "odd_postnorm_sample")

    def prompt_rows(a):
        return jnp.concatenate([a[meta_row:meta_row + N_META], a[:s_len]], axis=0)

    def sample_rows(a):
        return a[s_len:meta_row].reshape(ndb, nds, a.shape[1])

    kv_shape = (A_KV_HEADS, A_HEAD_DIM)
    y_prompt = y_prompt[None]
    y_sample = y_sample.reshape(ndb, nds, d)
    prompt_a_k = prompt_rows(k32).reshape(1, 1, N_META + s_len, *kv_shape)
    prompt_a_v = prompt_rows(v32).reshape(1, 1, N_META + s_len, *kv_shape)
    prompt_a_kidx = prompt_rows(ki32)[None, None]
    prompt_b_conv = u[s_len - (B_CONV - 1):s_len][None, None]
    prompt_c_conv = z[s_len - (C_CONV - 1):s_len][None, None]
    sample_a_k = sample_rows(k32).reshape(1, ndb, nds, *kv_shape)
    sample_a_v = sample_rows(v32).reshape(1, ndb, nds, *kv_shape)
    sample_a_kidx = sample_rows(ki32)[None]
    sample_b_conv = sample_rows(u)[None, :, nds - (B_CONV - 1):]
    sample_c_conv = sample_rows(z)[None, :, nds - (C_CONV - 1):]
    return (y_prompt, y_sample, prompt_a_k, prompt_a_v, prompt_a_kidx, prompt_b_conv, prompt_c_conv,
            sample_a_k, sample_a_v, sample_a_kidx, sample_b_conv, sample_c_conv)
```

```python
import functools
from typing import NamedTuple

import jax
import jax.numpy as jnp
from jax import lax
from jax.experimental import pallas as pl
from jax.experimental.pallas import tpu as pltpu

CHUNK = 64
N_META = 16
EPS = 1e-6
A_HEADS = 16
A_KV_HEADS = 4
A_HEAD_DIM = 128
A_WIDTH = A_HEADS * A_HEAD_DIM
A_KV_WIDTH = A_KV_HEADS * A_HEAD_DIM
A_GROUP = A_HEADS // A_KV_HEADS
IDX_HEADS = 16
IDX_DIM = 128
IDX_Q_WIDTH = IDX_HEADS * IDX_DIM
INDEX_SCALE = (IDX_HEADS * IDX_DIM) ** -0.5
TOPK_MAX = 256
B_CONV = 31
C_CONV = 3

V7X_LANES = 128
V7X_SUBLANES = 8
V7X_VMEM_LIMIT_BYTES = 56 * 1024 * 1024

Q_BLOCK = 128
KEY_BLOCK = 512
COUNT_ROWS = 512
SEARCH_GROUP = 4
SEQ_TILE = 64
B_HALO = 32
C_HALO = 8
INT_MIN = -2 ** 31
MASKED = -1e30
LOG2_E = 1.4426950408889634

F32 = jnp.float32
BF16 = jnp.bfloat16
I32 = jnp.int32


def _round_up(x, m):
    return (x + m - 1) // m * m


def _row_tile(rows, cap):
    best = 16
    for t in range(16, cap + 1, 16):
        if rows % t == 0:
            best = t
    return best


def _params(sem):
    return pltpu.CompilerParams(dimension_semantics=sem, vmem_limit_bytes=V7X_VMEM_LIMIT_BYTES)


def _rms(x, g):
    return x * lax.rsqrt(jnp.mean(x * x, axis=-1, keepdims=True) + EPS) * g


def _gather_pool_block(xp_ref, xs_ref, meta_ref, x_scr, n_prompt, n_sample):
    i = pl.program_id(0)

    @pl.when(i < n_prompt)
    def _():
        x_scr[...] = xp_ref[...]

    @pl.when((i >= n_prompt) & (i < n_prompt + n_sample))
    def _():
        x_scr[...] = xs_ref[...]

    @pl.when(i == n_prompt + n_sample)
    def _():
        x_scr[0:N_META, :] = meta_ref[...]
        x_scr[N_META:, :] = jnp.zeros((Q_BLOCK - N_META, x_scr.shape[1]), F32)


def _prenorm_kernel(xp_ref, xs_ref, meta_ref, g_ref, h_ref, x_scr, *, n_prompt, n_sample):
    _gather_pool_block(xp_ref, xs_ref, meta_ref, x_scr, n_prompt, n_sample)
    h_ref[...] = _rms(x_scr[...], g_ref[...]).astype(h_ref.dtype)


def _postnorm_prenorm_kernel(xp_ref, xs_ref, meta_ref, y_ref, gpost_ref, gpre_ref, xo_ref, h_ref, x_scr,
                             *, n_prompt, n_sample):
    _gather_pool_block(xp_ref, xs_ref, meta_ref, x_scr, n_prompt, n_sample)
    x1 = x_scr[...] + _rms(y_ref[...].astype(F32), gpost_ref[...])
    xo_ref[...] = x1
    h_ref[...] = _rms(x1, gpre_ref[...]).astype(h_ref.dtype)


def _postnorm_kernel(x_ref, y_ref, gpost_ref, xo_ref):
    xo_ref[...] = x_ref[...] + _rms(y_ref[...].astype(F32), gpost_ref[...])


def _pool_rowwise_call(body, x_prompt, x_sample, meta, pool_inputs, vec_inputs, out_dtypes, name):
    s_len, d = x_prompt.shape
    n_prompt, n_sample = s_len // Q_BLOCK, x_sample.shape[0] // Q_BLOCK
    n_blocks = n_prompt + n_sample + 1
    pool_spec = pl.BlockSpec((Q_BLOCK, d), lambda i: (i, 0))
    vec_spec = pl.BlockSpec((1, d), lambda i: (0, 0))
    return pl.pallas_call(
        functools.partial(body, n_prompt=n_prompt, n_sample=n_sample),
        grid=(n_blocks,),
        in_specs=[
            pl.BlockSpec((Q_BLOCK, d), lambda i: (jnp.minimum(i, n_prompt - 1), 0)),
            pl.BlockSpec((Q_BLOCK, d), lambda i: (jnp.clip(i - n_prompt, 0, n_sample - 1), 0)),
            pl.BlockSpec((N_META, d), lambda i: (0, 0)),
        ] + [pool_spec] * len(pool_inputs) + [vec_spec] * len(vec_inputs),
        out_specs=[pool_spec] * len(out_dtypes),
        out_shape=[jax.ShapeDtypeStruct((n_blocks * Q_BLOCK, d), dt) for dt in out_dtypes],
        scratch_shapes=[pltpu.VMEM((Q_BLOCK, d), F32)],
        compiler_params=_params(("arbitrary",)),
        name=name,
    )(x_prompt, x_sample, meta, *pool_inputs, *[v.reshape(1, d) for v in vec_inputs])


def _final_postnorm_call(x1, y1, g, first_block, n_blocks, name):
    d = x1.shape[1]
    in_spec = pl.BlockSpec((Q_BLOCK, d), lambda i: (i + first_block, 0))
    return pl.pallas_call(
        _postnorm_kernel,
        grid=(n_blocks,),
        in_specs=[in_spec, in_spec, pl.BlockSpec((1, d), lambda i: (0, 0))],
        out_specs=pl.BlockSpec((Q_BLOCK, d), lambda i: (i, 0)),
        out_shape=jax.ShapeDtypeStruct((n_blocks * Q_BLOCK, d), F32),
        compiler_params=_params(("parallel",)),
        name=name,
    )(x1, y1, g.reshape(1, d))


def _matmul_kernel(*refs, n_x, n_w, n_aux, terms, epilogue, transposed):
    xs = refs[:n_x]
    ws = refs[n_x:n_x + n_w]
    auxs = refs[n_x + n_w:n_x + n_w + n_aux]
    outs = refs[n_x + n_w + n_aux:]
    accs = [lax.dot_general(xs[a][...], ws[b][...],
                            (((1,), (1 if transposed[b] else 0,)), ((), ())),
                            preferred_element_type=F32) for a, b in terms]
    results = epilogue(*accs, *[r[...] for r in auxs])
    for o_ref, r in zip(outs, results):
        o_ref[...] = r.astype(o_ref.dtype)


class _WeightView(NamedTuple):
    array: jax.Array
    row0: int
    rows: int
    col0: int
    cols: int
    transposed: bool = False


def _weight_spec(w, bn):
    if w.transposed:
        return pl.BlockSpec(
            (pl.Element(bn), pl.Element(w.rows)),
            lambda i, j: (pl.multiple_of(w.col0 + j * bn, 2 * V7X_SUBLANES), w.row0))
    return pl.BlockSpec((w.rows, bn), lambda i, j: (w.row0 // w.rows, w.col0 // bn + j))


def _matmul_call(xs, ws, terms, epilogue, out_dtypes, *, bn, auxs=(), name):
    rows = xs[0].shape[0]
    n = ws[0].cols
    bm = _row_tile(rows, 1536)
    bn = max(t for t in range(V7X_LANES, min(bn, n) + 1, V7X_LANES)
             if n % t == 0 and all(w.transposed or w.col0 % t == 0 for w in ws))
    assert all(w.cols == n and w.row0 % w.rows == 0 for w in ws)
    assert all(w.col0 % (2 * V7X_SUBLANES) == 0 for w in ws if w.transposed)
    in_specs = [pl.BlockSpec((bm, x.shape[1]), lambda i, j: (i, 0)) for x in xs]
    in_specs += [_weight_spec(w, bn) for w in ws]
    in_specs += [pl.BlockSpec((1, bn), lambda i, j: (0, j)) for _ in auxs]
    out_spec = pl.BlockSpec((bm, bn), lambda i, j: (i, j))
    body = functools.partial(_matmul_kernel, n_x=len(xs), n_w=len(ws), n_aux=len(auxs),
                             terms=terms, epilogue=epilogue,
                             transposed=tuple(w.transposed for w in ws))
    return pl.pallas_call(
        body,
        grid=(rows // bm, n // bn),
        in_specs=in_specs,
        out_specs=[out_spec] * len(out_dtypes),
        out_shape=[jax.ShapeDtypeStruct((rows, n), dt) for dt in out_dtypes],
        compiler_params=_params(("parallel", "arbitrary")),
        name=name,
    )(*xs, *[w.array for w in ws], *[a.reshape(1, n) for a in auxs])


def _kv_kernel(x_ref, wk_ref, wv_ref, k32_ref, v32_ref, k16_ref, v16_ref):
    bm = x_ref.shape[0]
    for w_ref, o32_ref, o16_ref in ((wk_ref, k32_ref, k16_ref), (wv_ref, v32_ref, v16_ref)):
        acc = lax.dot_general(x_ref[...], w_ref[...], (((1,), (1,)), ((), ())),
                              preferred_element_type=F32)
        o16_ref[...] = acc.astype(o16_ref.dtype)
        for g in range(A_KV_HEADS):
            o32_ref[pl.ds(g, bm, stride=A_KV_HEADS), :] = acc[:, g * A_HEAD_DIM:(g + 1) * A_HEAD_DIM]


def _kv_call(x, w_k, w_v, name):
    rows = x.shape[0]
    bm = _row_tile(rows, 1536)
    assert w_k.transposed and w_v.transposed and w_k.cols == A_KV_WIDTH == w_v.cols
    head_rows = pl.BlockSpec((bm * A_KV_HEADS, A_HEAD_DIM), lambda i, j: (i, 0))
    wide = pl.BlockSpec((bm, A_KV_WIDTH), lambda i, j: (i, 0))
    return pl.pallas_call(
        _kv_kernel,
        grid=(rows // bm, 1),
        in_specs=[pl.BlockSpec((bm, x.shape[1]), lambda i, j: (i, 0)),
                  _weight_spec(w_k, A_KV_WIDTH), _weight_spec(w_v, A_KV_WIDTH)],
        out_specs=[head_rows, head_rows, wide, wide],
        out_shape=[jax.ShapeDtypeStruct((rows * A_KV_HEADS, A_HEAD_DIM), F32)] * 2
        + [jax.ShapeDtypeStruct((rows, A_KV_WIDTH), BF16)] * 2,
        compiler_params=_params(("parallel", "arbitrary")),
        name=name,
    )(x, w_k.array, w_v.array)


def _silu(x):
    return x * jax.nn.sigmoid(x)


def _ep_q_qi(q, qi):
    return q * (A_HEAD_DIM ** -0.5 * LOG2_E), qi


def _ep_kidx(ki, wi, g, b):
    kc = ki - jnp.mean(ki, axis=-1, keepdims=True)
    kn = kc * lax.rsqrt(jnp.mean(kc * kc, axis=-1, keepdims=True) + EPS) * g + b
    return kn, kn, wi * INDEX_SCALE


def _ep_glu(val, glu):
    return (val * jax.nn.sigmoid(glu),)


def _ep_silu(gate):
    return (_silu(gate),)


def _ep_sum(a, b):
    return (a + b,)


def _ep_identity(a):
    return (a,)


def _ep_odd(hx, cg, bg, gate):
    return cg * hx, bg * _silu(gate)


def _prompt_attn_kernel(qs_ref, qi_ref, wi_ref, sg_ref, ki_ref, k_ref, v_ref, o_ref, *scratch,
                        topk, n_prompt_blocks):
    step = pl.program_id(0)

    @pl.when(step <= n_prompt_blocks)
    def _():
        frame = (step - 1) * Q_BLOCK + lax.broadcasted_iota(I32, (1, Q_BLOCK), 1)
        limit = jnp.where(step == 0, 0, Q_BLOCK + CHUNK * (frame // CHUNK + 1))
        n_kt = jnp.where(step == 0, 1, pl.cdiv(Q_BLOCK * (step + 1), KEY_BLOCK))
        _attend(qs_ref, qi_ref, wi_ref, sg_ref, ki_ref, k_ref, v_ref, *scratch,
                topk=topk, n_lo=N_META, limit=limit, n_kt=n_kt)
        o_ref[...] = scratch[-1][...].astype(o_ref.dtype)

    @pl.when(step > n_prompt_blocks)
    def _():
        o_ref[...] = jnp.zeros(o_ref.shape, o_ref.dtype)


def _sample_attn_kernel(qs_ref, qi_ref, wi_ref, sg_ref, cki_ref, ck_ref, cv_ref, nki_ref, nk_ref, nv_ref,
                        pool_ref, o_ref, ki_buf, k_buf, v_buf, *scratch, topk):
    del pool_ref
    step = pl.program_id(0)
    past, new = cki_ref.shape[0], nk_ref.shape[0]
    for buf, cache_ref, new_ref in ((ki_buf, cki_ref, nki_ref), (k_buf, ck_ref, nk_ref), (v_buf, cv_ref, nv_ref)):
        heads = cache_ref.shape[0] // past
        for r in range(0, past, KEY_BLOCK):
            n = min(KEY_BLOCK, past - r)
            for g in range(heads):
                buf[r:r + n, g * A_HEAD_DIM:(g + 1) * A_HEAD_DIM] = cache_ref[
                    pl.ds(r * heads + g, n, stride=heads), :].astype(BF16)
        buf[past:past + new, :] = new_ref[...]
        buf[past + new:, :] = jnp.zeros((buf.shape[0] - past - new, buf.shape[1]), BF16)
    q0 = pl.multiple_of((step % (Q_BLOCK // new)) * new, new)
    _attend(qs_ref, qi_ref, wi_ref, sg_ref, ki_buf, k_buf, v_buf, *scratch,
            topk=topk, n_lo=past + new, limit=jnp.zeros((1, Q_BLOCK), I32),
            n_kt=ki_buf.shape[0] // KEY_BLOCK - 1, q0=q0, hq=new)
    o_ref[...] = scratch[-1][...].astype(o_ref.dtype)


def _attend(qs_ref, qi_ref, wi_ref, sg_ref, ki_ref, k_ref, v_ref,
            qrows, qirows, keys, bias, s_a, s_b, m_scr, l_scr, acc_scr, o_scr,
            *, topk, n_lo, limit, n_kt, q0=0, hq=Q_BLOCK):
    nq = Q_BLOCK
    lb = KEY_BLOCK
    fold = nq // hq
    assert fold in (1, 2)

    @pl.when(pl.program_id(0) == 0)
    def _():
        one_hot = (lax.broadcasted_iota(I32, (hq, nq), 0) == lax.broadcasted_iota(I32, (hq, nq), 1))
        for h in range(A_HEADS):
            qrows[h * hq:(h + 1) * hq, A_HEAD_DIM:A_HEAD_DIM + nq] = one_hot.astype(BF16)

    for h in range(A_HEADS):
        qrows[h * hq:(h + 1) * hq, 0:A_HEAD_DIM] = qs_ref[pl.ds(q0, hq), h * A_HEAD_DIM:(h + 1) * A_HEAD_DIM]
    for h in range(IDX_HEADS):
        qirows[h * hq:(h + 1) * hq, :] = qi_ref[pl.ds(q0, hq), h * IDX_DIM:(h + 1) * IDX_DIM]
    w_t = jnp.concatenate([wi_ref[pl.ds(q0, hq), :]] * fold, axis=0).T
    lane_head = lax.broadcasted_iota(I32, (1, nq), 1) // hq
    w_tiles = []
    for j in range(IDX_HEADS // fold):
        w_tile = w_t[j * fold:j * fold + 1, :]
        for f in range(1, fold):
            w_tile = jnp.where(lane_head == f, w_t[j * fold + f:j * fold + f + 1, :], w_tile)
        w_tiles.append(w_tile)

    ib = 2 * lb

    def score_block(kt, carry):
        r0 = pl.multiple_of(kt * ib, ib)
        d = lax.dot_general(ki_ref[pl.ds(r0, ib), :], qirows[...], (((1,), (1,)), ((), ())),
                            preferred_element_type=F32)
        sc = jnp.zeros((ib, nq), F32)
        for j, w_tile in enumerate(w_tiles):
            sc = sc + jnp.maximum(d[:, j * nq:(j + 1) * nq], 0.0) * w_tile
        for f in range(1, fold):
            sc = sc + pltpu.roll(sc, f * hq, axis=1)
        pos = r0 + lax.broadcasted_iota(I32, (ib, nq), 0)
        adm = (pos < n_lo) | ((pos >= Q_BLOCK) & (pos < limit))
        bits = lax.bitcast_convert_type(sc, I32)
        key = bits ^ ((bits >> 31) & 0x7FFFFFFF)
        keys[pl.ds(r0, ib), :] = jnp.where(adm, key, INT_MIN)
        return carry

    lax.fori_loop(0, (n_kt + 1) // 2, score_block, 0)

    n_ct = n_kt * (lb // COUNT_ROWS)

    def count_where(pred):
        def body(c, acc):
            r0 = pl.multiple_of(c * COUNT_ROWS, COUNT_ROWS)
            pos = r0 + lax.broadcasted_iota(I32, (COUNT_ROWS, nq), 0)
            hit = jnp.where(pred(keys[pl.ds(r0, COUNT_ROWS), :], pos), 1.0, 0.0)
            parts = [hit[i * V7X_SUBLANES:(i + 1) * V7X_SUBLANES, :]
                     for i in range(COUNT_ROWS // V7X_SUBLANES)]
            while len(parts) > 1:
                parts = [a + b for a, b in zip(parts[::2], parts[1::2])]
            return acc + parts[0]
        acc = lax.fori_loop(0, n_ct, body, jnp.zeros((V7X_SUBLANES, nq), F32))
        return jnp.sum(acc, axis=0, keepdims=True)

    def search_bit(i, state):
        prefix, n_sel = state
        trial = prefix | lax.shift_left(jnp.int32(1), 31 - i)
        cand = trial ^ INT_MIN
        n_trial = count_where(lambda k, pos: k >= cand)
        take = n_trial >= float(topk)
        return jnp.where(take, trial, prefix), jnp.where(take, n_trial, n_sel)

    def search_bits(state):
        group, prefix, n_sel = state
        prefix, n_sel = lax.fori_loop(
            group * SEARCH_GROUP, (group + 1) * SEARCH_GROUP, search_bit, (prefix, n_sel))
        return group + 1, prefix, n_sel

    def search_open(state):
        group, _, n_sel = state
        return (group < 32 // SEARCH_GROUP) & (jnp.max(n_sel) > float(topk))

    n_admissible = (n_lo + jnp.maximum(limit - Q_BLOCK, 0)).astype(F32)
    _, prefix, n_sel = lax.while_loop(search_open, search_bits,
                                      (jnp.int32(0), jnp.zeros((1, nq), I32), n_admissible))
    thr = jnp.maximum(prefix ^ INT_MIN, INT_MIN + 1)

    def write_bias(selected):
        def body(c, carry):
            r0 = pl.multiple_of(c * COUNT_ROWS, COUNT_ROWS)
            pos = r0 + lax.broadcasted_iota(I32, (COUNT_ROWS, nq), 0)
            bias[pl.ds(r0, COUNT_ROWS), :] = jnp.where(
                selected(keys[pl.ds(r0, COUNT_ROWS), :], pos), 0.0, MASKED).astype(BF16)
            return carry
        lax.fori_loop(0, n_ct, body, 0)

    surplus = jnp.max(n_sel) > float(topk)

    @pl.when(jnp.logical_not(surplus))
    def _():
        write_bias(lambda k, pos: k >= thr)

    @pl.when(surplus)
    def _():
        keep = float(topk) - count_where(lambda k, pos: k > thr)
        position_bits = keys.shape[0].bit_length()

        def cut_bit(i, prefix):
            trial = prefix | lax.shift_left(jnp.int32(1), position_bits - 1 - i)
            admitted = count_where(lambda k, pos: (k == thr) & (pos < trial))
            return jnp.where(admitted < keep, trial, prefix)
        cut = lax.fori_loop(0, position_bits, cut_bit, jnp.zeros((1, nq), I32)) + 1
        write_bias(lambda k, pos: (k > thr) | ((k == thr) & (pos < cut)))

    gw = A_GROUP * hq
    m_scr[...] = jnp.full(m_scr.shape, MASKED, F32)
    l_scr[...] = jnp.zeros(l_scr.shape, F32)
    acc_scr[...] = jnp.zeros(acc_scr.shape, F32)

    bias[pl.ds(pl.multiple_of(n_kt * lb, lb), lb), :] = jnp.full((lb, nq), MASKED, BF16)

    def score_keys(kt, s_ref, g):
        r0 = pl.multiple_of(kt * lb, lb)
        k_aug = jnp.concatenate(
            [k_ref[pl.ds(r0, lb), g * A_HEAD_DIM:(g + 1) * A_HEAD_DIM], bias[pl.ds(r0, lb), :]], axis=1)
        s_ref[g] = lax.dot_general(k_aug, qrows[g * gw:(g + 1) * gw, :], (((1,), (1,)), ((), ())),
                                   preferred_element_type=F32)

    def softmax_pv(kt, s_ref, g):
        s = s_ref[g]
        m_old = m_scr[g]
        m_new = jnp.maximum(m_old, jnp.max(s, axis=0, keepdims=True))
        alpha = jnp.exp2(m_old - m_new)
        p = jnp.exp2(s - m_new)
        l_scr[g] = l_scr[g] * alpha + jnp.sum(p, axis=0, keepdims=True)
        v_blk = v_ref[pl.ds(pl.multiple_of(kt * lb, lb), lb), g * A_HEAD_DIM:(g + 1) * A_HEAD_DIM]
        acc_scr[g] = acc_scr[g] * alpha + lax.dot_general(
            v_blk, p.astype(BF16), (((0,), (0,)), ((), ())), preferred_element_type=F32)
        m_scr[g] = m_new

    for g in range(A_KV_HEADS):
        score_keys(0, s_a, g)

    def attend_pair(j, carry):
        kt = 2 * j
        for g in range(A_KV_HEADS):
            score_keys(kt + 1, s_b, g)
        for g in range(A_KV_HEADS):
            softmax_pv(kt, s_a, g)
        for g in range(A_KV_HEADS):
            score_keys(jnp.minimum(kt + 2, n_kt), s_a, g)
        for g in range(A_KV_HEADS):
            softmax_pv(kt + 1, s_b, g)
        return carry

    lax.fori_loop(0, (n_kt + 1) // 2, attend_pair, 0)
    for g in range(A_KV_HEADS):
        o_g = (acc_scr[g] / l_scr[g]).T
        for r in range(A_GROUP):
            h = g * A_GROUP + r
            cols = slice(h * A_HEAD_DIM, (h + 1) * A_HEAD_DIM)
            o_scr[:, cols] = o_g[r * hq:(r + 1) * hq, :] * sg_ref[pl.ds(q0, hq), cols]


def _attn_scratch(lpad, hq=Q_BLOCK):
    gw = A_GROUP * hq
    return [
        pltpu.VMEM((A_HEADS * hq, A_HEAD_DIM + Q_BLOCK), BF16),
        pltpu.VMEM((IDX_HEADS * hq, IDX_DIM), BF16),
        pltpu.VMEM((lpad, Q_BLOCK), I32),
        pltpu.VMEM((lpad, Q_BLOCK), BF16),
        pltpu.VMEM((A_KV_HEADS, KEY_BLOCK, gw), F32),
        pltpu.VMEM((A_KV_HEADS, KEY_BLOCK, gw), F32),
        pltpu.VMEM((A_KV_HEADS, 1, gw), F32),
        pltpu.VMEM((A_KV_HEADS, 1, gw), F32),
        pltpu.VMEM((A_KV_HEADS, A_HEAD_DIM, gw), F32),
        pltpu.VMEM((hq, A_WIDTH), F32),
    ]


def _query_specs(q_index):
    qspec = pl.BlockSpec((Q_BLOCK, A_WIDTH), lambda s: (q_index(s), 0))
    return [qspec, qspec, pl.BlockSpec((Q_BLOCK, V7X_LANES), lambda s: (q_index(s), 0)), qspec]


def _prompt_attn_call(qs, qi, wi, sg, ki_seq, k_seq, v_seq, *, s_len, meta_block, topk, casts=()):
    lpad = ki_seq.shape[0]
    n_qb = s_len // Q_BLOCK
    full = lambda a: pl.BlockSpec(a.shape, lambda s: (0, 0))
    cast_specs, cast_shapes = _cast_specs(casts, 1, n_qb)
    body = functools.partial(_prompt_attn_kernel, topk=topk, n_prompt_blocks=n_qb)
    return pl.pallas_call(
        _with_casts(body, 7, len(casts)),
        grid=(1 + meta_block,),
        in_specs=_query_specs(lambda s: jnp.where(s == 0, meta_block, s - 1))
        + [full(ki_seq), full(k_seq), full(v_seq)] + cast_specs,
        out_specs=[pl.BlockSpec((Q_BLOCK, A_WIDTH), lambda s: (jnp.where(s == 0, meta_block, s - 1), 0))]
        + cast_specs,
        out_shape=[jax.ShapeDtypeStruct(((meta_block + 1) * Q_BLOCK, A_WIDTH), BF16)] + cast_shapes,
        scratch_shapes=_attn_scratch(lpad),
        compiler_params=_params(("arbitrary",)),
        name="prompt_attention",
    )(qs, qi, wi, sg, ki_seq, k_seq, v_seq, *casts)


def _sample_attn_call(qs, qi, wi, sg, cache_ki, cache_k, cache_v, ki16, k16, v16, pool_out,
                      *, s_len, topk):
    ndb, past, _ = cache_ki.shape
    new = SEQ_TILE
    lpad = _round_up(past + new, KEY_BLOCK) + KEY_BLOCK
    cache_spec = lambda a: pl.BlockSpec((None,) + a.shape[1:], lambda s: (s, 0, 0))
    new_spec = lambda a: pl.BlockSpec((new, a.shape[1]), lambda s: (s_len // new + s, 0))
    return pl.pallas_call(
        functools.partial(_sample_attn_kernel, topk=topk),
        grid=(ndb,),
        in_specs=_query_specs(lambda s: s_len // Q_BLOCK + s // (Q_BLOCK // new))
        + [cache_spec(cache_ki), cache_spec(cache_k), cache_spec(cache_v),
           new_spec(ki16), new_spec(k16), new_spec(v16), pl.BlockSpec(memory_space=pl.ANY)],
        out_specs=pl.BlockSpec((new, A_WIDTH), lambda s: (s_len // new + s, 0)),
        out_shape=jax.ShapeDtypeStruct(pool_out.shape, BF16),
        input_output_aliases={10: 0},
        scratch_shapes=[pltpu.VMEM((lpad, IDX_DIM), BF16), pltpu.VMEM((lpad, A_KV_WIDTH), BF16),
                        pltpu.VMEM((lpad, A_KV_WIDTH), BF16)] + _attn_scratch(lpad, new),
        compiler_params=_params(("arbitrary",)),
        name="sample_attention",
    )(qs, qi, wi, sg, cache_ki, cache_k, cache_v, ki16, k16, v16, pool_out)


CONV_UNITS = Q_BLOCK // SEQ_TILE


def _stream_conv_step(n_prompt_steps, halo, ext, state_ref, cur_ref, unit_fn):
    step = pl.program_id(0)
    is_sample = step > n_prompt_steps

    @pl.when(step == 0)
    def _():
        ext[0:halo, :] = jnp.zeros((halo, ext.shape[1]), F32)

    ext[halo:halo + Q_BLOCK, :] = cur_ref[...]
    for u in range(CONV_UNITS):
        @pl.when(is_sample)
        def _():
            ext[u * SEQ_TILE:u * SEQ_TILE + halo, :] = state_ref[u]

        unit_fn(u)

    @pl.when(step == 0)
    def _():
        if halo > N_META:
            ext[0:halo - N_META, :] = jnp.zeros((halo - N_META, ext.shape[1]), F32)
        keep = min(halo, N_META)
        ext[halo - keep:halo, :] = ext[halo + N_META - keep:halo + N_META, :]

    @pl.when(step > 0)
    def _():
        ext[0:halo, :] = ext[Q_BLOCK:Q_BLOCK + halo, :]


def _causal_taps(ext, unit, w_ref, halo, width, cols):
    first = halo - (width - 1)
    x = ext[unit * SEQ_TILE:unit * SEQ_TILE + halo + SEQ_TILE, cols]
    rows = x.shape[0]
    rotated = {0: x}
    for j in range(width):
        r = (first + j) % V7X_SUBLANES
        if r not in rotated:
            rotated[r] = pltpu.roll(x, rows - r, axis=0)
    acc = None
    for j in range(width):
        r = (first + j) % V7X_SUBLANES
        base = first + j - r
        term = rotated[r][base:base + SEQ_TILE, :] * w_ref[j:j + 1, cols]
        acc = term if acc is None else acc + term
    return acc


def _bconv_kernel(u_ref, st_ref, sg_ref, w_ref, cb_ref, g_ref, b_ref, o_ref, ext, y_scr,
                  *, n_prompt_steps):
    width = u_ref.shape[1]

    def unit_fn(unit):
        rows = slice(unit * SEQ_TILE, (unit + 1) * SEQ_TILE)
        total = jnp.zeros((SEQ_TILE, V7X_LANES), F32)
        for c in range(width // V7X_LANES):
            cols = slice(c * V7X_LANES, (c + 1) * V7X_LANES)
            y = _causal_taps(ext, unit, w_ref, B_HALO, B_CONV, cols) + cb_ref[:, cols]
            y_scr[:, cols] = y
            total = total + y
        mean = jnp.sum(total, axis=-1, keepdims=True) / width
        sq = jnp.zeros((SEQ_TILE, V7X_LANES), F32)
        for c in range(width // V7X_LANES):
            cols = slice(c * V7X_LANES, (c + 1) * V7X_LANES)
            yc = y_scr[:, cols] - mean
            sq = sq + yc * yc
        inv = lax.rsqrt(jnp.sum(sq, axis=-1, keepdims=True) / width + EPS)
        for c in range(width // V7X_LANES):
            cols = slice(c * V7X_LANES, (c + 1) * V7X_LANES)
            yn = (y_scr[:, cols] - mean) * inv * g_ref[:, cols] + b_ref[:, cols]
            o_ref[rows, cols] = (_silu(yn) * sg_ref[rows, cols]).astype(o_ref.dtype)

    _stream_conv_step(n_prompt_steps, B_HALO, ext, st_ref, u_ref, unit_fn)


def _cconv_kernel(z_ref, st_ref, bgs_ref, w_ref, cb_ref, o_ref, ext, *, n_prompt_steps):
    width = z_ref.shape[1]

    def unit_fn(unit):
        rows = slice(unit * SEQ_TILE, (unit + 1) * SEQ_TILE)
        for c in range(width // V7X_LANES):
            cols = slice(c * V7X_LANES, (c + 1) * V7X_LANES)
            y = _causal_taps(ext, unit, w_ref, C_HALO, C_CONV, cols) + cb_ref[:, cols]
            o_ref[rows, cols] = (bgs_ref[rows, cols] * y).astype(o_ref.dtype)

    _stream_conv_step(n_prompt_steps, C_HALO, ext, st_ref, z_ref, unit_fn)


def _cast_specs(weights, first_step, n_steps):
    specs, shapes = [], []
    for w in weights:
        slab = w.shape[0] // n_steps
        assert w.shape[0] % n_steps == 0 and slab % (2 * V7X_SUBLANES) == 0
        specs.append(pl.BlockSpec((slab, w.shape[1]), lambda s: (jnp.clip(s - first_step, 0, n_steps - 1), 0)))
        shapes.append(jax.ShapeDtypeStruct(w.shape, BF16))
    return specs, shapes


def _with_casts(body, n_inputs, n_casts):
    def kernel_body(*refs):
        sources = refs[n_inputs:n_inputs + n_casts]
        results = refs[n_inputs + n_casts + 1:n_inputs + 2 * n_casts + 1]
        for src, dst in zip(sources, results):
            dst[...] = src[...].astype(dst.dtype)
        body(*refs[:n_inputs], refs[n_inputs + n_casts], *refs[n_inputs + 2 * n_casts + 1:])
    return kernel_body


def _stream_conv_call(body, x, state, gate, gate_col_block, vecs, *, halo, n_prompt, n_sample,
                      scratch, name, casts=()):
    rows, width = x.shape
    n_blocks = n_prompt + n_sample
    assert rows == (n_blocks + 1) * Q_BLOCK
    assert state.shape[0] == n_sample * CONV_UNITS
    state = state.reshape(n_sample, CONV_UNITS, halo, width)
    block = lambda s: jnp.where(s == 0, n_blocks, s - 1)
    row_spec = pl.BlockSpec((Q_BLOCK, width), lambda s: (block(s), 0))
    vec_specs = [pl.BlockSpec(v.shape, lambda s: (0, 0)) for v in vecs]
    cast_specs, cast_shapes = _cast_specs(casts, 1, n_prompt)
    return pl.pallas_call(
        _with_casts(functools.partial(body, n_prompt_steps=n_prompt), 3 + len(vecs), len(casts)),
        grid=(1 + n_blocks,),
        in_specs=[
            row_spec,
            pl.BlockSpec((None, CONV_UNITS, halo, width),
                         lambda s: (jnp.clip(s - 1 - n_prompt, 0, n_sample - 1), 0, 0, 0)),
            pl.BlockSpec((Q_BLOCK, width), lambda s: (block(s), gate_col_block)),
        ] + vec_specs + cast_specs,
        out_specs=[row_spec] + cast_specs,
        out_shape=[jax.ShapeDtypeStruct((rows, width), BF16)] + cast_shapes,
        scratch_shapes=[pltpu.VMEM((halo + Q_BLOCK, width), F32)] + scratch,
        compiler_params=_params(("arbitrary",)),
        name=name,
    )(x, state, gate, *vecs, *casts)


def _pad_rows_front(a, rows):
    return jnp.pad(a, ((0, 0), (rows - a.shape[1], 0), (0, 0)))


def kernel(x_prompt, x_sample, cache_a_k, cache_a_v, cache_a_kidx, state_b_conv, state_c_conv, meta,
           e_norm_pre, e_w_in, e_kidx_ln_g, e_kidx_ln_b, e_bconv_w, e_bconv_b, e_bln_g, e_bln_b,
           e_w_out, e_norm_post, o_norm_pre, o_w_in, o_conv_w, o_conv_b, o_w_out, o_norm_post):
    nb, s_len, d = x_prompt.shape
    ndb, nds, _ = x_sample.shape
    past = cache_a_k.shape[2]
    bw = d // 2
    assert nb == 1 and e_w_in.shape[0] == 1 and o_w_in.shape[0] == 1
    assert nds == SEQ_TILE and s_len % Q_BLOCK == 0 and (ndb * nds) % Q_BLOCK == 0
    assert s_len >= B_CONV - 1 and A_WIDTH % bw == 0 and bw % V7X_LANES == 0
    ns = ndb * nds
    meta_row = s_len + ns
    rows = meta_row + Q_BLOCK
    topk_p = min(TOPK_MAX, s_len // 4)
    topk_s = min(TOPK_MAX, (past + nds) // 4)

    sources = (x_prompt[0], x_sample.reshape(ns, d), meta)

    w_in = e_w_in[0].T.astype(BF16)
    widths = (A_WIDTH, A_KV_WIDTH, A_KV_WIDTH, IDX_Q_WIDTH, IDX_DIM, IDX_HEADS, bw, bw, A_WIDTH + bw)
    starts = [sum(widths[:i]) for i in range(len(widths))]
    w_q, w_k, w_v, w_qi, w_ki, w_wi, w_val, w_glu, w_gate = [
        _WeightView(w_in, 0, d, s, max(w, V7X_LANES), transposed=True) for s, w in zip(starts, widths)]

    h0, = _pool_rowwise_call(_prenorm_kernel, *sources, [], [e_norm_pre[0]], [BF16], "even_prenorm")
    q_s, q_i = _matmul_call([h0], [w_q, w_qi], [(0, 0), (0, 1)], _ep_q_qi, [BF16, BF16], bn=512,
                            name="even_q_qidx")
    k32, v32, k16, v16 = _kv_call(h0, w_k, w_v, "even_kv")
    ki32, ki16, wi = _matmul_call([h0], [w_ki, w_wi], [(0, 0), (0, 1)], _ep_kidx, [F32, BF16, F32],
                                  bn=V7X_LANES, auxs=[e_kidx_ln_g[0], e_kidx_ln_b[0]], name="even_kidx")
    u, = _matmul_call([h0], [w_val, w_glu], [(0, 0), (0, 1)], _ep_glu, [F32], bn=512, name="even_glu")
    sg, = _matmul_call([h0], [w_gate], [(0, 0)], _ep_silu, [F32], bn=1024, name="even_gate")

    lpad_p = _round_up(Q_BLOCK + s_len, KEY_BLOCK) + KEY_BLOCK
    seq = lambda a: jnp.concatenate(
        [a[meta_row:meta_row + Q_BLOCK], a[:s_len],
         jnp.zeros((lpad_p - Q_BLOCK - s_len, a.shape[1]), a.dtype)], axis=0)
    oa, w_out, w_out_odd = _prompt_attn_call(
        q_s, q_i, wi, sg, seq(ki16), seq(k16), seq(v16), s_len=s_len,
        meta_block=meta_row // Q_BLOCK, topk=topk_p, casts=(e_w_out[0], o_w_out[0]))
    w_out_a = _WeightView(w_out, 0, A_WIDTH, 0, d)
    w_out_b = _WeightView(w_out, A_WIDTH, bw, 0, d)
    oa = _sample_attn_call(q_s, q_i, wi, sg, cache_a_kidx[0],
                           cache_a_k[0].reshape(ndb, past * A_KV_HEADS, A_HEAD_DIM),
                           cache_a_v[0].reshape(ndb, past * A_KV_HEADS, A_HEAD_DIM), ki16, k16, v16, oa,
                           s_len=s_len, topk=topk_s)

    conv_steps = dict(n_prompt=s_len // Q_BLOCK, n_sample=ns // Q_BLOCK)
    bconv_w = jnp.pad(e_bconv_w[0], ((0, B_HALO - B_CONV), (0, 0)))
    ob, w_odd = _stream_conv_call(
        _bconv_kernel, u, _pad_rows_front(state_b_conv[0], B_HALO), sg, A_WIDTH // bw,
        [bconv_w, e_bconv_b[0].reshape(1, bw), e_bln_g[0].reshape(1, bw), e_bln_b[0].reshape(1, bw)],
        halo=B_HALO, scratch=[pltpu.VMEM((SEQ_TILE, bw), F32)], name="conformer_conv",
        casts=(o_w_in[0],), **conv_steps)

    y0, = _matmul_call([oa, ob], [w_out_a, w_out_b], [(0, 0), (1, 1)], _ep_sum, [BF16],
                       bn=1024, name="even_out")
    x1, h1 = _pool_rowwise_call(_postnorm_prenorm_kernel, *sources, [y0],
                                [e_norm_post[0], o_norm_pre[0]], [F32, BF16], "even_postnorm_odd_prenorm")

    w_hx, w_cg, w_bg, w_og = [_WeightView(w_odd, 0, d, i * d, d) for i in range(4)]
    z, bgs = _matmul_call([h1], [w_hx, w_cg, w_bg, w_og], [(0, 0), (0, 1), (0, 2), (0, 3)], _ep_odd,
                          [F32, BF16], bn=256, name="odd_in")
    cconv_w = jnp.pad(o_conv_w[0], ((0, C_HALO - C_CONV), (0, 0)))
    o1, = _stream_conv_call(
        _cconv_kernel, z, _pad_rows_front(state_c_conv[0], C_HALO), bgs, 0,
        [cconv_w, o_conv_b[0].reshape(1, d)],
        halo=C_HALO, scratch=[], name="short_conv", **conv_steps)
    y1, = _matmul_call([o1], [_WeightView(w_out_odd, 0, d, 0, d)], [(0, 0)],
                       _ep_identity, [BF16], bn=1024, name="odd_out")
    y_prompt = _final_postnorm_call(x1, y1, o_norm_post[0], 0, s_len // Q_BLOCK, "odd_postnorm_prompt")
    y_sample = _final_postnorm_call(x1, y1, o_norm_post[0], s_len // Q_BLOCK, ns // Q_BLOCK,
                                    "odd_postnorm_sample")

    def prompt_rows(a, per_frame=1):
        return jnp.concatenate([a[meta_row * per_frame:(meta_row + N_META) * per_frame],
                                a[:s_len * per_frame]], axis=0)

    def sample_rows(a, per_frame=1):
        return a[s_len * per_frame:meta_row * per_frame].reshape(ndb, nds * per_frame, a.shape[1])

    kv_shape = (A_KV_HEADS, A_HEAD_DIM)
    y_prompt = y_prompt[None]
    y_sample = y_sample.reshape(ndb, nds, d)
    prompt_a_k = prompt_rows(k32, A_KV_HEADS).reshape(1, 1, N_META + s_len, *kv_shape)
    prompt_a_v = prompt_rows(v32, A_KV_HEADS).reshape(1, 1, N_META + s_len, *kv_shape)
    prompt_a_kidx = prompt_rows(ki32)[None, None]
    prompt_b_conv = u[s_len - (B_CONV - 1):s_len][None, None]
    prompt_c_conv = z[s_len - (C_CONV - 1):s_len][None, None]
    sample_a_k = sample_rows(k32, A_KV_HEADS).reshape(1, ndb, nds, *kv_shape)
    sample_a_v = sample_rows(v32, A_KV_HEADS).reshape(1, ndb, nds, *kv_shape)
    sample_a_kidx = sample_rows(ki32)[None]
    sample_b_conv = sample_rows(u)[None, :, nds - (B_CONV - 1):]
    sample_c_conv = sample_rows(z)[None, :, nds - (C_CONV - 1):]
    return (y_prompt, y_sample, prompt_a_k, prompt_a_v, prompt_a_kidx, prompt_b_conv, prompt_c_conv,
            sample_a_k, sample_a_v, sample_a_kidx, sample_b_conv, sample_c_conv)
```

```python
import functools
from typing import NamedTuple

import jax
import jax.numpy as jnp
from jax import lax
from jax.experimental import pallas as pl
from jax.experimental.pallas import tpu as pltpu

CHUNK = 64
N_META = 16
EPS = 1e-6
A_HEADS = 16
A_KV_HEADS = 4
A_HEAD_DIM = 128
A_WIDTH = A_HEADS * A_HEAD_DIM
A_KV_WIDTH = A_KV_HEADS * A_HEAD_DIM
A_GROUP = A_HEADS // A_KV_HEADS
IDX_HEADS = 16
IDX_DIM = 128
IDX_Q_WIDTH = IDX_HEADS * IDX_DIM
INDEX_SCALE = (IDX_HEADS * IDX_DIM) ** -0.5
TOPK_MAX = 256
B_CONV = 31
C_CONV = 3

V7X_LANES = 128
V7X_SUBLANES = 8
V7X_VMEM_LIMIT_BYTES = 56 * 1024 * 1024

Q_BLOCK = 128
KEY_BLOCK = 512
COUNT_ROWS = 512
SEARCH_GROUP = 4
SEQ_TILE = 64
B_HALO = 32
C_HALO = 8
INT_MIN = -2 ** 31
MASKED = -1e30
LOG2_E = 1.4426950408889634

F32 = jnp.float32
BF16 = jnp.bfloat16
I32 = jnp.int32


def _round_up(x, m):
    return (x + m - 1) // m * m


def _row_tile(rows, cap):
    best = 16
    for t in range(16, cap + 1, 16):
        if rows % t == 0:
            best = t
    return best


def _params(sem):
    return pltpu.CompilerParams(dimension_semantics=sem, vmem_limit_bytes=V7X_VMEM_LIMIT_BYTES)


def _rms(x, g):
    return x * lax.rsqrt(jnp.mean(x * x, axis=-1, keepdims=True) + EPS) * g


def _gather_pool_block(xp_ref, xs_ref, meta_ref, x_scr, n_prompt, n_sample):
    i = pl.program_id(0)

    @pl.when(i < n_prompt)
    def _():
        x_scr[...] = xp_ref[...]

    @pl.when((i >= n_prompt) & (i < n_prompt + n_sample))
    def _():
        x_scr[...] = xs_ref[...]

    @pl.when(i == n_prompt + n_sample)
    def _():
        x_scr[0:N_META, :] = meta_ref[...]
        x_scr[N_META:, :] = jnp.zeros((Q_BLOCK - N_META, x_scr.shape[1]), F32)


def _prenorm_kernel(xp_ref, xs_ref, meta_ref, g_ref, h_ref, x_scr, *, n_prompt, n_sample):
    _gather_pool_block(xp_ref, xs_ref, meta_ref, x_scr, n_prompt, n_sample)
    h_ref[...] = _rms(x_scr[...], g_ref[...]).astype(h_ref.dtype)


def _postnorm_prenorm_kernel(xp_ref, xs_ref, meta_ref, y_ref, gpost_ref, gpre_ref, xo_ref, h_ref, x_scr,
                             *, n_prompt, n_sample):
    _gather_pool_block(xp_ref, xs_ref, meta_ref, x_scr, n_prompt, n_sample)
    x1 = x_scr[...] + _rms(y_ref[...].astype(F32), gpost_ref[...])
    xo_ref[...] = x1
    h_ref[...] = _rms(x1, gpre_ref[...]).astype(h_ref.dtype)


def _postnorm_kernel(x_ref, y_ref, gpost_ref, xo_ref):
    xo_ref[...] = x_ref[...] + _rms(y_ref[...].astype(F32), gpost_ref[...])


def _pool_rowwise_call(body, x_prompt, x_sample, meta, pool_inputs, vec_inputs, out_dtypes, name):
    s_len, d = x_prompt.shape
    n_prompt, n_sample = s_len // Q_BLOCK, x_sample.shape[0] // Q_BLOCK
    n_blocks = n_prompt + n_sample + 1
    pool_spec = pl.BlockSpec((Q_BLOCK, d), lambda i: (i, 0))
    vec_spec = pl.BlockSpec((1, d), lambda i: (0, 0))
    return pl.pallas_call(
        functools.partial(body, n_prompt=n_prompt, n_sample=n_sample),
        grid=(n_blocks,),
        in_specs=[
            pl.BlockSpec((Q_BLOCK, d), lambda i: (jnp.minimum(i, n_prompt - 1), 0)),
            pl.BlockSpec((Q_BLOCK, d), lambda i: (jnp.clip(i - n_prompt, 0, n_sample - 1), 0)),
            pl.BlockSpec((N_META, d), lambda i: (0, 0)),
        ] + [pool_spec] * len(pool_inputs) + [vec_spec] * len(vec_inputs),
        out_specs=[pool_spec] * len(out_dtypes),
        out_shape=[jax.ShapeDtypeStruct((n_blocks * Q_BLOCK, d), dt) for dt in out_dtypes],
        scratch_shapes=[pltpu.VMEM((Q_BLOCK, d), F32)],
        compiler_params=_params(("arbitrary",)),
        name=name,
    )(x_prompt, x_sample, meta, *pool_inputs, *[v.reshape(1, d) for v in vec_inputs])


def _final_postnorm_call(x1, y1, g, first_block, n_blocks, name):
    d = x1.shape[1]
    per_step = max(p for p in (1, 2, 4) if first_block % p == 0 and n_blocks % p == 0)
    bm, first, steps = per_step * Q_BLOCK, first_block // per_step, n_blocks // per_step
    in_spec = pl.BlockSpec((bm, d), lambda i: (i + first, 0))
    return pl.pallas_call(
        _postnorm_kernel,
        grid=(steps,),
        in_specs=[in_spec, in_spec, pl.BlockSpec((1, d), lambda i: (0, 0))],
        out_specs=pl.BlockSpec((bm, d), lambda i: (i, 0)),
        out_shape=jax.ShapeDtypeStruct((n_blocks * Q_BLOCK, d), F32),
        compiler_params=_params(("parallel",)),
        name=name,
    )(x1, y1, g.reshape(1, d))


def _matmul_kernel(*refs, n_x, n_w, n_aux, terms, epilogue, transposed):
    xs = refs[:n_x]
    ws = refs[n_x:n_x + n_w]
    auxs = refs[n_x + n_w:n_x + n_w + n_aux]
    outs = refs[n_x + n_w + n_aux:]
    accs = [lax.dot_general(xs[a][...], ws[b][...],
                            (((1,), (1 if transposed[b] else 0,)), ((), ())),
                            preferred_element_type=F32) for a, b in terms]
    results = epilogue(*accs, *[r[...] for r in auxs])
    for o_ref, r in zip(outs, results):
        o_ref[...] = r.astype(o_ref.dtype)


class _WeightView(NamedTuple):
    array: jax.Array
    row0: int
    rows: int
    col0: int
    cols: int
    transposed: bool = False


def _weight_spec(w, bn):
    if w.transposed:
        return pl.BlockSpec(
            (pl.Element(bn), pl.Element(w.rows)),
            lambda i, j: (pl.multiple_of(w.col0 + j * bn, 2 * V7X_SUBLANES), w.row0))
    return pl.BlockSpec((w.rows, bn), lambda i, j: (w.row0 // w.rows, w.col0 // bn + j))


def _matmul_call(xs, ws, terms, epilogue, out_dtypes, *, bn, auxs=(), name):
    rows = xs[0].shape[0]
    n = ws[0].cols
    bm = _row_tile(rows, 1536)
    bn = max(t for t in range(V7X_LANES, min(bn, n) + 1, V7X_LANES)
             if n % t == 0 and all(w.transposed or w.col0 % t == 0 for w in ws))
    assert all(w.cols == n and w.row0 % w.rows == 0 for w in ws)
    assert all(w.col0 % (2 * V7X_SUBLANES) == 0 for w in ws if w.transposed)
    in_specs = [pl.BlockSpec((bm, x.shape[1]), lambda i, j: (i, 0)) for x in xs]
    in_specs += [_weight_spec(w, bn) for w in ws]
    in_specs += [pl.BlockSpec((1, bn), lambda i, j: (0, j)) for _ in auxs]
    out_spec = pl.BlockSpec((bm, bn), lambda i, j: (i, j))
    body = functools.partial(_matmul_kernel, n_x=len(xs), n_w=len(ws), n_aux=len(auxs),
                             terms=terms, epilogue=epilogue,
                             transposed=tuple(w.transposed for w in ws))
    return pl.pallas_call(
        body,
        grid=(rows // bm, n // bn),
        in_specs=in_specs,
        out_specs=[out_spec] * len(out_dtypes),
        out_shape=[jax.ShapeDtypeStruct((rows, n), dt) for dt in out_dtypes],
        compiler_params=_params(("parallel", "arbitrary")),
        name=name,
    )(*xs, *[w.array for w in ws], *[a.reshape(1, n) for a in auxs])


def _kv_kernel(x_ref, wk_ref, wv_ref, k32_ref, v32_ref, k16_ref, v16_ref):
    bm = x_ref.shape[0]
    for w_ref, o32_ref, o16_ref in ((wk_ref, k32_ref, k16_ref), (wv_ref, v32_ref, v16_ref)):
        acc = lax.dot_general(x_ref[...], w_ref[...], (((1,), (1,)), ((), ())),
                              preferred_element_type=F32)
        o16_ref[...] = acc.astype(o16_ref.dtype)
        for g in range(A_KV_HEADS):
            o32_ref[pl.ds(g, bm, stride=A_KV_HEADS), :] = acc[:, g * A_HEAD_DIM:(g + 1) * A_HEAD_DIM]


def _kv_call(x, w_k, w_v, name):
    rows = x.shape[0]
    bm = _row_tile(rows, 1536)
    assert w_k.transposed and w_v.transposed and w_k.cols == A_KV_WIDTH == w_v.cols
    head_rows = pl.BlockSpec((bm * A_KV_HEADS, A_HEAD_DIM), lambda i, j: (i, 0))
    wide = pl.BlockSpec((bm, A_KV_WIDTH), lambda i, j: (i, 0))
    return pl.pallas_call(
        _kv_kernel,
        grid=(rows // bm, 1),
        in_specs=[pl.BlockSpec((bm, x.shape[1]), lambda i, j: (i, 0)),
                  _weight_spec(w_k, A_KV_WIDTH), _weight_spec(w_v, A_KV_WIDTH)],
        out_specs=[head_rows, head_rows, wide, wide],
        out_shape=[jax.ShapeDtypeStruct((rows * A_KV_HEADS, A_HEAD_DIM), F32)] * 2
        + [jax.ShapeDtypeStruct((rows, A_KV_WIDTH), BF16)] * 2,
        compiler_params=_params(("parallel", "arbitrary")),
        name=name,
    )(x, w_k.array, w_v.array)


def _silu(x):
    return x * jax.nn.sigmoid(x)


def _ep_q_qi(q, qi):
    return q * (A_HEAD_DIM ** -0.5 * LOG2_E), qi


def _ep_kidx(ki, wi, g, b):
    kc = ki - jnp.mean(ki, axis=-1, keepdims=True)
    kn = kc * lax.rsqrt(jnp.mean(kc * kc, axis=-1, keepdims=True) + EPS) * g + b
    return kn, kn, wi * INDEX_SCALE


def _ep_glu(val, glu):
    return (val * jax.nn.sigmoid(glu),)


def _ep_silu(gate):
    return (_silu(gate),)


def _ep_sum(a, b):
    return (a + b,)


def _ep_identity(a):
    return (a,)


def _ep_odd(hx, cg, bg, gate):
    return cg * hx, bg * _silu(gate)


def _prompt_attn_kernel(qs_ref, qi_ref, wi_ref, sg_ref, ki_ref, k_ref, v_ref, o_ref, *scratch,
                        topk, n_prompt_blocks):
    step = pl.program_id(0)

    @pl.when(step <= n_prompt_blocks)
    def _():
        frame = (step - 1) * Q_BLOCK + lax.broadcasted_iota(I32, (1, Q_BLOCK), 1)
        limit = jnp.where(step == 0, 0, Q_BLOCK + CHUNK * (frame // CHUNK + 1))
        n_kt = jnp.where(step == 0, 1, pl.cdiv(Q_BLOCK * (step + 1), KEY_BLOCK))
        _attend(qs_ref, qi_ref, wi_ref, sg_ref, ki_ref, k_ref, v_ref, *scratch,
                topk=topk, n_lo=N_META, limit=limit, n_kt=n_kt)
        o_ref[...] = scratch[-1][...].astype(o_ref.dtype)

    @pl.when(step > n_prompt_blocks)
    def _():
        o_ref[...] = jnp.zeros(o_ref.shape, o_ref.dtype)


def _sample_attn_kernel(qs_ref, qi_ref, wi_ref, sg_ref, cki_ref, ck_ref, cv_ref, nki_ref, nk_ref, nv_ref,
                        pool_ref, o_ref, ki_buf, k_buf, v_buf, *scratch, topk):
    del pool_ref
    step = pl.program_id(0)
    past, new = cki_ref.shape[0], nk_ref.shape[0]
    for buf, cache_ref, new_ref in ((ki_buf, cki_ref, nki_ref), (k_buf, ck_ref, nk_ref), (v_buf, cv_ref, nv_ref)):
        heads = cache_ref.shape[0] // past
        for r in range(0, past, KEY_BLOCK):
            n = min(KEY_BLOCK, past - r)
            for g in range(heads):
                buf[r:r + n, g * A_HEAD_DIM:(g + 1) * A_HEAD_DIM] = cache_ref[
                    pl.ds(r * heads + g, n, stride=heads), :].astype(BF16)
        buf[past:past + new, :] = new_ref[...]
        buf[past + new:, :] = jnp.zeros((buf.shape[0] - past - new, buf.shape[1]), BF16)
    q0 = pl.multiple_of((step % (Q_BLOCK // new)) * new, new)
    _attend(qs_ref, qi_ref, wi_ref, sg_ref, ki_buf, k_buf, v_buf, *scratch,
            topk=topk, n_lo=past + new, limit=jnp.zeros((1, Q_BLOCK), I32),
            n_kt=ki_buf.shape[0] // KEY_BLOCK - 1, q0=q0, hq=new)
    o_ref[...] = scratch[-1][...].astype(o_ref.dtype)


def _attend(qs_ref, qi_ref, wi_ref, sg_ref, ki_ref, k_ref, v_ref,
            qrows, qirows, keys, bias, s_a, s_b, m_scr, l_scr, acc_scr, o_scr,
            *, topk, n_lo, limit, n_kt, q0=0, hq=Q_BLOCK):
    nq = Q_BLOCK
    lb = KEY_BLOCK
    fold = nq // hq
    assert fold in (1, 2)

    @pl.when(pl.program_id(0) == 0)
    def _():
        one_hot = (lax.broadcasted_iota(I32, (hq, nq), 0) == lax.broadcasted_iota(I32, (hq, nq), 1))
        for h in range(A_HEADS):
            qrows[h * hq:(h + 1) * hq, A_HEAD_DIM:A_HEAD_DIM + nq] = one_hot.astype(BF16)

    for h in range(A_HEADS):
        qrows[h * hq:(h + 1) * hq, 0:A_HEAD_DIM] = qs_ref[pl.ds(q0, hq), h * A_HEAD_DIM:(h + 1) * A_HEAD_DIM]
    for h in range(IDX_HEADS):
        qirows[h * hq:(h + 1) * hq, :] = qi_ref[pl.ds(q0, hq), h * IDX_DIM:(h + 1) * IDX_DIM]
    w_t = jnp.concatenate([wi_ref[pl.ds(q0, hq), :]] * fold, axis=0).T
    lane_head = lax.broadcasted_iota(I32, (1, nq), 1) // hq
    w_tiles = []
    for j in range(IDX_HEADS // fold):
        w_tile = w_t[j * fold:j * fold + 1, :]
        for f in range(1, fold):
            w_tile = jnp.where(lane_head == f, w_t[j * fold + f:j * fold + f + 1, :], w_tile)
        w_tiles.append(w_tile)

    ib = 2 * lb

    def score_block(kt, carry):
        r0 = pl.multiple_of(kt * ib, ib)
        d = lax.dot_general(ki_ref[pl.ds(r0, ib), :], qirows[...], (((1,), (1,)), ((), ())),
                            preferred_element_type=F32)
        sc = jnp.zeros((ib, nq), F32)
        for j, w_tile in enumerate(w_tiles):
            sc = sc + jnp.maximum(d[:, j * nq:(j + 1) * nq], 0.0) * w_tile
        for f in range(1, fold):
            sc = sc + pltpu.roll(sc, f * hq, axis=1)
        pos = r0 + lax.broadcasted_iota(I32, (ib, nq), 0)
        adm = (pos < n_lo) | ((pos >= Q_BLOCK) & (pos < limit))
        bits = lax.bitcast_convert_type(sc, I32)
        key = bits ^ ((bits >> 31) & 0x7FFFFFFF)
        keys[pl.ds(r0, ib), :] = jnp.where(adm, key, INT_MIN)
        return carry

    lax.fori_loop(0, (n_kt + 1) // 2, score_block, 0)

    n_ct = n_kt * (lb // COUNT_ROWS)

    def count_where(pred):
        def body(c, acc):
            r0 = pl.multiple_of(c * COUNT_ROWS, COUNT_ROWS)
            pos = r0 + lax.broadcasted_iota(I32, (COUNT_ROWS, nq), 0)
            hit = jnp.where(pred(keys[pl.ds(r0, COUNT_ROWS), :], pos), 1.0, 0.0)
            parts = [hit[i * V7X_SUBLANES:(i + 1) * V7X_SUBLANES, :]
                     for i in range(COUNT_ROWS // V7X_SUBLANES)]
            while len(parts) > 1:
                parts = [a + b for a, b in zip(parts[::2], parts[1::2])]
            return acc + parts[0]
        acc = lax.fori_loop(0, n_ct, body, jnp.zeros((V7X_SUBLANES, nq), F32))
        return jnp.sum(acc, axis=0, keepdims=True)

    def search_bit(i, state):
        prefix, n_sel = state
        trial = prefix | lax.shift_left(jnp.int32(1), 31 - i)
        cand = trial ^ INT_MIN
        n_trial = count_where(lambda k, pos: k >= cand)
        take = n_trial >= float(topk)
        return jnp.where(take, trial, prefix), jnp.where(take, n_trial, n_sel)

    def search_bits(state):
        group, prefix, n_sel = state
        prefix, n_sel = lax.fori_loop(
            group * SEARCH_GROUP, (group + 1) * SEARCH_GROUP, search_bit, (prefix, n_sel))
        return group + 1, prefix, n_sel

    def search_open(state):
        group, _, n_sel = state
        return (group < 32 // SEARCH_GROUP) & (jnp.max(n_sel) > float(topk))

    n_admissible = (n_lo + jnp.maximum(limit - Q_BLOCK, 0)).astype(F32)
    _, prefix, n_sel = lax.while_loop(search_open, search_bits,
                                      (jnp.int32(0), jnp.zeros((1, nq), I32), n_admissible))
    thr = jnp.maximum(prefix ^ INT_MIN, INT_MIN + 1)

    def write_bias(selected):
        def body(c, carry):
            r0 = pl.multiple_of(c * COUNT_ROWS, COUNT_ROWS)
            pos = r0 + lax.broadcasted_iota(I32, (COUNT_ROWS, nq), 0)
            bias[pl.ds(r0, COUNT_ROWS), :] = jnp.where(
                selected(keys[pl.ds(r0, COUNT_ROWS), :], pos), 0.0, MASKED).astype(BF16)
            return carry
        lax.fori_loop(0, n_ct, body, 0)

    surplus = jnp.max(n_sel) > float(topk)

    @pl.when(jnp.logical_not(surplus))
    def _():
        write_bias(lambda k, pos: k >= thr)

    @pl.when(surplus)
    def _():
        keep = float(topk) - count_where(lambda k, pos: k > thr)
        position_bits = keys.shape[0].bit_length()

        def cut_bit(i, prefix):
            trial = prefix | lax.shift_left(jnp.int32(1), position_bits - 1 - i)
            admitted = count_where(lambda k, pos: (k == thr) & (pos < trial))
            return jnp.where(admitted < keep, trial, prefix)
        cut = lax.fori_loop(0, position_bits, cut_bit, jnp.zeros((1, nq), I32)) + 1
        write_bias(lambda k, pos: (k > thr) | ((k == thr) & (pos < cut)))

    gw = A_GROUP * hq
    m_scr[...] = jnp.full(m_scr.shape, MASKED, F32)
    l_scr[...] = jnp.zeros(l_scr.shape, F32)
    acc_scr[...] = jnp.zeros(acc_scr.shape, F32)

    bias[pl.ds(pl.multiple_of(n_kt * lb, lb), lb), :] = jnp.full((lb, nq), MASKED, BF16)

    def score_keys(kt, s_ref, g):
        r0 = pl.multiple_of(kt * lb, lb)
        k_aug = jnp.concatenate(
            [k_ref[pl.ds(r0, lb), g * A_HEAD_DIM:(g + 1) * A_HEAD_DIM], bias[pl.ds(r0, lb), :]], axis=1)
        s_ref[g] = lax.dot_general(k_aug, qrows[g * gw:(g + 1) * gw, :], (((1,), (1,)), ((), ())),
                                   preferred_element_type=F32)

    def softmax_pv(kt, s_ref, g):
        s = s_ref[g]
        m_old = m_scr[g]
        m_new = jnp.maximum(m_old, jnp.max(s, axis=0, keepdims=True))
        alpha = jnp.exp2(m_old - m_new)
        p = jnp.exp2(s - m_new)
        l_scr[g] = l_scr[g] * alpha + jnp.sum(p, axis=0, keepdims=True)
        v_blk = v_ref[pl.ds(pl.multiple_of(kt * lb, lb), lb), g * A_HEAD_DIM:(g + 1) * A_HEAD_DIM]
        acc_scr[g] = acc_scr[g] * alpha + lax.dot_general(
            v_blk, p.astype(BF16), (((0,), (0,)), ((), ())), preferred_element_type=F32)
        m_scr[g] = m_new

    for g in range(A_KV_HEADS):
        score_keys(0, s_a, g)

    def attend_pair(j, carry):
        kt = 2 * j
        for g in range(A_KV_HEADS):
            score_keys(kt + 1, s_b, g)
        for g in range(A_KV_HEADS):
            softmax_pv(kt, s_a, g)
        for g in range(A_KV_HEADS):
            score_keys(jnp.minimum(kt + 2, n_kt), s_a, g)
        for g in range(A_KV_HEADS):
            softmax_pv(kt + 1, s_b, g)
        return carry

    lax.fori_loop(0, (n_kt + 1) // 2, attend_pair, 0)
    for g in range(A_KV_HEADS):
        o_g = (acc_scr[g] / l_scr[g]).T
        for r in range(A_GROUP):
            h = g * A_GROUP + r
            cols = slice(h * A_HEAD_DIM, (h + 1) * A_HEAD_DIM)
            o_scr[:, cols] = o_g[r * hq:(r + 1) * hq, :] * sg_ref[pl.ds(q0, hq), cols]


def _attn_scratch(lpad, hq=Q_BLOCK):
    gw = A_GROUP * hq
    return [
        pltpu.VMEM((A_HEADS * hq, A_HEAD_DIM + Q_BLOCK), BF16),
        pltpu.VMEM((IDX_HEADS * hq, IDX_DIM), BF16),
        pltpu.VMEM((lpad, Q_BLOCK), I32),
        pltpu.VMEM((lpad, Q_BLOCK), BF16),
        pltpu.VMEM((A_KV_HEADS, KEY_BLOCK, gw), F32),
        pltpu.VMEM((A_KV_HEADS, KEY_BLOCK, gw), F32),
        pltpu.VMEM((A_KV_HEADS, 1, gw), F32),
        pltpu.VMEM((A_KV_HEADS, 1, gw), F32),
        pltpu.VMEM((A_KV_HEADS, A_HEAD_DIM, gw), F32),
        pltpu.VMEM((hq, A_WIDTH), F32),
    ]


def _query_specs(q_index):
    qspec = pl.BlockSpec((Q_BLOCK, A_WIDTH), lambda s: (q_index(s), 0))
    return [qspec, qspec, pl.BlockSpec((Q_BLOCK, V7X_LANES), lambda s: (q_index(s), 0)), qspec]


def _prompt_attn_call(qs, qi, wi, sg, ki_seq, k_seq, v_seq, *, s_len, meta_block, topk, casts=()):
    lpad = ki_seq.shape[0]
    n_qb = s_len // Q_BLOCK
    full = lambda a: pl.BlockSpec(a.shape, lambda s: (0, 0))
    cast_specs, cast_shapes = _cast_specs(casts, 1, n_qb)
    body = functools.partial(_prompt_attn_kernel, topk=topk, n_prompt_blocks=n_qb)
    return pl.pallas_call(
        _with_casts(body, 7, len(casts)),
        grid=(1 + meta_block,),
        in_specs=_query_specs(lambda s: jnp.where(s == 0, meta_block, s - 1))
        + [full(ki_seq), full(k_seq), full(v_seq)] + cast_specs,
        out_specs=[pl.BlockSpec((Q_BLOCK, A_WIDTH), lambda s: (jnp.where(s == 0, meta_block, s - 1), 0))]
        + cast_specs,
        out_shape=[jax.ShapeDtypeStruct(((meta_block + 1) * Q_BLOCK, A_WIDTH), BF16)] + cast_shapes,
        scratch_shapes=_attn_scratch(lpad),
        compiler_params=_params(("arbitrary",)),
        name="prompt_attention",
    )(qs, qi, wi, sg, ki_seq, k_seq, v_seq, *casts)


def _sample_attn_call(qs, qi, wi, sg, cache_ki, cache_k, cache_v, ki16, k16, v16, pool_out,
                      *, s_len, topk):
    ndb, past, _ = cache_ki.shape
    new = SEQ_TILE
    lpad = _round_up(past + new, KEY_BLOCK) + KEY_BLOCK
    cache_spec = lambda a: pl.BlockSpec((None,) + a.shape[1:], lambda s: (s, 0, 0))
    new_spec = lambda a: pl.BlockSpec((new, a.shape[1]), lambda s: (s_len // new + s, 0))
    return pl.pallas_call(
        functools.partial(_sample_attn_kernel, topk=topk),
        grid=(ndb,),
        in_specs=_query_specs(lambda s: s_len // Q_BLOCK + s // (Q_BLOCK // new))
        + [cache_spec(cache_ki), cache_spec(cache_k), cache_spec(cache_v),
           new_spec(ki16), new_spec(k16), new_spec(v16), pl.BlockSpec(memory_space=pl.ANY)],
        out_specs=pl.BlockSpec((new, A_WIDTH), lambda s: (s_len // new + s, 0)),
        out_shape=jax.ShapeDtypeStruct(pool_out.shape, BF16),
        input_output_aliases={10: 0},
        scratch_shapes=[pltpu.VMEM((lpad, IDX_DIM), BF16), pltpu.VMEM((lpad, A_KV_WIDTH), BF16),
                        pltpu.VMEM((lpad, A_KV_WIDTH), BF16)] + _attn_scratch(lpad, new),
        compiler_params=_params(("arbitrary",)),
        name="sample_attention",
    )(qs, qi, wi, sg, cache_ki, cache_k, cache_v, ki16, k16, v16, pool_out)


CONV_UNITS = Q_BLOCK // SEQ_TILE


def _stream_conv_step(n_prompt_steps, halo, ext, state_ref, cur_ref, unit_fn):
    step = pl.program_id(0)
    is_sample = step > n_prompt_steps

    @pl.when(step == 0)
    def _():
        ext[0:halo, :] = jnp.zeros((halo, ext.shape[1]), F32)

    ext[halo:halo + Q_BLOCK, :] = cur_ref[...]
    for u in range(CONV_UNITS):
        @pl.when(is_sample)
        def _():
            ext[u * SEQ_TILE:u * SEQ_TILE + halo, :] = state_ref[u]

        unit_fn(u)

    @pl.when(step == 0)
    def _():
        if halo > N_META:
            ext[0:halo - N_META, :] = jnp.zeros((halo - N_META, ext.shape[1]), F32)
        keep = min(halo, N_META)
        ext[halo - keep:halo, :] = ext[halo + N_META - keep:halo + N_META, :]

    @pl.when(step > 0)
    def _():
        ext[0:halo, :] = ext[Q_BLOCK:Q_BLOCK + halo, :]


def _causal_taps(ext, unit, w_ref, halo, width, cols):
    first = halo - (width - 1)
    x = ext[unit * SEQ_TILE:unit * SEQ_TILE + halo + SEQ_TILE, cols]
    rows = x.shape[0]
    rotated = {0: x}
    for j in range(width):
        r = (first + j) % V7X_SUBLANES
        if r not in rotated:
            rotated[r] = pltpu.roll(x, rows - r, axis=0)
    acc = None
    for j in range(width):
        r = (first + j) % V7X_SUBLANES
        base = first + j - r
        term = rotated[r][base:base + SEQ_TILE, :] * w_ref[j:j + 1, cols]
        acc = term if acc is None else acc + term
    return acc


def _bconv_kernel(u_ref, st_ref, sg_ref, w_ref, cb_ref, g_ref, b_ref, o_ref, ext, y_scr,
                  *, n_prompt_steps):
    width = u_ref.shape[1]

    def unit_fn(unit):
        rows = slice(unit * SEQ_TILE, (unit + 1) * SEQ_TILE)
        total = jnp.zeros((SEQ_TILE, V7X_LANES), F32)
        for c in range(width // V7X_LANES):
            cols = slice(c * V7X_LANES, (c + 1) * V7X_LANES)
            y = _causal_taps(ext, unit, w_ref, B_HALO, B_CONV, cols) + cb_ref[:, cols]
            y_scr[:, cols] = y
            total = total + y
        mean = jnp.sum(total, axis=-1, keepdims=True) / width
        sq = jnp.zeros((SEQ_TILE, V7X_LANES), F32)
        for c in range(width // V7X_LANES):
            cols = slice(c * V7X_LANES, (c + 1) * V7X_LANES)
            yc = y_scr[:, cols] - mean
            sq = sq + yc * yc
        inv = lax.rsqrt(jnp.sum(sq, axis=-1, keepdims=True) / width + EPS)
        for c in range(width // V7X_LANES):
            cols = slice(c * V7X_LANES, (c + 1) * V7X_LANES)
            yn = (y_scr[:, cols] - mean) * inv * g_ref[:, cols] + b_ref[:, cols]
            o_ref[rows, cols] = (_silu(yn) * sg_ref[rows, cols]).astype(o_ref.dtype)

    _stream_conv_step(n_prompt_steps, B_HALO, ext, st_ref, u_ref, unit_fn)


def _cconv_kernel(z_ref, st_ref, bgs_ref, w_ref, cb_ref, o_ref, ext, *, n_prompt_steps):
    width = z_ref.shape[1]

    def unit_fn(unit):
        rows = slice(unit * SEQ_TILE, (unit + 1) * SEQ_TILE)
        for c in range(width // V7X_LANES):
            cols = slice(c * V7X_LANES, (c + 1) * V7X_LANES)
            y = _causal_taps(ext, unit, w_ref, C_HALO, C_CONV, cols) + cb_ref[:, cols]
            o_ref[rows, cols] = (bgs_ref[rows, cols] * y).astype(o_ref.dtype)

    _stream_conv_step(n_prompt_steps, C_HALO, ext, st_ref, z_ref, unit_fn)


def _cast_specs(weights, first_step, n_steps):
    specs, shapes = [], []
    for w in weights:
        slab = w.shape[0] // n_steps
        assert w.shape[0] % n_steps == 0 and slab % (2 * V7X_SUBLANES) == 0
        specs.append(pl.BlockSpec((slab, w.shape[1]), lambda s: (jnp.clip(s - first_step, 0, n_steps - 1), 0)))
        shapes.append(jax.ShapeDtypeStruct(w.shape, BF16))
    return specs, shapes


def _with_casts(body, n_inputs, n_casts):
    def kernel_body(*refs):
        sources = refs[n_inputs:n_inputs + n_casts]
        results = refs[n_inputs + n_casts + 1:n_inputs + 2 * n_casts + 1]
        for src, dst in zip(sources, results):
            dst[...] = src[...].astype(dst.dtype)
        body(*refs[:n_inputs], refs[n_inputs + n_casts], *refs[n_inputs + 2 * n_casts + 1:])
    return kernel_body


def _stream_conv_call(body, x, state, gate, gate_col_block, vecs, *, halo, n_prompt, n_sample,
                      scratch, name, casts=()):
    rows, width = x.shape
    n_blocks = n_prompt + n_sample
    assert rows == (n_blocks + 1) * Q_BLOCK
    assert state.shape[0] == n_sample * CONV_UNITS
    state = state.reshape(n_sample, CONV_UNITS, halo, width)
    block = lambda s: jnp.where(s == 0, n_blocks, s - 1)
    row_spec = pl.BlockSpec((Q_BLOCK, width), lambda s: (block(s), 0))
    vec_specs = [pl.BlockSpec(v.shape, lambda s: (0, 0)) for v in vecs]
    cast_specs, cast_shapes = _cast_specs(casts, 1, n_prompt)
    return pl.pallas_call(
        _with_casts(functools.partial(body, n_prompt_steps=n_prompt), 3 + len(vecs), len(casts)),
        grid=(1 + n_blocks,),
        in_specs=[
            row_spec,
            pl.BlockSpec((None, CONV_UNITS, halo, width),
                         lambda s: (jnp.clip(s - 1 - n_prompt, 0, n_sample - 1), 0, 0, 0)),
            pl.BlockSpec((Q_BLOCK, width), lambda s: (block(s), gate_col_block)),
        ] + vec_specs + cast_specs,
        out_specs=[row_spec] + cast_specs,
        out_shape=[jax.ShapeDtypeStruct((rows, width), BF16)] + cast_shapes,
        scratch_shapes=[pltpu.VMEM((halo + Q_BLOCK, width), F32)] + scratch,
        compiler_params=_params(("arbitrary",)),
        name=name,
    )(x, state, gate, *vecs, *casts)


def _pad_rows_front(a, rows):
    return jnp.pad(a, ((0, 0), (rows - a.shape[1], 0), (0, 0)))


def kernel(x_prompt, x_sample, cache_a_k, cache_a_v, cache_a_kidx, state_b_conv, state_c_conv, meta,
           e_norm_pre, e_w_in, e_kidx_ln_g, e_kidx_ln_b, e_bconv_w, e_bconv_b, e_bln_g, e_bln_b,
           e_w_out, e_norm_post, o_norm_pre, o_w_in, o_conv_w, o_conv_b, o_w_out, o_norm_post):
    nb, s_len, d = x_prompt.shape
    ndb, nds, _ = x_sample.shape
    past = cache_a_k.shape[2]
    bw = d // 2
    assert nb == 1 and e_w_in.shape[0] == 1 and o_w_in.shape[0] == 1
    assert nds == SEQ_TILE and s_len % Q_BLOCK == 0 and (ndb * nds) % Q_BLOCK == 0
    assert s_len >= B_CONV - 1 and A_WIDTH % bw == 0 and bw % V7X_LANES == 0
    ns = ndb * nds
    meta_row = s_len + ns
    rows = meta_row + Q_BLOCK
    topk_p = min(TOPK_MAX, s_len // 4)
    topk_s = min(TOPK_MAX, (past + nds) // 4)

    sources = (x_prompt[0], x_sample.reshape(ns, d), meta)

    w_in = e_w_in[0].T.astype(BF16)
    widths = (A_WIDTH, A_KV_WIDTH, A_KV_WIDTH, IDX_Q_WIDTH, IDX_DIM, IDX_HEADS, bw, bw, A_WIDTH + bw)
    starts = [sum(widths[:i]) for i in range(len(widths))]
    w_q, w_k, w_v, w_qi, w_ki, w_wi, w_val, w_glu, w_gate = [
        _WeightView(w_in, 0, d, s, max(w, V7X_LANES), transposed=True) for s, w in zip(starts, widths)]

    h0, = _pool_rowwise_call(_prenorm_kernel, *sources, [], [e_norm_pre[0]], [BF16], "even_prenorm")
    q_s, q_i = _matmul_call([h0], [w_q, w_qi], [(0, 0), (0, 1)], _ep_q_qi, [BF16, BF16], bn=512,
                            name="even_q_qidx")
    k32, v32, k16, v16 = _kv_call(h0, w_k, w_v, "even_kv")
    ki32, ki16, wi = _matmul_call([h0], [w_ki, w_wi], [(0, 0), (0, 1)], _ep_kidx, [F32, BF16, F32],
                                  bn=V7X_LANES, auxs=[e_kidx_ln_g[0], e_kidx_ln_b[0]], name="even_kidx")
    u, = _matmul_call([h0], [w_val, w_glu], [(0, 0), (0, 1)], _ep_glu, [F32], bn=512, name="even_glu")
    sg, = _matmul_call([h0], [w_gate], [(0, 0)], _ep_silu, [F32], bn=1024, name="even_gate")

    lpad_p = _round_up(Q_BLOCK + s_len, KEY_BLOCK) + KEY_BLOCK
    seq = lambda a: jnp.concatenate(
        [a[meta_row:meta_row + Q_BLOCK], a[:s_len],
         jnp.zeros((lpad_p - Q_BLOCK - s_len, a.shape[1]), a.dtype)], axis=0)
    oa, w_out, w_out_odd = _prompt_attn_call(
        q_s, q_i, wi, sg, seq(ki16), seq(k16), seq(v16), s_len=s_len,
        meta_block=meta_row // Q_BLOCK, topk=topk_p, casts=(e_w_out[0], o_w_out[0]))
    w_out_a = _WeightView(w_out, 0, A_WIDTH, 0, d)
    w_out_b = _WeightView(w_out, A_WIDTH, bw, 0, d)
    oa = _sample_attn_call(q_s, q_i, wi, sg, cache_a_kidx[0],
                           cache_a_k[0].reshape(ndb, past * A_KV_HEADS, A_HEAD_DIM),
                           cache_a_v[0].reshape(ndb, past * A_KV_HEADS, A_HEAD_DIM), ki16, k16, v16, oa,
                           s_len=s_len, topk=topk_s)

    conv_steps = dict(n_prompt=s_len // Q_BLOCK, n_sample=ns // Q_BLOCK)
    bconv_w = jnp.pad(e_bconv_w[0], ((0, B_HALO - B_CONV), (0, 0)))
    ob, w_odd = _stream_conv_call(
        _bconv_kernel, u, _pad_rows_front(state_b_conv[0], B_HALO), sg, A_WIDTH // bw,
        [bconv_w, e_bconv_b[0].reshape(1, bw), e_bln_g[0].reshape(1, bw), e_bln_b[0].reshape(1, bw)],
        halo=B_HALO, scratch=[pltpu.VMEM((SEQ_TILE, bw), F32)], name="conformer_conv",
        casts=(o_w_in[0],), **conv_steps)

    y0, = _matmul_call([oa, ob], [w_out_a, w_out_b], [(0, 0), (1, 1)], _ep_sum, [BF16],
                       bn=1024, name="even_out")
    x1, h1 = _pool_rowwise_call(_postnorm_prenorm_kernel, *sources, [y0],
                                [e_norm_post[0], o_norm_pre[0]], [F32, BF16], "even_postnorm_odd_prenorm")

    w_hx, w_cg, w_bg, w_og = [_WeightView(w_odd, 0, d, i * d, d) for i in range(4)]
    z, bgs = _matmul_call([h1], [w_hx, w_cg, w_bg, w_og], [(0, 0), (0, 1), (0, 2), (0, 3)], _ep_odd,
                          [F32, BF16], bn=256, name="odd_in")
    cconv_w = jnp.pad(o_conv_w[0], ((0, C_HALO - C_CONV), (0, 0)))
    o1, = _stream_conv_call(
        _cconv_kernel, z, _pad_rows_front(state_c_conv[0], C_HALO), bgs, 0,
        [cconv_w, o_conv_b[0].reshape(1, d)],
        halo=C_HALO, scratch=[], name="short_conv", **conv_steps)
    y1, = _matmul_call([o1], [_WeightView(w_out_odd, 0, d, 0, d)], [(0, 0)],
                       _ep_identity, [BF16], bn=1024, name="odd_out")
    y_prompt = _final_postnorm_call(x1, y1, o_norm_post[0], 0, s_len // Q_BLOCK, "odd_postnorm_prompt")
    y_sample = _final_postnorm_call(x1, y1, o_norm_post[0], s_len // Q_BLOCK, ns // Q_BLOCK,
                                    "odd_postnorm_sample")

    def prompt_rows(a, per_frame=1):
        return jnp.concatenate([a[meta_row * per_frame:(meta_row + N_META) * per_frame],
                                a[:s_len * per_frame]], axis=0)

    def sample_rows(a, per_frame=1):
        return a[s_len * per_frame:meta_row * per_frame].reshape(ndb, nds * per_frame, a.shape[1])

    kv_shape = (A_KV_HEADS, A_HEAD_DIM)
    y_prompt = y_prompt[None]
    y_sample = y_sample.reshape(ndb, nds, d)
    prompt_a_k = prompt_rows(k32, A_KV_HEADS).reshape(1, 1, N_META + s_len, *kv_shape)
    prompt_a_v = prompt_rows(v32, A_KV_HEADS).reshape(1, 1, N_META + s_len, *kv_shape)
    prompt_a_kidx = prompt_rows(ki32)[None, None]
    prompt_b_conv = u[s_len - (B_CONV - 1):s_len][None, None]
    prompt_c_conv = z[s_len - (C_CONV - 1):s_len][None, None]
    sample_a_k = sample_rows(k32, A_KV_HEADS).reshape(1, ndb, nds, *kv_shape)
    sample_a_v = sample_rows(v32, A_KV_HEADS).reshape(1, ndb, nds, *kv_shape)
    sample_a_kidx = sample_rows(ki32)[None]
    sample_b_conv = sample_rows(u)[None, :, nds - (B_CONV - 1):]
    sample_c_conv = sample_rows(z)[None, :, nds - (C_CONV - 1):]
    return (y_prompt, y_sample, prompt_a_k, prompt_a_v, prompt_a_kidx, prompt_b_conv, prompt_c_conv,
            sample_a_k, sample_a_v, sample_a_kidx, sample_b_conv, sample_c_conv)
```

```python
import functools
from typing import NamedTuple

import jax
import jax.numpy as jnp
from jax import lax
from jax.experimental import pallas as pl
from jax.experimental.pallas import tpu as pltpu

CHUNK = 64
N_META = 16
EPS = 1e-6
A_HEADS = 16
A_KV_HEADS = 4
A_HEAD_DIM = 128
A_WIDTH = A_HEADS * A_HEAD_DIM
A_KV_WIDTH = A_KV_HEADS * A_HEAD_DIM
A_GROUP = A_HEADS // A_KV_HEADS
IDX_HEADS = 16
IDX_DIM = 128
IDX_Q_WIDTH = IDX_HEADS * IDX_DIM
INDEX_SCALE = (IDX_HEADS * IDX_DIM) ** -0.5
TOPK_MAX = 256
B_CONV = 31
C_CONV = 3

V7X_LANES = 128
V7X_SUBLANES = 8
V7X_VMEM_LIMIT_BYTES = 56 * 1024 * 1024

Q_BLOCK = 128
KEY_BLOCK = 512
COUNT_ROWS = 512
SEARCH_GROUP = 4
SEQ_TILE = 64
B_HALO = 32
C_HALO = 8
INT_MIN = -2 ** 31
MASKED = -1e30
LOG2_E = 1.4426950408889634

F32 = jnp.float32
BF16 = jnp.bfloat16
I32 = jnp.int32


def _round_up(x, m):
    return (x + m - 1) // m * m


def _row_tile(rows, cap):
    best = 16
    for t in range(16, cap + 1, 16):
        if rows % t == 0:
            best = t
    return best


def _params(sem):
    return pltpu.CompilerParams(dimension_semantics=sem, vmem_limit_bytes=V7X_VMEM_LIMIT_BYTES)


def _rms(x, g):
    return x * lax.rsqrt(jnp.mean(x * x, axis=-1, keepdims=True) + EPS) * g


def _gather_pool_block(xp_ref, xs_ref, meta_ref, x_scr, n_prompt, n_sample):
    i = pl.program_id(0)

    @pl.when(i < n_prompt)
    def _():
        x_scr[...] = xp_ref[...]

    @pl.when((i >= n_prompt) & (i < n_prompt + n_sample))
    def _():
        x_scr[...] = xs_ref[...]

    @pl.when(i == n_prompt + n_sample)
    def _():
        x_scr[0:N_META, :] = meta_ref[...]
        x_scr[N_META:, :] = jnp.zeros((Q_BLOCK - N_META, x_scr.shape[1]), F32)


def _prenorm_kernel(xp_ref, xs_ref, meta_ref, g_ref, h_ref, x_scr, *, n_prompt, n_sample):
    _gather_pool_block(xp_ref, xs_ref, meta_ref, x_scr, n_prompt, n_sample)
    h_ref[...] = _rms(x_scr[...], g_ref[...]).astype(h_ref.dtype)


def _postnorm_prenorm_kernel(xp_ref, xs_ref, meta_ref, y_ref, gpost_ref, gpre_ref, xo_ref, h_ref, x_scr,
                             *, n_prompt, n_sample):
    _gather_pool_block(xp_ref, xs_ref, meta_ref, x_scr, n_prompt, n_sample)
    x1 = x_scr[...] + _rms(y_ref[...].astype(F32), gpost_ref[...])
    xo_ref[...] = x1
    h_ref[...] = _rms(x1, gpre_ref[...]).astype(h_ref.dtype)


def _postnorm_kernel(x_ref, y_ref, gpost_ref, xo_ref):
    xo_ref[...] = x_ref[...] + _rms(y_ref[...].astype(F32), gpost_ref[...])


def _pool_rowwise_call(body, x_prompt, x_sample, meta, pool_inputs, vec_inputs, out_dtypes, name):
    s_len, d = x_prompt.shape
    n_prompt, n_sample = s_len // Q_BLOCK, x_sample.shape[0] // Q_BLOCK
    n_blocks = n_prompt + n_sample + 1
    pool_spec = pl.BlockSpec((Q_BLOCK, d), lambda i: (i, 0))
    vec_spec = pl.BlockSpec((1, d), lambda i: (0, 0))
    return pl.pallas_call(
        functools.partial(body, n_prompt=n_prompt, n_sample=n_sample),
        grid=(n_blocks,),
        in_specs=[
            pl.BlockSpec((Q_BLOCK, d), lambda i: (jnp.minimum(i, n_prompt - 1), 0)),
            pl.BlockSpec((Q_BLOCK, d), lambda i: (jnp.clip(i - n_prompt, 0, n_sample - 1), 0)),
            pl.BlockSpec((N_META, d), lambda i: (0, 0)),
        ] + [pool_spec] * len(pool_inputs) + [vec_spec] * len(vec_inputs),
        out_specs=[pool_spec] * len(out_dtypes),
        out_shape=[jax.ShapeDtypeStruct((n_blocks * Q_BLOCK, d), dt) for dt in out_dtypes],
        scratch_shapes=[pltpu.VMEM((Q_BLOCK, d), F32)],
        compiler_params=_params(("arbitrary",)),
        name=name,
    )(x_prompt, x_sample, meta, *pool_inputs, *[v.reshape(1, d) for v in vec_inputs])


def _final_postnorm_call(x1, y1, g, first_block, n_blocks, name):
    d = x1.shape[1]
    per_step = max(p for p in (1, 2, 4) if first_block % p == 0 and n_blocks % p == 0)
    bm, first, steps = per_step * Q_BLOCK, first_block // per_step, n_blocks // per_step
    in_spec = pl.BlockSpec((bm, d), lambda i: (i + first, 0))
    return pl.pallas_call(
        _postnorm_kernel,
        grid=(steps,),
        in_specs=[in_spec, in_spec, pl.BlockSpec((1, d), lambda i: (0, 0))],
        out_specs=pl.BlockSpec((bm, d), lambda i: (i, 0)),
        out_shape=jax.ShapeDtypeStruct((n_blocks * Q_BLOCK, d), F32),
        compiler_params=_params(("parallel",)),
        name=name,
    )(x1, y1, g.reshape(1, d))


def _matmul_kernel(*refs, n_x, n_w, n_aux, terms, epilogue, transposed):
    xs = refs[:n_x]
    ws = refs[n_x:n_x + n_w]
    auxs = refs[n_x + n_w:n_x + n_w + n_aux]
    outs = refs[n_x + n_w + n_aux:]
    accs = [lax.dot_general(xs[a][...], ws[b][...],
                            (((1,), (1 if transposed[b] else 0,)), ((), ())),
                            preferred_element_type=F32) for a, b in terms]
    results = epilogue(*accs, *[r[...] for r in auxs])
    for o_ref, r in zip(outs, results):
        o_ref[...] = r.astype(o_ref.dtype)


class _WeightView(NamedTuple):
    array: jax.Array
    row0: int
    rows: int
    col0: int
    cols: int
    transposed: bool = False


def _weight_spec(w, bn):
    if w.transposed:
        return pl.BlockSpec(
            (pl.Element(bn), pl.Element(w.rows)),
            lambda i, j: (pl.multiple_of(w.col0 + j * bn, 2 * V7X_SUBLANES), w.row0))
    return pl.BlockSpec((w.rows, bn), lambda i, j: (w.row0 // w.rows, w.col0 // bn + j))


def _matmul_call(xs, ws, terms, epilogue, out_dtypes, *, bn, auxs=(), name):
    rows = xs[0].shape[0]
    n = ws[0].cols
    bm = _row_tile(rows, 1536)
    bn = max(t for t in range(V7X_LANES, min(bn, n) + 1, V7X_LANES)
             if n % t == 0 and all(w.transposed or w.col0 % t == 0 for w in ws))
    assert all(w.cols == n and w.row0 % w.rows == 0 for w in ws)
    assert all(w.col0 % (2 * V7X_SUBLANES) == 0 for w in ws if w.transposed)
    in_specs = [pl.BlockSpec((bm, x.shape[1]), lambda i, j: (i, 0)) for x in xs]
    in_specs += [_weight_spec(w, bn) for w in ws]
    in_specs += [pl.BlockSpec((1, bn), lambda i, j: (0, j)) for _ in auxs]
    out_spec = pl.BlockSpec((bm, bn), lambda i, j: (i, j))
    body = functools.partial(_matmul_kernel, n_x=len(xs), n_w=len(ws), n_aux=len(auxs),
                             terms=terms, epilogue=epilogue,
                             transposed=tuple(w.transposed for w in ws))
    return pl.pallas_call(
        body,
        grid=(rows // bm, n // bn),
        in_specs=in_specs,
        out_specs=[out_spec] * len(out_dtypes),
        out_shape=[jax.ShapeDtypeStruct((rows, n), dt) for dt in out_dtypes],
        compiler_params=_params(("parallel", "arbitrary")),
        name=name,
    )(*xs, *[w.array for w in ws], *[a.reshape(1, n) for a in auxs])


def _kv_kernel(x_ref, wk_ref, wv_ref, k32_ref, v32_ref, k16_ref, v16_ref):
    bm = x_ref.shape[0]
    for w_ref, o32_ref, o16_ref in ((wk_ref, k32_ref, k16_ref), (wv_ref, v32_ref, v16_ref)):
        acc = lax.dot_general(x_ref[...], w_ref[...], (((1,), (1,)), ((), ())),
                              preferred_element_type=F32)
        o16_ref[...] = acc.astype(o16_ref.dtype)
        for g in range(A_KV_HEADS):
            o32_ref[pl.ds(g, bm, stride=A_KV_HEADS), :] = acc[:, g * A_HEAD_DIM:(g + 1) * A_HEAD_DIM]


def _kv_call(x, w_k, w_v, name):
    rows = x.shape[0]
    bm = _row_tile(rows, 1536)
    assert w_k.transposed and w_v.transposed and w_k.cols == A_KV_WIDTH == w_v.cols
    head_rows = pl.BlockSpec((bm * A_KV_HEADS, A_HEAD_DIM), lambda i, j: (i, 0))
    wide = pl.BlockSpec((bm, A_KV_WIDTH), lambda i, j: (i, 0))
    return pl.pallas_call(
        _kv_kernel,
        grid=(rows // bm, 1),
        in_specs=[pl.BlockSpec((bm, x.shape[1]), lambda i, j: (i, 0)),
                  _weight_spec(w_k, A_KV_WIDTH), _weight_spec(w_v, A_KV_WIDTH)],
        out_specs=[head_rows, head_rows, wide, wide],
        out_shape=[jax.ShapeDtypeStruct((rows * A_KV_HEADS, A_HEAD_DIM), F32)] * 2
        + [jax.ShapeDtypeStruct((rows, A_KV_WIDTH), BF16)] * 2,
        compiler_params=_params(("parallel", "arbitrary")),
        name=name,
    )(x, w_k.array, w_v.array)


def _silu(x):
    return x * jax.nn.sigmoid(x)


def _ep_q_qi(q, qi):
    return q * (A_HEAD_DIM ** -0.5 * LOG2_E), qi


def _ep_kidx(ki, wi, g, b):
    kc = ki - jnp.mean(ki, axis=-1, keepdims=True)
    kn = kc * lax.rsqrt(jnp.mean(kc * kc, axis=-1, keepdims=True) + EPS) * g + b
    return kn, kn, wi * INDEX_SCALE


def _ep_glu(val, glu):
    return (val * jax.nn.sigmoid(glu),)


def _ep_silu(gate):
    return (_silu(gate),)


def _ep_sum(a, b):
    return (a + b,)


def _ep_identity(a):
    return (a,)


def _ep_odd(hx, cg, bg, gate):
    return cg * hx, bg * _silu(gate)


def _prompt_attn_kernel(qs_ref, qi_ref, wi_ref, sg_ref, ki_ref, k_ref, v_ref, o_ref, *scratch,
                        topk, n_prompt_blocks):
    step = pl.program_id(0)

    @pl.when(step <= n_prompt_blocks)
    def _():
        frame = (step - 1) * Q_BLOCK + lax.broadcasted_iota(I32, (1, Q_BLOCK), 1)
        limit = jnp.where(step == 0, 0, Q_BLOCK + CHUNK * (frame // CHUNK + 1))
        n_kt = jnp.where(step == 0, 1, pl.cdiv(Q_BLOCK * (step + 1), KEY_BLOCK))
        _attend(qs_ref, qi_ref, wi_ref, sg_ref, ki_ref, k_ref, v_ref, *scratch,
                topk=topk, n_lo=N_META, limit=limit, n_kt=n_kt)
        o_ref[...] = scratch[-1][...].astype(o_ref.dtype)

    @pl.when(step > n_prompt_blocks)
    def _():
        o_ref[...] = jnp.zeros(o_ref.shape, o_ref.dtype)


def _sample_attn_kernel(qs_ref, qi_ref, wi_ref, sg_ref, cki_ref, ck_ref, cv_ref, nki_ref, nk_ref, nv_ref,
                        pool_ref, o_ref, ki_buf, k_buf, v_buf, *scratch, topk):
    del pool_ref
    step = pl.program_id(0)
    past, new = cki_ref.shape[0], nk_ref.shape[0]
    for buf, cache_ref, new_ref in ((ki_buf, cki_ref, nki_ref), (k_buf, ck_ref, nk_ref), (v_buf, cv_ref, nv_ref)):
        heads = cache_ref.shape[0] // past
        for r in range(0, past, KEY_BLOCK):
            n = min(KEY_BLOCK, past - r)
            for g in range(heads):
                buf[r:r + n, g * A_HEAD_DIM:(g + 1) * A_HEAD_DIM] = cache_ref[
                    pl.ds(r * heads + g, n, stride=heads), :].astype(BF16)
        buf[past:past + new, :] = new_ref[...]
        buf[past + new:, :] = jnp.zeros((buf.shape[0] - past - new, buf.shape[1]), BF16)
    q0 = pl.multiple_of((step % (Q_BLOCK // new)) * new, new)
    _attend(qs_ref, qi_ref, wi_ref, sg_ref, ki_buf, k_buf, v_buf, *scratch,
            topk=topk, n_lo=past + new, limit=jnp.zeros((1, Q_BLOCK), I32),
            n_kt=ki_buf.shape[0] // KEY_BLOCK - 1, q0=q0, hq=new)
    o_ref[...] = scratch[-1][...].astype(o_ref.dtype)


def _attend(qs_ref, qi_ref, wi_ref, sg_ref, ki_ref, k_ref, v_ref,
            qrows, qirows, keys, bias, s_a, s_b, m_scr, l_scr, acc_scr, o_scr,
            *, topk, n_lo, limit, n_kt, q0=0, hq=Q_BLOCK):
    nq = Q_BLOCK
    lb = KEY_BLOCK
    fold = nq // hq
    assert fold in (1, 2)

    @pl.when(pl.program_id(0) == 0)
    def _():
        one_hot = (lax.broadcasted_iota(I32, (hq, nq), 0) == lax.broadcasted_iota(I32, (hq, nq), 1))
        for h in range(A_HEADS):
            qrows[h * hq:(h + 1) * hq, A_HEAD_DIM:A_HEAD_DIM + nq] = one_hot.astype(BF16)

    for h in range(A_HEADS):
        qrows[h * hq:(h + 1) * hq, 0:A_HEAD_DIM] = qs_ref[pl.ds(q0, hq), h * A_HEAD_DIM:(h + 1) * A_HEAD_DIM]
    for h in range(IDX_HEADS):
        qirows[h * hq:(h + 1) * hq, :] = qi_ref[pl.ds(q0, hq), h * IDX_DIM:(h + 1) * IDX_DIM]
    w_t = jnp.concatenate([wi_ref[pl.ds(q0, hq), :]] * fold, axis=0).T
    lane_head = lax.broadcasted_iota(I32, (1, nq), 1) // hq
    w_tiles = []
    for j in range(IDX_HEADS // fold):
        w_tile = w_t[j * fold:j * fold + 1, :]
        for f in range(1, fold):
            w_tile = jnp.where(lane_head == f, w_t[j * fold + f:j * fold + f + 1, :], w_tile)
        w_tiles.append(w_tile)

    ib = 2 * lb

    def score_block(kt, carry):
        r0 = pl.multiple_of(kt * ib, ib)
        d = lax.dot_general(ki_ref[pl.ds(r0, ib), :], qirows[...], (((1,), (1,)), ((), ())),
                            preferred_element_type=F32)
        sc = jnp.zeros((ib, nq), F32)
        for j, w_tile in enumerate(w_tiles):
            sc = sc + jnp.maximum(d[:, j * nq:(j + 1) * nq], 0.0) * w_tile
        for f in range(1, fold):
            sc = sc + pltpu.roll(sc, f * hq, axis=1)
        pos = r0 + lax.broadcasted_iota(I32, (ib, nq), 0)
        adm = (pos < n_lo) | ((pos >= Q_BLOCK) & (pos < limit))
        bits = lax.bitcast_convert_type(sc, I32)
        key = bits ^ ((bits >> 31) & 0x7FFFFFFF)
        keys[pl.ds(r0, ib), :] = jnp.where(adm, key, INT_MIN)
        return carry

    lax.fori_loop(0, (n_kt + 1) // 2, score_block, 0)

    n_ct = n_kt * (lb // COUNT_ROWS)

    def count_where(pred):
        def body(c, acc):
            r0 = pl.multiple_of(c * COUNT_ROWS, COUNT_ROWS)
            pos = r0 + lax.broadcasted_iota(I32, (COUNT_ROWS, nq), 0)
            hit = jnp.where(pred(keys[pl.ds(r0, COUNT_ROWS), :], pos), 1.0, 0.0)
            parts = [hit[i * V7X_SUBLANES:(i + 1) * V7X_SUBLANES, :]
                     for i in range(COUNT_ROWS // V7X_SUBLANES)]
            while len(parts) > 1:
                parts = [a + b for a, b in zip(parts[::2], parts[1::2])]
            return acc + parts[0]
        acc = lax.fori_loop(0, n_ct, body, jnp.zeros((V7X_SUBLANES, nq), F32))
        return jnp.sum(acc, axis=0, keepdims=True)

    def search_bit(i, state):
        prefix, n_sel = state
        trial = prefix | lax.shift_left(jnp.int32(1), 31 - i)
        cand = trial ^ INT_MIN
        n_trial = count_where(lambda k, pos: k >= cand)
        take = n_trial >= float(topk)
        return jnp.where(take, trial, prefix), jnp.where(take, n_trial, n_sel)

    def search_bits(state):
        group, prefix, n_sel = state
        prefix, n_sel = lax.fori_loop(
            group * SEARCH_GROUP, (group + 1) * SEARCH_GROUP, search_bit, (prefix, n_sel))
        return group + 1, prefix, n_sel

    def search_open(state):
        group, _, n_sel = state
        return (group < 32 // SEARCH_GROUP) & (jnp.max(n_sel) > float(topk))

    n_admissible = (n_lo + jnp.maximum(limit - Q_BLOCK, 0)).astype(F32)
    _, prefix, n_sel = lax.while_loop(search_open, search_bits,
                                      (jnp.int32(0), jnp.zeros((1, nq), I32), n_admissible))
    thr = jnp.maximum(prefix ^ INT_MIN, INT_MIN + 1)

    def write_bias(selected):
        def body(c, carry):
            r0 = pl.multiple_of(c * COUNT_ROWS, COUNT_ROWS)
            pos = r0 + lax.broadcasted_iota(I32, (COUNT_ROWS, nq), 0)
            bias[pl.ds(r0, COUNT_ROWS), :] = jnp.where(
                selected(keys[pl.ds(r0, COUNT_ROWS), :], pos), 0.0, MASKED).astype(BF16)
            return carry
        lax.fori_loop(0, n_ct, body, 0)

    surplus = jnp.max(n_sel) > float(topk)

    @pl.when(jnp.logical_not(surplus))
    def _():
        write_bias(lambda k, pos: k >= thr)

    @pl.when(surplus)
    def _():
        keep = float(topk) - count_where(lambda k, pos: k > thr)
        position_bits = keys.shape[0].bit_length()

        def cut_bit(i, prefix):
            trial = prefix | lax.shift_left(jnp.int32(1), position_bits - 1 - i)
            admitted = count_where(lambda k, pos: (k == thr) & (pos < trial))
            return jnp.where(admitted < keep, trial, prefix)
        cut = lax.fori_loop(0, position_bits, cut_bit, jnp.zeros((1, nq), I32)) + 1
        write_bias(lambda k, pos: (k > thr) | ((k == thr) & (pos < cut)))

    gw = A_GROUP * hq
    m_scr[...] = jnp.full(m_scr.shape, MASKED, F32)
    l_scr[...] = jnp.zeros(l_scr.shape, F32)
    acc_scr[...] = jnp.zeros(acc_scr.shape, F32)

    bias[pl.ds(pl.multiple_of(n_kt * lb, lb), lb), :] = jnp.full((lb, nq), MASKED, BF16)

    def score_keys(kt, s_ref, g):
        r0 = pl.multiple_of(kt * lb, lb)
        k_aug = jnp.concatenate(
            [k_ref[pl.ds(r0, lb), g * A_HEAD_DIM:(g + 1) * A_HEAD_DIM], bias[pl.ds(r0, lb), :]], axis=1)
        s_ref[g] = lax.dot_general(k_aug, qrows[g * gw:(g + 1) * gw, :], (((1,), (1,)), ((), ())),
                                   preferred_element_type=F32)

    def softmax_pv(kt, s_ref, g):
        s = s_ref[g]
        m_old = m_scr[g]
        m_new = jnp.maximum(m_old, jnp.max(s, axis=0, keepdims=True))
        alpha = jnp.exp2(m_old - m_new)
        p = jnp.exp2(s - m_new)
        l_scr[g] = l_scr[g] * alpha + jnp.sum(p, axis=0, keepdims=True)
        v_blk = v_ref[pl.ds(pl.multiple_of(kt * lb, lb), lb), g * A_HEAD_DIM:(g + 1) * A_HEAD_DIM]
        acc_scr[g] = acc_scr[g] * alpha + lax.dot_general(
            v_blk, p.astype(BF16), (((0,), (0,)), ((), ())), preferred_element_type=F32)
        m_scr[g] = m_new

    for g in range(A_KV_HEADS):
        score_keys(0, s_a, g)

    def attend_pair(j, carry):
        kt = 2 * j
        for g in range(A_KV_HEADS):
            score_keys(kt + 1, s_b, g)
        for g in range(A_KV_HEADS):
            softmax_pv(kt, s_a, g)
        for g in range(A_KV_HEADS):
            score_keys(jnp.minimum(kt + 2, n_kt), s_a, g)
        for g in range(A_KV_HEADS):
            softmax_pv(kt + 1, s_b, g)
        return carry

    lax.fori_loop(0, n_kt // 2, attend_pair, 0)

    @pl.when(n_kt % 2 == 1)
    def _():
        for g in range(A_KV_HEADS):
            softmax_pv(n_kt - 1, s_a, g)

    for g in range(A_KV_HEADS):
        o_g = (acc_scr[g] / l_scr[g]).T
        for r in range(A_GROUP):
            h = g * A_GROUP + r
            cols = slice(h * A_HEAD_DIM, (h + 1) * A_HEAD_DIM)
            o_scr[:, cols] = o_g[r * hq:(r + 1) * hq, :] * sg_ref[pl.ds(q0, hq), cols]


def _attn_scratch(lpad, hq=Q_BLOCK):
    gw = A_GROUP * hq
    return [
        pltpu.VMEM((A_HEADS * hq, A_HEAD_DIM + Q_BLOCK), BF16),
        pltpu.VMEM((IDX_HEADS * hq, IDX_DIM), BF16),
        pltpu.VMEM((lpad, Q_BLOCK), I32),
        pltpu.VMEM((lpad, Q_BLOCK), BF16),
        pltpu.VMEM((A_KV_HEADS, KEY_BLOCK, gw), F32),
        pltpu.VMEM((A_KV_HEADS, KEY_BLOCK, gw), F32),
        pltpu.VMEM((A_KV_HEADS, 1, gw), F32),
        pltpu.VMEM((A_KV_HEADS, 1, gw), F32),
        pltpu.VMEM((A_KV_HEADS, A_HEAD_DIM, gw), F32),
        pltpu.VMEM((hq, A_WIDTH), F32),
    ]


def _query_specs(q_index):
    qspec = pl.BlockSpec((Q_BLOCK, A_WIDTH), lambda s: (q_index(s), 0))
    return [qspec, qspec, pl.BlockSpec((Q_BLOCK, V7X_LANES), lambda s: (q_index(s), 0)), qspec]


def _prompt_attn_call(qs, qi, wi, sg, ki_seq, k_seq, v_seq, *, s_len, meta_block, topk, casts=()):
    lpad = ki_seq.shape[0]
    n_qb = s_len // Q_BLOCK
    full = lambda a: pl.BlockSpec(a.shape, lambda s: (0, 0))
    cast_specs, cast_shapes = _cast_specs(casts, 1, n_qb)
    body = functools.partial(_prompt_attn_kernel, topk=topk, n_prompt_blocks=n_qb)
    return pl.pallas_call(
        _with_casts(body, 7, len(casts)),
        grid=(1 + meta_block,),
        in_specs=_query_specs(lambda s: jnp.where(s == 0, meta_block, s - 1))
        + [full(ki_seq), full(k_seq), full(v_seq)] + cast_specs,
        out_specs=[pl.BlockSpec((Q_BLOCK, A_WIDTH), lambda s: (jnp.where(s == 0, meta_block, s - 1), 0))]
        + cast_specs,
        out_shape=[jax.ShapeDtypeStruct(((meta_block + 1) * Q_BLOCK, A_WIDTH), BF16)] + cast_shapes,
        scratch_shapes=_attn_scratch(lpad),
        compiler_params=_params(("arbitrary",)),
        name="prompt_attention",
    )(qs, qi, wi, sg, ki_seq, k_seq, v_seq, *casts)


def _sample_attn_call(qs, qi, wi, sg, cache_ki, cache_k, cache_v, ki16, k16, v16, pool_out,
                      *, s_len, topk):
    ndb, past, _ = cache_ki.shape
    new = SEQ_TILE
    lpad = _round_up(past + new, KEY_BLOCK) + KEY_BLOCK
    cache_spec = lambda a: pl.BlockSpec((None,) + a.shape[1:], lambda s: (s, 0, 0))
    new_spec = lambda a: pl.BlockSpec((new, a.shape[1]), lambda s: (s_len // new + s, 0))
    return pl.pallas_call(
        functools.partial(_sample_attn_kernel, topk=topk),
        grid=(ndb,),
        in_specs=_query_specs(lambda s: s_len // Q_BLOCK + s // (Q_BLOCK // new))
        + [cache_spec(cache_ki), cache_spec(cache_k), cache_spec(cache_v),
           new_spec(ki16), new_spec(k16), new_spec(v16), pl.BlockSpec(memory_space=pl.ANY)],
        out_specs=pl.BlockSpec((new, A_WIDTH), lambda s: (s_len // new + s, 0)),
        out_shape=jax.ShapeDtypeStruct(pool_out.shape, BF16),
        input_output_aliases={10: 0},
        scratch_shapes=[pltpu.VMEM((lpad, IDX_DIM), BF16), pltpu.VMEM((lpad, A_KV_WIDTH), BF16),
                        pltpu.VMEM((lpad, A_KV_WIDTH), BF16)] + _attn_scratch(lpad, new),
        compiler_params=_params(("arbitrary",)),
        name="sample_attention",
    )(qs, qi, wi, sg, cache_ki, cache_k, cache_v, ki16, k16, v16, pool_out)


CONV_UNITS = Q_BLOCK // SEQ_TILE


def _stream_conv_step(n_prompt_steps, halo, ext, state_ref, cur_ref, unit_fn):
    step = pl.program_id(0)
    is_sample = step > n_prompt_steps

    @pl.when(step == 0)
    def _():
        ext[0:halo, :] = jnp.zeros((halo, ext.shape[1]), F32)

    ext[halo:halo + Q_BLOCK, :] = cur_ref[...]
    for u in range(CONV_UNITS):
        @pl.when(is_sample)
        def _():
            ext[u * SEQ_TILE:u * SEQ_TILE + halo, :] = state_ref[u]

        unit_fn(u)

    @pl.when(step == 0)
    def _():
        if halo > N_META:
            ext[0:halo - N_META, :] = jnp.zeros((halo - N_META, ext.shape[1]), F32)
        keep = min(halo, N_META)
        ext[halo - keep:halo, :] = ext[halo + N_META - keep:halo + N_META, :]

    @pl.when(step > 0)
    def _():
        ext[0:halo, :] = ext[Q_BLOCK:Q_BLOCK + halo, :]


def _causal_taps(ext, unit, w_ref, halo, width, cols):
    first = halo - (width - 1)
    x = ext[unit * SEQ_TILE:unit * SEQ_TILE + halo + SEQ_TILE, cols]
    rows = x.shape[0]
    rotated = {0: x}
    for j in range(width):
        r = (first + j) % V7X_SUBLANES
        if r not in rotated:
            rotated[r] = pltpu.roll(x, rows - r, axis=0)
    acc = None
    for j in range(width):
        r = (first + j) % V7X_SUBLANES
        base = first + j - r
        term = rotated[r][base:base + SEQ_TILE, :] * w_ref[j:j + 1, cols]
        acc = term if acc is None else acc + term
    return acc


def _bconv_kernel(u_ref, st_ref, sg_ref, w_ref, cb_ref, g_ref, b_ref, o_ref, ext, y_scr,
                  *, n_prompt_steps):
    width = u_ref.shape[1]

    def unit_fn(unit):
        rows = slice(unit * SEQ_TILE, (unit + 1) * SEQ_TILE)
        total = jnp.zeros((SEQ_TILE, V7X_LANES), F32)
        for c in range(width // V7X_LANES):
            cols = slice(c * V7X_LANES, (c + 1) * V7X_LANES)
            y = _causal_taps(ext, unit, w_ref, B_HALO, B_CONV, cols) + cb_ref[:, cols]
            y_scr[:, cols] = y
            total = total + y
        mean = jnp.sum(total, axis=-1, keepdims=True) / width
        sq = jnp.zeros((SEQ_TILE, V7X_LANES), F32)
        for c in range(width // V7X_LANES):
            cols = slice(c * V7X_LANES, (c + 1) * V7X_LANES)
            yc = y_scr[:, cols] - mean
            sq = sq + yc * yc
        inv = lax.rsqrt(jnp.sum(sq, axis=-1, keepdims=True) / width + EPS)
        for c in range(width // V7X_LANES):
            cols = slice(c * V7X_LANES, (c + 1) * V7X_LANES)
            yn = (y_scr[:, cols] - mean) * inv * g_ref[:, cols] + b_ref[:, cols]
            o_ref[rows, cols] = (_silu(yn) * sg_ref[rows, cols]).astype(o_ref.dtype)

    _stream_conv_step(n_prompt_steps, B_HALO, ext, st_ref, u_ref, unit_fn)


def _cconv_kernel(z_ref, st_ref, bgs_ref, w_ref, cb_ref, o_ref, ext, *, n_prompt_steps):
    width = z_ref.shape[1]

    def unit_fn(unit):
        rows = slice(unit * SEQ_TILE, (unit + 1) * SEQ_TILE)
        for c in range(width // V7X_LANES):
            cols = slice(c * V7X_LANES, (c + 1) * V7X_LANES)
            y = _causal_taps(ext, unit, w_ref, C_HALO, C_CONV, cols) + cb_ref[:, cols]
            o_ref[rows, cols] = (bgs_ref[rows, cols] * y).astype(o_ref.dtype)

    _stream_conv_step(n_prompt_steps, C_HALO, ext, st_ref, z_ref, unit_fn)


def _cast_specs(weights, first_step, n_steps):
    specs, shapes = [], []
    for w in weights:
        slab = w.shape[0] // n_steps
        assert w.shape[0] % n_steps == 0 and slab % (2 * V7X_SUBLANES) == 0
        specs.append(pl.BlockSpec((slab, w.shape[1]), lambda s: (jnp.clip(s - first_step, 0, n_steps - 1), 0)))
        shapes.append(jax.ShapeDtypeStruct(w.shape, BF16))
    return specs, shapes


def _with_casts(body, n_inputs, n_casts):
    def kernel_body(*refs):
        sources = refs[n_inputs:n_inputs + n_casts]
        results = refs[n_inputs + n_casts + 1:n_inputs + 2 * n_casts + 1]
        for src, dst in zip(sources, results):
            dst[...] = src[...].astype(dst.dtype)
        body(*refs[:n_inputs], refs[n_inputs + n_casts], *refs[n_inputs + 2 * n_casts + 1:])
    return kernel_body


def _stream_conv_call(body, x, state, gate, gate_col_block, vecs, *, halo, n_prompt, n_sample,
                      scratch, name, casts=()):
    rows, width = x.shape
    n_blocks = n_prompt + n_sample
    assert rows == (n_blocks + 1) * Q_BLOCK
    assert state.shape[0] == n_sample * CONV_UNITS
    state = state.reshape(n_sample, CONV_UNITS, halo, width)
    block = lambda s: jnp.where(s == 0, n_blocks, s - 1)
    row_spec = pl.BlockSpec((Q_BLOCK, width), lambda s: (block(s), 0))
    vec_specs = [pl.BlockSpec(v.shape, lambda s: (0, 0)) for v in vecs]
    cast_specs, cast_shapes = _cast_specs(casts, 1, n_prompt)
    return pl.pallas_call(
        _with_casts(functools.partial(body, n_prompt_steps=n_prompt), 3 + len(vecs), len(casts)),
        grid=(1 + n_blocks,),
        in_specs=[
            row_spec,
            pl.BlockSpec((None, CONV_UNITS, halo, width),
                         lambda s: (jnp.clip(s - 1 - n_prompt, 0, n_sample - 1), 0, 0, 0)),
            pl.BlockSpec((Q_BLOCK, width), lambda s: (block(s), gate_col_block)),
        ] + vec_specs + cast_specs,
        out_specs=[row_spec] + cast_specs,
        out_shape=[jax.ShapeDtypeStruct((rows, width), BF16)] + cast_shapes,
        scratch_shapes=[pltpu.VMEM((halo + Q_BLOCK, width), F32)] + scratch,
        compiler_params=_params(("arbitrary",)),
        name=name,
    )(x, state, gate, *vecs, *casts)


def _pad_rows_front(a, rows):
    return jnp.pad(a, ((0, 0), (rows - a.shape[1], 0), (0, 0)))


def kernel(x_prompt, x_sample, cache_a_k, cache_a_v, cache_a_kidx, state_b_conv, state_c_conv, meta,
           e_norm_pre, e_w_in, e_kidx_ln_g, e_kidx_ln_b, e_bconv_w, e_bconv_b, e_bln_g, e_bln_b,
           e_w_out, e_norm_post, o_norm_pre, o_w_in, o_conv_w, o_conv_b, o_w_out, o_norm_post):
    nb, s_len, d = x_prompt.shape
    ndb, nds, _ = x_sample.shape
    past = cache_a_k.shape[2]
    bw = d // 2
    assert nb == 1 and e_w_in.shape[0] == 1 and o_w_in.shape[0] == 1
    assert nds == SEQ_TILE and s_len % Q_BLOCK == 0 and (ndb * nds) % Q_BLOCK == 0
    assert s_len >= B_CONV - 1 and A_WIDTH % bw == 0 and bw % V7X_LANES == 0
    ns = ndb * nds
    meta_row = s_len + ns
    rows = meta_row + Q_BLOCK
    topk_p = min(TOPK_MAX, s_len // 4)
    topk_s = min(TOPK_MAX, (past + nds) // 4)

    sources = (x_prompt[0], x_sample.reshape(ns, d), meta)

    w_in = e_w_in[0].T.astype(BF16)
    widths = (A_WIDTH, A_KV_WIDTH, A_KV_WIDTH, IDX_Q_WIDTH, IDX_DIM, IDX_HEADS, bw, bw, A_WIDTH + bw)
    starts = [sum(widths[:i]) for i in range(len(widths))]
    w_q, w_k, w_v, w_qi, w_ki, w_wi, w_val, w_glu, w_gate = [
        _WeightView(w_in, 0, d, s, max(w, V7X_LANES), transposed=True) for s, w in zip(starts, widths)]

    h0, = _pool_rowwise_call(_prenorm_kernel, *sources, [], [e_norm_pre[0]], [BF16], "even_prenorm")
    q_s, q_i = _matmul_call([h0], [w_q, w_qi], [(0, 0), (0, 1)], _ep_q_qi, [BF16, BF16], bn=512,
                            name="even_q_qidx")
    k32, v32, k16, v16 = _kv_call(h0, w_k, w_v, "even_kv")
    ki32, ki16, wi = _matmul_call([h0], [w_ki, w_wi], [(0, 0), (0, 1)], _ep_kidx, [F32, BF16, F32],
                                  bn=V7X_LANES, auxs=[e_kidx_ln_g[0], e_kidx_ln_b[0]], name="even_kidx")
    u, = _matmul_call([h0], [w_val, w_glu], [(0, 0), (0, 1)], _ep_glu, [F32], bn=512, name="even_glu")
    sg, = _matmul_call([h0], [w_gate], [(0, 0)], _ep_silu, [F32], bn=1024, name="even_gate")

    lpad_p = _round_up(Q_BLOCK + s_len, KEY_BLOCK) + KEY_BLOCK
    seq = lambda a: jnp.concatenate(
        [a[meta_row:meta_row + Q_BLOCK], a[:s_len],
         jnp.zeros((lpad_p - Q_BLOCK - s_len, a.shape[1]), a.dtype)], axis=0)
    oa, w_out, w_out_odd = _prompt_attn_call(
        q_s, q_i, wi, sg, seq(ki16), seq(k16), seq(v16), s_len=s_len,
        meta_block=meta_row // Q_BLOCK, topk=topk_p, casts=(e_w_out[0], o_w_out[0]))
    w_out_a = _WeightView(w_out, 0, A_WIDTH, 0, d)
    w_out_b = _WeightView(w_out, A_WIDTH, bw, 0, d)
    oa = _sample_attn_call(q_s, q_i, wi, sg, cache_a_kidx[0],
                           cache_a_k[0].reshape(ndb, past * A_KV_HEADS, A_HEAD_DIM),
                           cache_a_v[0].reshape(ndb, past * A_KV_HEADS, A_HEAD_DIM), ki16, k16, v16, oa,
                           s_len=s_len, topk=topk_s)

    conv_steps = dict(n_prompt=s_len // Q_BLOCK, n_sample=ns // Q_BLOCK)
    bconv_w = jnp.pad(e_bconv_w[0], ((0, B_HALO - B_CONV), (0, 0)))
    ob, w_odd = _stream_conv_call(
        _bconv_kernel, u, _pad_rows_front(state_b_conv[0], B_HALO), sg, A_WIDTH // bw,
        [bconv_w, e_bconv_b[0].reshape(1, bw), e_bln_g[0].reshape(1, bw), e_bln_b[0].reshape(1, bw)],
        halo=B_HALO, scratch=[pltpu.VMEM((SEQ_TILE, bw), F32)], name="conformer_conv",
        casts=(o_w_in[0],), **conv_steps)

    y0, = _matmul_call([oa, ob], [w_out_a, w_out_b], [(0, 0), (1, 1)], _ep_sum, [BF16],
                       bn=1024, name="even_out")
    x1, h1 = _pool_rowwise_call(_postnorm_prenorm_kernel, *sources, [y0],
                                [e_norm_post[0], o_norm_pre[0]], [F32, BF16], "even_postnorm_odd_prenorm")

    w_hx, w_cg, w_bg, w_og = [_WeightView(w_odd, 0, d, i * d, d) for i in range(4)]
    z, bgs = _matmul_call([h1], [w_hx, w_cg, w_bg, w_og], [(0, 0), (0, 1), (0, 2), (0, 3)], _ep_odd,
                          [F32, BF16], bn=256, name="odd_in")
    cconv_w = jnp.pad(o_conv_w[0], ((0, C_HALO - C_CONV), (0, 0)))
    o1, = _stream_conv_call(
        _cconv_kernel, z, _pad_rows_front(state_c_conv[0], C_HALO), bgs, 0,
        [cconv_w, o_conv_b[0].reshape(1, d)],
        halo=C_HALO, scratch=[], name="short_conv", **conv_steps)
    y1, = _matmul_call([o1], [_WeightView(w_out_odd, 0, d, 0, d)], [(0, 0)],
                       _ep_identity, [BF16], bn=1024, name="odd_out")
    y_prompt = _final_postnorm_call(x1, y1, o_norm_post[0], 0, s_len // Q_BLOCK, "odd_postnorm_prompt")
    y_sample = _final_postnorm_call(x1, y1, o_norm_post[0], s_len // Q_BLOCK, ns // Q_BLOCK,
                                    "odd_postnorm_sample")

    def prompt_rows(a, per_frame=1):
        return jnp.concatenate([a[meta_row * per_frame:(meta_row + N_META) * per_frame],
                                a[:s_len * per_frame]], axis=0)

    def sample_rows(a, per_frame=1):
        return a[s_len * per_frame:meta_row * per_frame].reshape(ndb, nds * per_frame, a.shape[1])

    kv_shape = (A_KV_HEADS, A_HEAD_DIM)
    y_prompt = y_prompt[None]
    y_sample = y_sample.reshape(ndb, nds, d)
    prompt_a_k = prompt_rows(k32, A_KV_HEADS).reshape(1, 1, N_META + s_len, *kv_shape)
    prompt_a_v = prompt_rows(v32, A_KV_HEADS).reshape(1, 1, N_META + s_len, *kv_shape)
    prompt_a_kidx = prompt_rows(ki32)[None, None]
    prompt_b_conv = u[s_len - (B_CONV - 1):s_len][None, None]
    prompt_c_conv = z[s_len - (C_CONV - 1):s_len][None, None]
    sample_a_k = sample_rows(k32, A_KV_HEADS).reshape(1, ndb, nds, *kv_shape)
    sample_a_v = sample_rows(v32, A_KV_HEADS).reshape(1, ndb, nds, *kv_shape)
    sample_a_kidx = sample_rows(ki32)[None]
    sample_b_conv = sample_rows(u)[None, :, nds - (B_CONV - 1):]
    sample_c_conv = sample_rows(z)[None, :, nds - (C_CONV - 1):]
    return (y_prompt, y_sample, prompt_a_k, prompt_a_v, prompt_a_kidx, prompt_b_conv, prompt_c_conv,
            sample_a_k, sample_a_v, sample_a_kidx, sample_b_conv, sample_c_conv)
```

```python
import functools
from typing import NamedTuple

import jax
import jax.numpy as jnp
from jax import lax
from jax.experimental import pallas as pl
from jax.experimental.pallas import tpu as pltpu

CHUNK = 64
N_META = 16
EPS = 1e-6
A_HEADS = 16
A_KV_HEADS = 4
A_HEAD_DIM = 128
A_WIDTH = A_HEADS * A_HEAD_DIM
A_KV_WIDTH = A_KV_HEADS * A_HEAD_DIM
A_GROUP = A_HEADS // A_KV_HEADS
IDX_HEADS = 16
IDX_DIM = 128
IDX_Q_WIDTH = IDX_HEADS * IDX_DIM
INDEX_SCALE = (IDX_HEADS * IDX_DIM) ** -0.5
TOPK_MAX = 256
B_CONV = 31
C_CONV = 3

V7X_LANES = 128
V7X_SUBLANES = 8
V7X_VMEM_LIMIT_BYTES = 56 * 1024 * 1024

Q_BLOCK = 128
KEY_BLOCK = 512
COUNT_ROWS = 512
SEARCH_GROUP = 4
SOFTMAX_ROWS = 256
SEQ_TILE = 64
B_HALO = 32
C_HALO = 8
INT_MIN = -2 ** 31
MASKED = -1e30
LOG2_E = 1.4426950408889634

F32 = jnp.float32
BF16 = jnp.bfloat16
I32 = jnp.int32


def _round_up(x, m):
    return (x + m - 1) // m * m


def _row_tile(rows, cap):
    best = 16
    for t in range(16, cap + 1, 16):
        if rows % t == 0:
            best = t
    return best


def _params(sem):
    return pltpu.CompilerParams(dimension_semantics=sem, vmem_limit_bytes=V7X_VMEM_LIMIT_BYTES)


def _rms(x, g):
    return x * lax.rsqrt(jnp.mean(x * x, axis=-1, keepdims=True) + EPS) * g


def _gather_pool_block(xp_ref, xs_ref, meta_ref, x_scr, n_prompt, n_sample):
    i = pl.program_id(0)

    @pl.when(i < n_prompt)
    def _():
        x_scr[...] = xp_ref[...]

    @pl.when((i >= n_prompt) & (i < n_prompt + n_sample))
    def _():
        x_scr[...] = xs_ref[...]

    @pl.when(i == n_prompt + n_sample)
    def _():
        x_scr[0:N_META, :] = meta_ref[...]
        x_scr[N_META:, :] = jnp.zeros((Q_BLOCK - N_META, x_scr.shape[1]), F32)


def _prenorm_kernel(xp_ref, xs_ref, meta_ref, g_ref, h_ref, x_scr, *, n_prompt, n_sample):
    _gather_pool_block(xp_ref, xs_ref, meta_ref, x_scr, n_prompt, n_sample)
    h_ref[...] = _rms(x_scr[...], g_ref[...]).astype(h_ref.dtype)


def _postnorm_prenorm_kernel(xp_ref, xs_ref, meta_ref, y_ref, gpost_ref, gpre_ref, xo_ref, h_ref, x_scr,
                             *, n_prompt, n_sample):
    _gather_pool_block(xp_ref, xs_ref, meta_ref, x_scr, n_prompt, n_sample)
    x1 = x_scr[...] + _rms(y_ref[...].astype(F32), gpost_ref[...])
    xo_ref[...] = x1
    h_ref[...] = _rms(x1, gpre_ref[...]).astype(h_ref.dtype)


def _postnorm_kernel(x_ref, y_ref, gpost_ref, xo_ref):
    xo_ref[...] = x_ref[...] + _rms(y_ref[...].astype(F32), gpost_ref[...])


def _pool_rowwise_call(body, x_prompt, x_sample, meta, pool_inputs, vec_inputs, out_dtypes, name):
    s_len, d = x_prompt.shape
    n_prompt, n_sample = s_len // Q_BLOCK, x_sample.shape[0] // Q_BLOCK
    n_blocks = n_prompt + n_sample + 1
    pool_spec = pl.BlockSpec((Q_BLOCK, d), lambda i: (i, 0))
    vec_spec = pl.BlockSpec((1, d), lambda i: (0, 0))
    return pl.pallas_call(
        functools.partial(body, n_prompt=n_prompt, n_sample=n_sample),
        grid=(n_blocks,),
        in_specs=[
            pl.BlockSpec((Q_BLOCK, d), lambda i: (jnp.minimum(i, n_prompt - 1), 0)),
            pl.BlockSpec((Q_BLOCK, d), lambda i: (jnp.clip(i - n_prompt, 0, n_sample - 1), 0)),
            pl.BlockSpec((N_META, d), lambda i: (0, 0)),
        ] + [pool_spec] * len(pool_inputs) + [vec_spec] * len(vec_inputs),
        out_specs=[pool_spec] * len(out_dtypes),
        out_shape=[jax.ShapeDtypeStruct((n_blocks * Q_BLOCK, d), dt) for dt in out_dtypes],
        scratch_shapes=[pltpu.VMEM((Q_BLOCK, d), F32)],
        compiler_params=_params(("arbitrary",)),
        name=name,
    )(x_prompt, x_sample, meta, *pool_inputs, *[v.reshape(1, d) for v in vec_inputs])


def _final_postnorm_call(x1, y1, g, first_block, n_blocks, name):
    d = x1.shape[1]
    per_step = max(p for p in (1, 2, 4) if first_block % p == 0 and n_blocks % p == 0)
    bm, first, steps = per_step * Q_BLOCK, first_block // per_step, n_blocks // per_step
    in_spec = pl.BlockSpec((bm, d), lambda i: (i + first, 0))
    return pl.pallas_call(
        _postnorm_kernel,
        grid=(steps,),
        in_specs=[in_spec, in_spec, pl.BlockSpec((1, d), lambda i: (0, 0))],
        out_specs=pl.BlockSpec((bm, d), lambda i: (i, 0)),
        out_shape=jax.ShapeDtypeStruct((n_blocks * Q_BLOCK, d), F32),
        compiler_params=_params(("parallel",)),
        name=name,
    )(x1, y1, g.reshape(1, d))


def _matmul_kernel(*refs, n_x, n_w, n_aux, terms, epilogue, transposed):
    xs = refs[:n_x]
    ws = refs[n_x:n_x + n_w]
    auxs = refs[n_x + n_w:n_x + n_w + n_aux]
    outs = refs[n_x + n_w + n_aux:]
    accs = [lax.dot_general(xs[a][...], ws[b][...],
                            (((1,), (1 if transposed[b] else 0,)), ((), ())),
                            preferred_element_type=F32) for a, b in terms]
    results = epilogue(*accs, *[r[...] for r in auxs])
    for o_ref, r in zip(outs, results):
        o_ref[...] = r.astype(o_ref.dtype)


class _WeightView(NamedTuple):
    array: jax.Array
    row0: int
    rows: int
    col0: int
    cols: int
    transposed: bool = False


def _weight_spec(w, bn):
    if w.transposed:
        return pl.BlockSpec(
            (pl.Element(bn), pl.Element(w.rows)),
            lambda i, j: (pl.multiple_of(w.col0 + j * bn, 2 * V7X_SUBLANES), w.row0))
    return pl.BlockSpec((w.rows, bn), lambda i, j: (w.row0 // w.rows, w.col0 // bn + j))


def _matmul_call(xs, ws, terms, epilogue, out_dtypes, *, bn, auxs=(), name):
    rows = xs[0].shape[0]
    n = ws[0].cols
    bm = _row_tile(rows, 1536)
    bn = max(t for t in range(V7X_LANES, min(bn, n) + 1, V7X_LANES)
             if n % t == 0 and all(w.transposed or w.col0 % t == 0 for w in ws))
    assert all(w.cols == n and w.row0 % w.rows == 0 for w in ws)
    assert all(w.col0 % (2 * V7X_SUBLANES) == 0 for w in ws if w.transposed)
    in_specs = [pl.BlockSpec((bm, x.shape[1]), lambda i, j: (i, 0)) for x in xs]
    in_specs += [_weight_spec(w, bn) for w in ws]
    in_specs += [pl.BlockSpec((1, bn), lambda i, j: (0, j)) for _ in auxs]
    out_spec = pl.BlockSpec((bm, bn), lambda i, j: (i, j))
    body = functools.partial(_matmul_kernel, n_x=len(xs), n_w=len(ws), n_aux=len(auxs),
                             terms=terms, epilogue=epilogue,
                             transposed=tuple(w.transposed for w in ws))
    return pl.pallas_call(
        body,
        grid=(rows // bm, n // bn),
        in_specs=in_specs,
        out_specs=[out_spec] * len(out_dtypes),
        out_shape=[jax.ShapeDtypeStruct((rows, n), dt) for dt in out_dtypes],
        compiler_params=_params(("parallel", "arbitrary")),
        name=name,
    )(*xs, *[w.array for w in ws], *[a.reshape(1, n) for a in auxs])


def _kv_kernel(x_ref, wk_ref, wv_ref, k32_ref, v32_ref, k16_ref, v16_ref):
    bm = x_ref.shape[0]
    for w_ref, o32_ref, o16_ref in ((wk_ref, k32_ref, k16_ref), (wv_ref, v32_ref, v16_ref)):
        acc = lax.dot_general(x_ref[...], w_ref[...], (((1,), (1,)), ((), ())),
                              preferred_element_type=F32)
        o16_ref[...] = acc.astype(o16_ref.dtype)
        for g in range(A_KV_HEADS):
            o32_ref[pl.ds(g, bm, stride=A_KV_HEADS), :] = acc[:, g * A_HEAD_DIM:(g + 1) * A_HEAD_DIM]


def _kv_call(x, w_k, w_v, name):
    rows = x.shape[0]
    bm = _row_tile(rows, 1536)
    assert w_k.transposed and w_v.transposed and w_k.cols == A_KV_WIDTH == w_v.cols
    head_rows = pl.BlockSpec((bm * A_KV_HEADS, A_HEAD_DIM), lambda i, j: (i, 0))
    wide = pl.BlockSpec((bm, A_KV_WIDTH), lambda i, j: (i, 0))
    return pl.pallas_call(
        _kv_kernel,
        grid=(rows // bm, 1),
        in_specs=[pl.BlockSpec((bm, x.shape[1]), lambda i, j: (i, 0)),
                  _weight_spec(w_k, A_KV_WIDTH), _weight_spec(w_v, A_KV_WIDTH)],
        out_specs=[head_rows, head_rows, wide, wide],
        out_shape=[jax.ShapeDtypeStruct((rows * A_KV_HEADS, A_HEAD_DIM), F32)] * 2
        + [jax.ShapeDtypeStruct((rows, A_KV_WIDTH), BF16)] * 2,
        compiler_params=_params(("parallel", "arbitrary")),
        name=name,
    )(x, w_k.array, w_v.array)


def _silu(x):
    return x * jax.nn.sigmoid(x)


def _ep_q_qi(q, qi):
    return q * (A_HEAD_DIM ** -0.5 * LOG2_E), qi


def _ep_kidx(ki, wi, g, b):
    kc = ki - jnp.mean(ki, axis=-1, keepdims=True)
    kn = kc * lax.rsqrt(jnp.mean(kc * kc, axis=-1, keepdims=True) + EPS) * g + b
    return kn, kn, wi * INDEX_SCALE


def _ep_glu(val, glu):
    return (val * jax.nn.sigmoid(glu),)


def _ep_silu(gate):
    return (_silu(gate),)


def _ep_sum(a, b):
    return (a + b,)


def _ep_identity(a):
    return (a,)


def _ep_odd(hx, cg, bg, gate):
    return cg * hx, bg * _silu(gate)


def _prompt_attn_kernel(qs_ref, qi_ref, wi_ref, sg_ref, ki_ref, k_ref, v_ref, o_ref, *scratch,
                        topk, n_prompt_blocks):
    step = pl.program_id(0)

    @pl.when(step <= n_prompt_blocks)
    def _():
        frame = (step - 1) * Q_BLOCK + lax.broadcasted_iota(I32, (1, Q_BLOCK), 1)
        limit = jnp.where(step == 0, 0, Q_BLOCK + CHUNK * (frame // CHUNK + 1))
        n_kt = jnp.where(step == 0, 1, pl.cdiv(Q_BLOCK * (step + 1), KEY_BLOCK))
        _attend(qs_ref, qi_ref, wi_ref, sg_ref, ki_ref, k_ref, v_ref, *scratch,
                topk=topk, n_lo=N_META, limit=limit, n_kt=n_kt)
        o_ref[...] = scratch[-1][...].astype(o_ref.dtype)

    @pl.when(step > n_prompt_blocks)
    def _():
        o_ref[...] = jnp.zeros(o_ref.shape, o_ref.dtype)


def _sample_attn_kernel(qs_ref, qi_ref, wi_ref, sg_ref, cki_ref, ck_ref, cv_ref, nki_ref, nk_ref, nv_ref,
                        pool_ref, o_ref, ki_buf, k_buf, v_buf, *scratch, topk):
    del pool_ref
    step = pl.program_id(0)
    past, new = cki_ref.shape[0], nk_ref.shape[0]
    for buf, cache_ref, new_ref in ((ki_buf, cki_ref, nki_ref), (k_buf, ck_ref, nk_ref), (v_buf, cv_ref, nv_ref)):
        heads = cache_ref.shape[0] // past
        for r in range(0, past, KEY_BLOCK):
            n = min(KEY_BLOCK, past - r)
            for g in range(heads):
                buf[r:r + n, g * A_HEAD_DIM:(g + 1) * A_HEAD_DIM] = cache_ref[
                    pl.ds(r * heads + g, n, stride=heads), :].astype(BF16)
        buf[past:past + new, :] = new_ref[...]
        buf[past + new:, :] = jnp.zeros((buf.shape[0] - past - new, buf.shape[1]), BF16)
    q0 = pl.multiple_of((step % (Q_BLOCK // new)) * new, new)
    _attend(qs_ref, qi_ref, wi_ref, sg_ref, ki_buf, k_buf, v_buf, *scratch,
            topk=topk, n_lo=past + new, limit=jnp.zeros((1, Q_BLOCK), I32),
            n_kt=ki_buf.shape[0] // KEY_BLOCK - 1, q0=q0, hq=new)
    o_ref[...] = scratch[-1][...].astype(o_ref.dtype)


def _attend(qs_ref, qi_ref, wi_ref, sg_ref, ki_ref, k_ref, v_ref,
            qrows, qirows, keys, bias, s_a, s_b, p_scr, m_scr, l_scr, acc_scr, o_scr,
            *, topk, n_lo, limit, n_kt, q0=0, hq=Q_BLOCK):
    nq = Q_BLOCK
    lb = KEY_BLOCK
    fold = nq // hq
    assert fold in (1, 2)

    @pl.when(pl.program_id(0) == 0)
    def _():
        one_hot = (lax.broadcasted_iota(I32, (hq, nq), 0) == lax.broadcasted_iota(I32, (hq, nq), 1))
        for h in range(A_HEADS):
            qrows[h * hq:(h + 1) * hq, A_HEAD_DIM:A_HEAD_DIM + nq] = one_hot.astype(BF16)

    for h in range(A_HEADS):
        qrows[h * hq:(h + 1) * hq, 0:A_HEAD_DIM] = qs_ref[pl.ds(q0, hq), h * A_HEAD_DIM:(h + 1) * A_HEAD_DIM]
    for h in range(IDX_HEADS):
        qirows[h * hq:(h + 1) * hq, :] = qi_ref[pl.ds(q0, hq), h * IDX_DIM:(h + 1) * IDX_DIM]
    w_t = jnp.concatenate([wi_ref[pl.ds(q0, hq), :]] * fold, axis=0).T
    lane_head = lax.broadcasted_iota(I32, (1, nq), 1) // hq
    w_tiles = []
    for j in range(IDX_HEADS // fold):
        w_tile = w_t[j * fold:j * fold + 1, :]
        for f in range(1, fold):
            w_tile = jnp.where(lane_head == f, w_t[j * fold + f:j * fold + f + 1, :], w_tile)
        w_tiles.append(w_tile)

    ib = 2 * lb

    def score_block(kt, carry):
        r0 = pl.multiple_of(kt * ib, ib)
        d = lax.dot_general(ki_ref[pl.ds(r0, ib), :], qirows[...], (((1,), (1,)), ((), ())),
                            preferred_element_type=F32)
        sc = jnp.zeros((ib, nq), F32)
        for j, w_tile in enumerate(w_tiles):
            sc = sc + jnp.maximum(d[:, j * nq:(j + 1) * nq], 0.0) * w_tile
        for f in range(1, fold):
            sc = sc + pltpu.roll(sc, f * hq, axis=1)
        pos = r0 + lax.broadcasted_iota(I32, (ib, nq), 0)
        adm = (pos < n_lo) | ((pos >= Q_BLOCK) & (pos < limit))
        bits = lax.bitcast_convert_type(sc, I32)
        key = bits ^ ((bits >> 31) & 0x7FFFFFFF)
        keys[pl.ds(r0, ib), :] = jnp.where(adm, key, INT_MIN)
        return carry

    lax.fori_loop(0, (n_kt + 1) // 2, score_block, 0)

    n_ct = n_kt * (lb // COUNT_ROWS)

    def count_where(pred):
        def body(c, acc):
            r0 = pl.multiple_of(c * COUNT_ROWS, COUNT_ROWS)
            pos = r0 + lax.broadcasted_iota(I32, (COUNT_ROWS, nq), 0)
            hit = jnp.where(pred(keys[pl.ds(r0, COUNT_ROWS), :], pos), 1.0, 0.0)
            parts = [hit[i * V7X_SUBLANES:(i + 1) * V7X_SUBLANES, :]
                     for i in range(COUNT_ROWS // V7X_SUBLANES)]
            while len(parts) > 1:
                parts = [a + b for a, b in zip(parts[::2], parts[1::2])]
            return acc + parts[0]
        acc = lax.fori_loop(0, n_ct, body, jnp.zeros((V7X_SUBLANES, nq), F32))
        return jnp.sum(acc, axis=0, keepdims=True)

    def search_bit(i, state):
        prefix, n_sel = state
        trial = prefix | lax.shift_left(jnp.int32(1), 31 - i)
        cand = trial ^ INT_MIN
        n_trial = count_where(lambda k, pos: k >= cand)
        take = n_trial >= float(topk)
        return jnp.where(take, trial, prefix), jnp.where(take, n_trial, n_sel)

    def search_bits(state):
        group, prefix, n_sel = state
        prefix, n_sel = lax.fori_loop(
            group * SEARCH_GROUP, (group + 1) * SEARCH_GROUP, search_bit, (prefix, n_sel))
        return group + 1, prefix, n_sel

    def search_open(state):
        group, _, n_sel = state
        return (group < 32 // SEARCH_GROUP) & (jnp.max(n_sel) > float(topk))

    n_admissible = (n_lo + jnp.maximum(limit - Q_BLOCK, 0)).astype(F32)
    _, prefix, n_sel = lax.while_loop(search_open, search_bits,
                                      (jnp.int32(0), jnp.zeros((1, nq), I32), n_admissible))
    thr = jnp.maximum(prefix ^ INT_MIN, INT_MIN + 1)

    def write_bias(selected):
        def body(c, carry):
            r0 = pl.multiple_of(c * COUNT_ROWS, COUNT_ROWS)
            pos = r0 + lax.broadcasted_iota(I32, (COUNT_ROWS, nq), 0)
            bias[pl.ds(r0, COUNT_ROWS), :] = jnp.where(
                selected(keys[pl.ds(r0, COUNT_ROWS), :], pos), 0.0, MASKED).astype(BF16)
            return carry
        lax.fori_loop(0, n_ct, body, 0)

    surplus = jnp.max(n_sel) > float(topk)

    @pl.when(jnp.logical_not(surplus))
    def _():
        write_bias(lambda k, pos: k >= thr)

    @pl.when(surplus)
    def _():
        keep = float(topk) - count_where(lambda k, pos: k > thr)
        position_bits = keys.shape[0].bit_length()

        def cut_bit(i, prefix):
            trial = prefix | lax.shift_left(jnp.int32(1), position_bits - 1 - i)
            admitted = count_where(lambda k, pos: (k == thr) & (pos < trial))
            return jnp.where(admitted < keep, trial, prefix)
        cut = lax.fori_loop(0, position_bits, cut_bit, jnp.zeros((1, nq), I32)) + 1
        write_bias(lambda k, pos: (k > thr) | ((k == thr) & (pos < cut)))

    gw = A_GROUP * hq
    m_scr[...] = jnp.full(m_scr.shape, MASKED, F32)
    l_scr[...] = jnp.zeros(l_scr.shape, F32)
    acc_scr[...] = jnp.zeros(acc_scr.shape, F32)

    bias[pl.ds(pl.multiple_of(n_kt * lb, lb), lb), :] = jnp.full((lb, nq), MASKED, BF16)

    def score_keys(kt, s_ref, g):
        r0 = pl.multiple_of(kt * lb, lb)
        k_aug = jnp.concatenate(
            [k_ref[pl.ds(r0, lb), g * A_HEAD_DIM:(g + 1) * A_HEAD_DIM], bias[pl.ds(r0, lb), :]], axis=1)
        s_ref[g] = lax.dot_general(k_aug, qrows[g * gw:(g + 1) * gw, :], (((1,), (1,)), ((), ())),
                                   preferred_element_type=F32)

    def softmax_pv(kt, s_ref, g):
        chunks = [slice(c * SOFTMAX_ROWS, (c + 1) * SOFTMAX_ROWS) for c in range(lb // SOFTMAX_ROWS)]
        m_old = m_scr[g]
        m_new = m_old
        for rows in chunks:
            m_new = jnp.maximum(m_new, jnp.max(s_ref[g, rows, :], axis=0, keepdims=True))
        alpha = jnp.exp2(m_old - m_new)
        p_sum = jnp.zeros_like(m_new)
        for rows in chunks:
            p = jnp.exp2(s_ref[g, rows, :] - m_new)
            p_sum = p_sum + jnp.sum(p, axis=0, keepdims=True)
            p_scr[g, rows, :] = p.astype(BF16)
        l_scr[g] = l_scr[g] * alpha + p_sum
        v_blk = v_ref[pl.ds(pl.multiple_of(kt * lb, lb), lb), g * A_HEAD_DIM:(g + 1) * A_HEAD_DIM]
        acc_scr[g] = acc_scr[g] * alpha + lax.dot_general(
            v_blk, p_scr[g], (((0,), (0,)), ((), ())), preferred_element_type=F32)
        m_scr[g] = m_new

    for g in range(A_KV_HEADS):
        score_keys(0, s_a, g)

    def attend_pair(j, carry):
        kt = 2 * j
        for g in range(A_KV_HEADS):
            score_keys(kt + 1, s_b, g)
        for g in range(A_KV_HEADS):
            softmax_pv(kt, s_a, g)
        for g in range(A_KV_HEADS):
            score_keys(jnp.minimum(kt + 2, n_kt), s_a, g)
        for g in range(A_KV_HEADS):
            softmax_pv(kt + 1, s_b, g)
        return carry

    lax.fori_loop(0, n_kt // 2, attend_pair, 0)

    @pl.when(n_kt % 2 == 1)
    def _():
        for g in range(A_KV_HEADS):
            softmax_pv(n_kt - 1, s_a, g)

    for g in range(A_KV_HEADS):
        o_g = (acc_scr[g] / l_scr[g]).T
        for r in range(A_GROUP):
            h = g * A_GROUP + r
            cols = slice(h * A_HEAD_DIM, (h + 1) * A_HEAD_DIM)
            o_scr[:, cols] = o_g[r * hq:(r + 1) * hq, :] * sg_ref[pl.ds(q0, hq), cols]


def _attn_scratch(lpad, hq=Q_BLOCK):
    gw = A_GROUP * hq
    return [
        pltpu.VMEM((A_HEADS * hq, A_HEAD_DIM + Q_BLOCK), BF16),
        pltpu.VMEM((IDX_HEADS * hq, IDX_DIM), BF16),
        pltpu.VMEM((lpad, Q_BLOCK), I32),
        pltpu.VMEM((lpad, Q_BLOCK), BF16),
        pltpu.VMEM((A_KV_HEADS, KEY_BLOCK, gw), F32),
        pltpu.VMEM((A_KV_HEADS, KEY_BLOCK, gw), F32),
        pltpu.VMEM((A_KV_HEADS, KEY_BLOCK, gw), BF16),
        pltpu.VMEM((A_KV_HEADS, 1, gw), F32),
        pltpu.VMEM((A_KV_HEADS, 1, gw), F32),
        pltpu.VMEM((A_KV_HEADS, A_HEAD_DIM, gw), F32),
        pltpu.VMEM((hq, A_WIDTH), F32),
    ]


def _query_specs(q_index):
    qspec = pl.BlockSpec((Q_BLOCK, A_WIDTH), lambda s: (q_index(s), 0))
    return [qspec, qspec, pl.BlockSpec((Q_BLOCK, V7X_LANES), lambda s: (q_index(s), 0)), qspec]


def _prompt_attn_call(qs, qi, wi, sg, ki_seq, k_seq, v_seq, *, s_len, meta_block, topk, casts=()):
    lpad = ki_seq.shape[0]
    n_qb = s_len // Q_BLOCK
    full = lambda a: pl.BlockSpec(a.shape, lambda s: (0, 0))
    cast_specs, cast_shapes = _cast_specs(casts, 1, n_qb)
    body = functools.partial(_prompt_attn_kernel, topk=topk, n_prompt_blocks=n_qb)
    return pl.pallas_call(
        _with_casts(body, 7, len(casts)),
        grid=(1 + meta_block,),
        in_specs=_query_specs(lambda s: jnp.where(s == 0, meta_block, s - 1))
        + [full(ki_seq), full(k_seq), full(v_seq)] + cast_specs,
        out_specs=[pl.BlockSpec((Q_BLOCK, A_WIDTH), lambda s: (jnp.where(s == 0, meta_block, s - 1), 0))]
        + cast_specs,
        out_shape=[jax.ShapeDtypeStruct(((meta_block + 1) * Q_BLOCK, A_WIDTH), BF16)] + cast_shapes,
        scratch_shapes=_attn_scratch(lpad),
        compiler_params=_params(("arbitrary",)),
        name="prompt_attention",
    )(qs, qi, wi, sg, ki_seq, k_seq, v_seq, *casts)


def _sample_attn_call(qs, qi, wi, sg, cache_ki, cache_k, cache_v, ki16, k16, v16, pool_out,
                      *, s_len, topk):
    ndb, past, _ = cache_ki.shape
    new = SEQ_TILE
    lpad = _round_up(past + new, KEY_BLOCK) + KEY_BLOCK
    cache_spec = lambda a: pl.BlockSpec((None,) + a.shape[1:], lambda s: (s, 0, 0))
    new_spec = lambda a: pl.BlockSpec((new, a.shape[1]), lambda s: (s_len // new + s, 0))
    return pl.pallas_call(
        functools.partial(_sample_attn_kernel, topk=topk),
        grid=(ndb,),
        in_specs=_query_specs(lambda s: s_len // Q_BLOCK + s // (Q_BLOCK // new))
        + [cache_spec(cache_ki), cache_spec(cache_k), cache_spec(cache_v),
           new_spec(ki16), new_spec(k16), new_spec(v16), pl.BlockSpec(memory_space=pl.ANY)],
        out_specs=pl.BlockSpec((new, A_WIDTH), lambda s: (s_len // new + s, 0)),
        out_shape=jax.ShapeDtypeStruct(pool_out.shape, BF16),
        input_output_aliases={10: 0},
        scratch_shapes=[pltpu.VMEM((lpad, IDX_DIM), BF16), pltpu.VMEM((lpad, A_KV_WIDTH), BF16),
                        pltpu.VMEM((lpad, A_KV_WIDTH), BF16)] + _attn_scratch(lpad, new),
        compiler_params=_params(("arbitrary",)),
        name="sample_attention",
    )(qs, qi, wi, sg, cache_ki, cache_k, cache_v, ki16, k16, v16, pool_out)


CONV_UNITS = Q_BLOCK // SEQ_TILE


def _stream_conv_step(n_prompt_steps, halo, ext, state_ref, cur_ref, unit_fn):
    step = pl.program_id(0)
    is_sample = step > n_prompt_steps

    @pl.when(step == 0)
    def _():
        ext[0:halo, :] = jnp.zeros((halo, ext.shape[1]), F32)

    ext[halo:halo + Q_BLOCK, :] = cur_ref[...]
    for u in range(CONV_UNITS):
        @pl.when(is_sample)
        def _():
            ext[u * SEQ_TILE:u * SEQ_TILE + halo, :] = state_ref[u]

        unit_fn(u)

    @pl.when(step == 0)
    def _():
        if halo > N_META:
            ext[0:halo - N_META, :] = jnp.zeros((halo - N_META, ext.shape[1]), F32)
        keep = min(halo, N_META)
        ext[halo - keep:halo, :] = ext[halo + N_META - keep:halo + N_META, :]

    @pl.when(step > 0)
    def _():
        ext[0:halo, :] = ext[Q_BLOCK:Q_BLOCK + halo, :]


def _causal_taps(ext, unit, w_ref, halo, width, cols):
    first = halo - (width - 1)
    x = ext[unit * SEQ_TILE:unit * SEQ_TILE + halo + SEQ_TILE, cols]
    rows = x.shape[0]
    rotated = {0: x}
    for j in range(width):
        r = (first + j) % V7X_SUBLANES
        if r not in rotated:
            rotated[r] = pltpu.roll(x, rows - r, axis=0)
    acc = None
    for j in range(width):
        r = (first + j) % V7X_SUBLANES
        base = first + j - r
        term = rotated[r][base:base + SEQ_TILE, :] * w_ref[j:j + 1, cols]
        acc = term if acc is None else acc + term
    return acc


def _bconv_kernel(u_ref, st_ref, sg_ref, w_ref, cb_ref, g_ref, b_ref, o_ref, ext, y_scr,
                  *, n_prompt_steps):
    width = u_ref.shape[1]

    def unit_fn(unit):
        rows = slice(unit * SEQ_TILE, (unit + 1) * SEQ_TILE)
        total = jnp.zeros((SEQ_TILE, V7X_LANES), F32)
        for c in range(width // V7X_LANES):
            cols = slice(c * V7X_LANES, (c + 1) * V7X_LANES)
            y = _causal_taps(ext, unit, w_ref, B_HALO, B_CONV, cols) + cb_ref[:, cols]
            y_scr[:, cols] = y
            total = total + y
        mean = jnp.sum(total, axis=-1, keepdims=True) / width
        sq = jnp.zeros((SEQ_TILE, V7X_LANES), F32)
        for c in range(width // V7X_LANES):
            cols = slice(c * V7X_LANES, (c + 1) * V7X_LANES)
            yc = y_scr[:, cols] - mean
            sq = sq + yc * yc
        inv = lax.rsqrt(jnp.sum(sq, axis=-1, keepdims=True) / width + EPS)
        for c in range(width // V7X_LANES):
            cols = slice(c * V7X_LANES, (c + 1) * V7X_LANES)
            yn = (y_scr[:, cols] - mean) * inv * g_ref[:, cols] + b_ref[:, cols]
            o_ref[rows, cols] = (_silu(yn) * sg_ref[rows, cols]).astype(o_ref.dtype)

    _stream_conv_step(n_prompt_steps, B_HALO, ext, st_ref, u_ref, unit_fn)


def _cconv_kernel(z_ref, st_ref, bgs_ref, w_ref, cb_ref, o_ref, ext, *, n_prompt_steps):
    width = z_ref.shape[1]

    def unit_fn(unit):
        rows = slice(unit * SEQ_TILE, (unit + 1) * SEQ_TILE)
        for c in range(width // V7X_LANES):
            cols = slice(c * V7X_LANES, (c + 1) * V7X_LANES)
            y = _causal_taps(ext, unit, w_ref, C_HALO, C_CONV, cols) + cb_ref[:, cols]
            o_ref[rows, cols] = (bgs_ref[rows, cols] * y).astype(o_ref.dtype)

    _stream_conv_step(n_prompt_steps, C_HALO, ext, st_ref, z_ref, unit_fn)


def _cast_specs(weights, first_step, n_steps):
    specs, shapes = [], []
    for w in weights:
        slab = w.shape[0] // n_steps
        assert w.shape[0] % n_steps == 0 and slab % (2 * V7X_SUBLANES) == 0
        specs.append(pl.BlockSpec((slab, w.shape[1]), lambda s: (jnp.clip(s - first_step, 0, n_steps - 1), 0)))
        shapes.append(jax.ShapeDtypeStruct(w.shape, BF16))
    return specs, shapes


def _with_casts(body, n_inputs, n_casts):
    def kernel_body(*refs):
        sources = refs[n_inputs:n_inputs + n_casts]
        results = refs[n_inputs + n_casts + 1:n_inputs + 2 * n_casts + 1]
        for src, dst in zip(sources, results):
            dst[...] = src[...].astype(dst.dtype)
        body(*refs[:n_inputs], refs[n_inputs + n_casts], *refs[n_inputs + 2 * n_casts + 1:])
    return kernel_body


def _stream_conv_call(body, x, state, gate, gate_col_block, vecs, *, halo, n_prompt, n_sample,
                      scratch, name, casts=()):
    rows, width = x.shape
    n_blocks = n_prompt + n_sample
    assert rows == (n_blocks + 1) * Q_BLOCK
    assert state.shape[0] == n_sample * CONV_UNITS
    state = state.reshape(n_sample, CONV_UNITS, halo, width)
    block = lambda s: jnp.where(s == 0, n_blocks, s - 1)
    row_spec = pl.BlockSpec((Q_BLOCK, width), lambda s: (block(s), 0))
    vec_specs = [pl.BlockSpec(v.shape, lambda s: (0, 0)) for v in vecs]
    cast_specs, cast_shapes = _cast_specs(casts, 1, n_prompt)
    return pl.pallas_call(
        _with_casts(functools.partial(body, n_prompt_steps=n_prompt), 3 + len(vecs), len(casts)),
        grid=(1 + n_blocks,),
        in_specs=[
            row_spec,
            pl.BlockSpec((None, CONV_UNITS, halo, width),
                         lambda s: (jnp.clip(s - 1 - n_prompt, 0, n_sample - 1), 0, 0, 0)),
            pl.BlockSpec((Q_BLOCK, width), lambda s: (block(s), gate_col_block)),
        ] + vec_specs + cast_specs,
        out_specs=[row_spec] + cast_specs,
        out_shape=[jax.ShapeDtypeStruct((rows, width), BF16)] + cast_shapes,
        scratch_shapes=[pltpu.VMEM((halo + Q_BLOCK, width), F32)] + scratch,
        compiler_params=_params(("arbitrary",)),
        name=name,
    )(x, state, gate, *vecs, *casts)


def _pad_rows_front(a, rows):
    return jnp.pad(a, ((0, 0), (rows - a.shape[1], 0), (0, 0)))


def kernel(x_prompt, x_sample, cache_a_k, cache_a_v, cache_a_kidx, state_b_conv, state_c_conv, meta,
           e_norm_pre, e_w_in, e_kidx_ln_g, e_kidx_ln_b, e_bconv_w, e_bconv_b, e_bln_g, e_bln_b,
           e_w_out, e_norm_post, o_norm_pre, o_w_in, o_conv_w, o_conv_b, o_w_out, o_norm_post):
    nb, s_len, d = x_prompt.shape
    ndb, nds, _ = x_sample.shape
    past = cache_a_k.shape[2]
    bw = d // 2
    assert nb == 1 and e_w_in.shape[0] == 1 and o_w_in.shape[0] == 1
    assert nds == SEQ_TILE and s_len % Q_BLOCK == 0 and (ndb * nds) % Q_BLOCK == 0
    assert s_len >= B_CONV - 1 and A_WIDTH % bw == 0 and bw % V7X_LANES == 0
    ns = ndb * nds
    meta_row = s_len + ns
    rows = meta_row + Q_BLOCK
    topk_p = min(TOPK_MAX, s_len // 4)
    topk_s = min(TOPK_MAX, (past + nds) // 4)

    sources = (x_prompt[0], x_sample.reshape(ns, d), meta)

    w_in = e_w_in[0].T.astype(BF16)
    widths = (A_WIDTH, A_KV_WIDTH, A_KV_WIDTH, IDX_Q_WIDTH, IDX_DIM, IDX_HEADS, bw, bw, A_WIDTH + bw)
    starts = [sum(widths[:i]) for i in range(len(widths))]
    w_q, w_k, w_v, w_qi, w_ki, w_wi, w_val, w_glu, w_gate = [
        _WeightView(w_in, 0, d, s, max(w, V7X_LANES), transposed=True) for s, w in zip(starts, widths)]

    h0, = _pool_rowwise_call(_prenorm_kernel, *sources, [], [e_norm_pre[0]], [BF16], "even_prenorm")
    q_s, q_i = _matmul_call([h0], [w_q, w_qi], [(0, 0), (0, 1)], _ep_q_qi, [BF16, BF16], bn=512,
                            name="even_q_qidx")
    k32, v32, k16, v16 = _kv_call(h0, w_k, w_v, "even_kv")
    ki32, ki16, wi = _matmul_call([h0], [w_ki, w_wi], [(0, 0), (0, 1)], _ep_kidx, [F32, BF16, F32],
                                  bn=V7X_LANES, auxs=[e_kidx_ln_g[0], e_kidx_ln_b[0]], name="even_kidx")
    u, = _matmul_call([h0], [w_val, w_glu], [(0, 0), (0, 1)], _ep_glu, [F32], bn=512, name="even_glu")
    sg, = _matmul_call([h0], [w_gate], [(0, 0)], _ep_silu, [F32], bn=1024, name="even_gate")

    lpad_p = _round_up(Q_BLOCK + s_len, KEY_BLOCK) + KEY_BLOCK
    seq = lambda a: jnp.concatenate(
        [a[meta_row:meta_row + Q_BLOCK], a[:s_len],
         jnp.zeros((lpad_p - Q_BLOCK - s_len, a.shape[1]), a.dtype)], axis=0)
    oa, w_out, w_out_odd = _prompt_attn_call(
        q_s, q_i, wi, sg, seq(ki16), seq(k16), seq(v16), s_len=s_len,
        meta_block=meta_row // Q_BLOCK, topk=topk_p, casts=(e_w_out[0], o_w_out[0]))
    w_out_a = _WeightView(w_out, 0, A_WIDTH, 0, d)
    w_out_b = _WeightView(w_out, A_WIDTH, bw, 0, d)
    oa = _sample_attn_call(q_s, q_i, wi, sg, cache_a_kidx[0],
                           cache_a_k[0].reshape(ndb, past * A_KV_HEADS, A_HEAD_DIM),
                           cache_a_v[0].reshape(ndb, past * A_KV_HEADS, A_HEAD_DIM), ki16, k16, v16, oa,
                           s_len=s_len, topk=topk_s)

    conv_steps = dict(n_prompt=s_len // Q_BLOCK, n_sample=ns // Q_BLOCK)
    bconv_w = jnp.pad(e_bconv_w[0], ((0, B_HALO - B_CONV), (0, 0)))
    ob, w_odd = _stream_conv_call(
        _bconv_kernel, u, _pad_rows_front(state_b_conv[0], B_HALO), sg, A_WIDTH // bw,
        [bconv_w, e_bconv_b[0].reshape(1, bw), e_bln_g[0].reshape(1, bw), e_bln_b[0].reshape(1, bw)],
        halo=B_HALO, scratch=[pltpu.VMEM((SEQ_TILE, bw), F32)], name="conformer_conv",
        casts=(o_w_in[0],), **conv_steps)

    y0, = _matmul_call([oa, ob], [w_out_a, w_out_b], [(0, 0), (1, 1)], _ep_sum, [BF16],
                       bn=1024, name="even_out")
    x1, h1 = _pool_rowwise_call(_postnorm_prenorm_kernel, *sources, [y0],
                                [e_norm_post[0], o_norm_pre[0]], [F32, BF16], "even_postnorm_odd_prenorm")

    w_hx, w_cg, w_bg, w_og = [_WeightView(w_odd, 0, d, i * d, d) for i in range(4)]
    z, bgs = _matmul_call([h1], [w_hx, w_cg, w_bg, w_og], [(0, 0), (0, 1), (0, 2), (0, 3)], _ep_odd,
                          [F32, BF16], bn=256, name="odd_in")
    cconv_w = jnp.pad(o_conv_w[0], ((0, C_HALO - C_CONV), (0, 0)))
    o1, = _stream_conv_call(
        _cconv_kernel, z, _pad_rows_front(state_c_conv[0], C_HALO), bgs, 0,
        [cconv_w, o_conv_b[0].reshape(1, d)],
        halo=C_HALO, scratch=[], name="short_conv", **conv_steps)
    y1, = _matmul_call([o1], [_WeightView(w_out_odd, 0, d, 0, d)], [(0, 0)],
                       _ep_identity, [BF16], bn=1024, name="odd_out")
    y_prompt = _final_postnorm_call(x1, y1, o_norm_post[0], 0, s_len // Q_BLOCK, "odd_postnorm_prompt")
    y_sample = _final_postnorm_call(x1, y1, o_norm_post[0], s_len // Q_BLOCK, ns // Q_BLOCK,
                                    "odd_postnorm_sample")

    def prompt_rows(a, per_frame=1):
        return jnp.concatenate([a[meta_row * per_frame:(meta_row + N_META) * per_frame],
                                a[:s_len * per_frame]], axis=0)

    def sample_rows(a, per_frame=1):
        return a[s_len * per_frame:meta_row * per_frame].reshape(ndb, nds * per_frame, a.shape[1])

    kv_shape = (A_KV_HEADS, A_HEAD_DIM)
    y_prompt = y_prompt[None]
    y_sample = y_sample.reshape(ndb, nds, d)
    prompt_a_k = prompt_rows(k32, A_KV_HEADS).reshape(1, 1, N_META + s_len, *kv_shape)
    prompt_a_v = prompt_rows(v32, A_KV_HEADS).reshape(1, 1, N_META + s_len, *kv_shape)
    prompt_a_kidx = prompt_rows(ki32)[None, None]
    prompt_b_conv = u[s_len - (B_CONV - 1):s_len][None, None]
    prompt_c_conv = z[s_len - (C_CONV - 1):s_len][None, None]
    sample_a_k = sample_rows(k32, A_KV_HEADS).reshape(1, ndb, nds, *kv_shape)
    sample_a_v = sample_rows(v32, A_KV_HEADS).reshape(1, ndb, nds, *kv_shape)
    sample_a_kidx = sample_rows(ki32)[None]
    sample_b_conv = sample_rows(u)[None, :, nds - (B_CONV - 1):]
    sample_c_conv = sample_rows(z)[None, :, nds - (C_CONV - 1):]
    return (y_prompt, y_sample, prompt_a_k, prompt_a_v, prompt_a_kidx, prompt_b_conv, prompt_c_conv,
            sample_a_k, sample_a_v, sample_a_kidx, sample_b_conv, sample_c_conv)
```
